```python
import jax, jax.numpy as jnp
from jax import lax
import numpy as np

D_MODEL = 1024
BATCH = 1
SEQ = 16384
DEPTH = 4
DEC_BATCH = 32
DEC_SEQ = 16
PAST_LEN = 1024

CHUNK = 64
D_PLE = 256
A_HEADS = 8
A_KV_HEADS = 2
A_HEAD_DIM = 64
A_GROUP = A_HEADS // A_KV_HEADS
A_WIDTH = A_HEADS * A_HEAD_DIM
A_KV_WIDTH = A_KV_HEADS * A_HEAD_DIM
WINDOW = 128
WINDOW_CHUNKS = WINDOW // CHUNK
ROT_DIM = A_HEAD_DIM // 4
ROPE_THETA = 500000.0
B_HEADS = 4
B_KEY_DIM = 64
B_VAL_DIM = 128
B_QK_WIDTH = B_HEADS * B_KEY_DIM
B_WIDTH = B_HEADS * B_VAL_DIM
GATE_RANK = 16
GATE_TAU = 16.0
GLA_BLOCK = 16
D_MIX = A_WIDTH + B_WIDTH
IN_SPLITS = (A_WIDTH, A_KV_WIDTH, A_KV_WIDTH, A_WIDTH, B_QK_WIDTH, B_QK_WIDTH, B_WIDTH, B_WIDTH, GATE_RANK)
IN_COLS = sum(IN_SPLITS)
NEG_INF = -1e30

kernel_name = "hymba_swa_sink_gla_streaming_step"


def _split_points():
    pts, acc = [], 0
    for n in IN_SPLITS[:-1]:
        acc += n
        pts.append(acc)
    return pts


def rms_norm(x, g, eps=1e-6):
    xf = x.astype(jnp.float32)
    y = xf * lax.rsqrt(jnp.mean(xf * xf, axis=-1, keepdims=True) + eps)
    return (y * g.astype(jnp.float32)).astype(x.dtype)


def partial_rope(x, pos):
    half = ROT_DIM // 2
    inv = jnp.power(jnp.float32(ROPE_THETA), -jnp.arange(half, dtype=jnp.float32) * (2.0 / ROT_DIM))
    ang = pos.astype(jnp.float32)[:, None] * inv[None, :]
    cos = jnp.cos(ang)[None, :, None, :]
    sin = jnp.sin(ang)[None, :, None, :]
    xr = x[..., :ROT_DIM].astype(jnp.float32)
    x1, x2 = xr[..., :half], xr[..., half:]
    rot = jnp.concatenate([x1 * cos - x2 * sin, x2 * cos + x1 * sin], axis=-1).astype(x.dtype)
    return jnp.concatenate([rot, x[..., ROT_DIM:]], axis=-1)


def sink_softmax_av(s, valid, sink, v, eq):
    if valid is not None:
        s = jnp.where(valid, s, NEG_INF)
    sk = sink.astype(jnp.float32)[:, :, None, None]
    m = jnp.maximum(jnp.max(s, axis=-1, keepdims=True), sk)
    pr = jnp.exp(s - m)
    den = jnp.sum(pr, axis=-1, keepdims=True) + jnp.exp(sk - m)
    return jnp.einsum(eq, pr / den, v.astype(jnp.float32))


def swa_prompt(q, k, v, sink):
    bn, L = q.shape[0], q.shape[1]
    nc = L // CHUNK
    band = (WINDOW_CHUNKS + 1) * CHUNK
    qc = q.reshape(bn, nc, CHUNK, A_KV_HEADS, A_GROUP, A_HEAD_DIM).astype(jnp.float32)
    pad = ((0, 0), (WINDOW_CHUNKS * CHUNK, 0), (0, 0), (0, 0))
    kp = jnp.pad(k, pad).reshape(bn, nc + WINDOW_CHUNKS, CHUNK, A_KV_HEADS, A_HEAD_DIM)
    vp = jnp.pad(v, pad).reshape(bn, nc + WINDOW_CHUNKS, CHUNK, A_KV_HEADS, A_HEAD_DIM)
    kb = jnp.concatenate([kp[:, j:j + nc] for j in range(WINDOW_CHUNKS + 1)], axis=2)
    vb = jnp.concatenate([vp[:, j:j + nc] for j in range(WINDOW_CHUNKS + 1)], axis=2)
    key_chunk = jnp.arange(nc)[:, None] - WINDOW_CHUNKS + (jnp.arange(band) // CHUNK)[None, :]
    valid = (key_chunk >= 0).reshape(1, nc, 1, 1, 1, band)
    s = jnp.einsum('bnqkgd,bnskd->bnkgqs', qc, kb.astype(jnp.float32)) * (A_HEAD_DIM ** -0.5)
    o = sink_softmax_av(s, valid, sink.reshape(A_KV_HEADS, A_GROUP), vb, 'bnkgqs,bnskd->bnqkgd')
    return o.reshape(bn, L, A_HEADS, A_HEAD_DIM).astype(q.dtype)


def swa_sample(q, k, v, ck, cv, sink):
    bn, T = q.shape[0], q.shape[1]
    kk = jnp.concatenate([ck.astype(k.dtype), k], axis=1).astype(jnp.float32)
    vv = jnp.concatenate([cv.astype(v.dtype), v], axis=1)
    qg = q.reshape(bn, T, A_KV_HEADS, A_GROUP, A_HEAD_DIM).astype(jnp.float32)
    s = jnp.einsum('bqkgd,bskd->bkgqs', qg, kk) * (A_HEAD_DIM ** -0.5)
    o = sink_softmax_av(s, None, sink.reshape(A_KV_HEADS, A_GROUP), vv, 'bkgqs,bskd->bqkgd')
    return o.reshape(bn, T, A_HEADS, A_HEAD_DIM).astype(q.dtype)


def gla_recurrent(q, k, v, logg, s0):
    bn, L, H, DK = q.shape
    DV = v.shape[-1]
    Lp = -(-L // GLA_BLOCK) * GLA_BLOCK
    padw = ((0, 0), (0, Lp - L), (0, 0), (0, 0))
    nb = Lp // GLA_BLOCK
    f = lambda a: jnp.pad(a.astype(jnp.float32), padw).reshape(bn, nb, GLA_BLOCK, H, a.shape[-1])
    qf, kf, vf, gf = f(q), f(k), f(v), f(logg)
    b = jnp.cumsum(gf, axis=2)
    b_last = b[:, :, -1:]
    qt = qf * jnp.exp(b)
    kt = kf * jnp.exp(-b)
    ke = kf * jnp.exp(b_last - b)
    causal = jnp.tril(jnp.ones((GLA_BLOCK, GLA_BLOCK), dtype=bool))
    att = jnp.einsum('bnthk,bnshk->bnhts', qt, kt)
    o_intra = jnp.einsum('bnhts,bnshv->bnthv', jnp.where(causal, att, 0.0), vf)

    def step(S, xs):
        qt_n, ke_n, v_n, dec_n = xs
        o = jnp.einsum('bthk,bhkv->bthv', qt_n, S)
        S = dec_n[..., None] * S + jnp.einsum('bthk,bthv->bhkv', ke_n, v_n)
        return S, o

    xs = (jnp.moveaxis(qt, 1, 0), jnp.moveaxis(ke, 1, 0), jnp.moveaxis(vf, 1, 0),
          jnp.moveaxis(jnp.exp(b_last[:, :, 0]), 1, 0))
    s_fin, o_inter = lax.scan(step, s0.astype(jnp.float32), xs)
    o = (o_intra + jnp.moveaxis(o_inter, 0, 1)).reshape(bn, Lp, H, DV)[:, :L]
    return o.astype(v.dtype), s_fin.astype(s0.dtype)


def trunk_layer(h, p_l, pos, ck, cv, s0, norm_g, w_in, q_norm_g, k_norm_g, sinks, w_gate_up, b_gate,
                gla_norm_g, w_out, pe_norm_g, w_pe, w_pg):
    bn, L = h.shape[0], h.shape[1]
    xn = rms_norm(h, norm_g)
    u = xn @ w_in
    qa, ka, va, ga, qb, kb, vb, gb, ab = jnp.split(u, _split_points(), axis=-1)
    qa = partial_rope(rms_norm(qa.reshape(bn, L, A_HEADS, A_HEAD_DIM), q_norm_g), pos)
    ka = partial_rope(rms_norm(ka.reshape(bn, L, A_KV_HEADS, A_HEAD_DIM), k_norm_g), pos)
    va = va.reshape(bn, L, A_KV_HEADS, A_HEAD_DIM)
    if ck is None:
        oa = swa_prompt(qa, ka, va, sinks)
        new_k, new_v = ka[:, L - WINDOW:], va[:, L - WINDOW:]
    else:
        oa = swa_sample(qa, ka, va, ck, cv, sinks)
        new_k, new_v = ka, va
    qb = qb.reshape(bn, L, B_HEADS, B_KEY_DIM) * (B_KEY_DIM ** -0.5)
    kb = kb.reshape(bn, L, B_HEADS, B_KEY_DIM)
    vb = vb.reshape(bn, L, B_HEADS, B_VAL_DIM)
    logg = jax.nn.log_sigmoid((ab @ w_gate_up + b_gate).astype(jnp.float32)) / GATE_TAU
    logg = logg.reshape(bn, L, B_HEADS, B_KEY_DIM)
    if s0 is None:
        s0 = jnp.zeros((bn, B_HEADS, B_KEY_DIM, B_VAL_DIM), h.dtype)
    ob, s_new = gla_recurrent(qb, kb, vb, logg, s0)
    ob = rms_norm(ob, gla_norm_g)
    mix = jnp.concatenate([oa.reshape(bn, L, A_WIDTH) * jax.nn.silu(ga),
                           ob.reshape(bn, L, B_WIDTH) * jax.nn.silu(gb)], axis=-1)
    h = h + mix @ w_out
    gate = jax.nn.sigmoid(rms_norm(h, pe_norm_g) @ w_pg)
    h = h + gate * (p_l @ w_pe)
    return h, new_k, new_v, s_new


def setup_inputs(seed: int = 0) -> dict:
    key = jax.random.key(seed)
    ks = jax.random.split(key, 20)
    f32 = jnp.float32
    nrm = lambda k, shape, s: jax.random.normal(k, shape, f32) * s
    swa_rows = min(WINDOW, PAST_LEN)
    return {
        "x_prompt": nrm(ks[0], (BATCH, SEQ, D_MODEL), 1.0),
        "x_sample": nrm(ks[1], (DEC_BATCH, DEC_SEQ, D_MODEL), 1.0),
        "cache_k": nrm(ks[2], (DEPTH, DEC_BATCH, swa_rows, A_KV_HEADS, A_HEAD_DIM), 1.0),
        "cache_v": nrm(ks[3], (DEPTH, DEC_BATCH, swa_rows, A_KV_HEADS, A_HEAD_DIM), 1.0),
        "state_gla": nrm(ks[4], (DEPTH, DEC_BATCH, B_HEADS, B_KEY_DIM, B_VAL_DIM), 0.5),
        "p_prompt": nrm(ks[5], (DEPTH, BATCH, SEQ, D_PLE), 1.0),
        "p_sample": nrm(ks[6], (DEPTH, DEC_BATCH, DEC_SEQ, D_PLE), 1.0),
        "norm_g": 1.0 + nrm(ks[7], (DEPTH, D_MODEL), 0.02),
        "w_in": nrm(ks[8], (DEPTH, D_MODEL, IN_COLS), D_MODEL ** -0.5),
        "q_norm_g": 1.0 + nrm(ks[9], (DEPTH, A_HEAD_DIM), 0.02),
        "k_norm_g": 1.0 + nrm(ks[10], (DEPTH, A_HEAD_DIM), 0.02),
        "sinks": nrm(ks[11], (DEPTH, A_HEADS), 0.5),
        "w_gate_up": nrm(ks[12], (DEPTH, GATE_RANK, B_QK_WIDTH), GATE_RANK ** -0.5),
        "b_gate": nrm(ks[13], (DEPTH, B_QK_WIDTH), 0.1),
        "gla_norm_g": 1.0 + nrm(ks[14], (DEPTH, B_VAL_DIM), 0.02),
        "w_out": nrm(ks[15], (DEPTH, D_MIX, D_MODEL), D_MIX ** -0.5),
        "pe_norm_g": 1.0 + nrm(ks[16], (DEPTH, D_MODEL), 0.02),
        "w_pe": nrm(ks[17], (DEPTH, D_PLE, D_MODEL), D_PLE ** -0.5),
        "w_pg": nrm(ks[18], (DEPTH, D_MODEL, D_MODEL), D_MODEL ** -0.5),
    }


def reference(x_prompt, x_sample, cache_k, cache_v, state_gla, p_prompt, p_sample, norm_g, w_in,
              q_norm_g, k_norm_g, sinks, w_gate_up, b_gate, gla_norm_g, w_out, pe_norm_g, w_pe, w_pg):
    pos_prompt = jnp.arange(x_prompt.shape[1], dtype=jnp.int32)
    pos_sample = PAST_LEN + jnp.arange(x_sample.shape[1], dtype=jnp.int32)
    hp, hs = x_prompt, x_sample
    kp_l, vp_l, sp_l, ks_l, vs_l, ss_l = [], [], [], [], [], []
    for i in range(DEPTH):
        lw = (norm_g[i], w_in[i], q_norm_g[i], k_norm_g[i], sinks[i], w_gate_up[i], b_gate[i],
              gla_norm_g[i], w_out[i], pe_norm_g[i], w_pe[i], w_pg[i])
        hp, kn, vn, sn = trunk_layer(hp, p_prompt[i], pos_prompt, None, None, None, *lw)
        kp_l.append(kn); vp_l.append(vn); sp_l.append(sn)
        hs, kn, vn, sn = trunk_layer(hs, p_sample[i], pos_sample, cache_k[i], cache_v[i], state_gla[i], *lw)
        ks_l.append(kn); vs_l.append(vn); ss_l.append(sn)
    return (hp, hs, jnp.stack(kp_l), jnp.stack(vp_l), jnp.stack(sp_l),
            jnp.stack(ks_l), jnp.stack(vs_l), jnp.stack(ss_l))
```

```python
import functools

import jax
import jax.numpy as jnp
from jax import lax
from jax.experimental import pallas as pl
from jax.experimental.pallas import tpu as pltpu

F32 = jnp.float32
BF16 = jnp.bfloat16

D_MODEL = 1024
DEPTH = 4
PAST_LEN = 1024
CHUNK = 64
D_PLE = 256
A_HEADS = 8
A_KV_HEADS = 2
A_HEAD_DIM = 64
A_GROUP = A_HEADS // A_KV_HEADS
A_WIDTH = A_HEADS * A_HEAD_DIM
A_KV_WIDTH = A_KV_HEADS * A_HEAD_DIM
WINDOW = 128
ROT_DIM = A_HEAD_DIM // 4
ROPE_THETA = 500000.0
B_HEADS = 4
B_KEY_DIM = 64
B_VAL_DIM = 128
B_QK_WIDTH = B_HEADS * B_KEY_DIM
B_WIDTH = B_HEADS * B_VAL_DIM
GATE_RANK = 16
GATE_TAU = 16.0
NEG_INF = -1e30
EPS = 1e-6

LANES = 128

C_QA = 0
C_KA = C_QA + A_WIDTH
C_VA = C_KA + A_KV_WIDTH
C_GA = C_VA + A_KV_WIDTH
C_QB = C_GA + A_WIDTH
C_KB = C_QB + B_QK_WIDTH
C_VB = C_KB + B_QK_WIDTH
C_GB = C_VB + B_WIDTH
C_AB = C_GB + B_WIDTH
IN_COLS_PAD = C_AB + LANES

PROMPT_TILE = 256
SAMPLE_BATCH_TILE = 8
VMEM_LIMIT_BYTES = 56 * 1024 * 1024


def _dot(a, b):
    return jnp.dot(a, b, preferred_element_type=F32)


def _dot_nt(a, b):
    return lax.dot_general(a, b, (((1,), (1,)), ((), ())), preferred_element_type=F32)


def _dot_tn(a, b):
    return lax.dot_general(a, b, (((0,), (0,)), ((), ())), preferred_element_type=F32)


def _split_bf16(x):
    hi = x.astype(BF16)
    lo = (x - hi.astype(F32)).astype(BF16)
    return hi, lo


def _rms(x, g):
    ms = jnp.mean(x * x, axis=-1, keepdims=True)
    return x * lax.rsqrt(ms + EPS) * g


def _sigmoid(x):
    return 1.0 / (1.0 + jnp.exp(-x))


def _silu(x):
    return x * _sigmoid(x)


def _log_sigmoid(x):
    return jnp.minimum(x, 0.0) - jnp.log(1.0 + jnp.exp(-jnp.abs(x)))


def _block_mask(n, block, lower):
    r = lax.broadcasted_iota(jnp.int32, (n, n), 0)
    c = lax.broadcasted_iota(jnp.int32, (n, n), 1)
    m = (r // block) == (c // block)
    if lower:
        m = m & (r >= c)
    return jnp.where(m, 1.0, 0.0).astype(BF16)


def _project_a(xn, rope_ref, win_ref, gqk_ref):
    qk = _dot(xn, win_ref[:, C_QA:C_VA])
    seg = _block_mask(256, A_HEAD_DIM, lower=False)
    hi, lo = _split_bf16(qk * qk)
    parts = []
    for c0 in range(0, C_VA, 256):
        c1 = min(c0 + 256, C_VA)
        sg = seg[: c1 - c0, : c1 - c0]
        parts.append(_dot(hi[:, c0:c1], sg) + _dot(lo[:, c0:c1], sg))
    ss = jnp.concatenate(parts, axis=1)
    qkn = qk * lax.rsqrt(ss * (1.0 / A_HEAD_DIM) + EPS) * gqk_ref[...]
    cos = rope_ref[:, 0:LANES]
    sin_a = rope_ref[:, LANES:2 * LANES]
    sin_b = rope_ref[:, 2 * LANES:3 * LANES]
    slabs = []
    for s in range(C_VA // LANES):
        x = qkn[:, s * LANES:(s + 1) * LANES]
        up = pltpu.roll(x, LANES - ROT_DIM // 2, 1)
        dn = pltpu.roll(x, ROT_DIM // 2, 1)
        slabs.append(x * cos + up * sin_a + dn * sin_b)
    q = jnp.concatenate(slabs[:A_WIDTH // LANES], axis=1)
    k = slabs[A_WIDTH // LANES]
    v = _dot(xn, win_ref[:, C_VA:C_GA])
    return q, k, v


def _kv_operands(k, v, lo_half):
    ksw = pltpu.roll(k, A_HEAD_DIM, 1)
    vsw = pltpu.roll(v, A_HEAD_DIM, 1)
    one = jnp.ones_like(v)
    kd = (jnp.where(lo_half, k, ksw).astype(BF16), jnp.where(lo_half, ksw, k).astype(BF16))
    vx = (jnp.concatenate([jnp.where(lo_half, v, one), jnp.where(lo_half, one, vsw)], axis=1).astype(BF16),
          jnp.concatenate([jnp.where(lo_half, vsw, one), jnp.where(lo_half, one, v)], axis=1).astype(BF16))
    return kd, vx


def _attn_block(q_blk, j, kd, vx, bias, sink_ref, lo_half):
    rows = q_blk.shape[0]
    a = q_blk[:, (2 * j) * LANES:(2 * j + 1) * LANES]
    b = q_blk[:, (2 * j + 1) * LANES:(2 * j + 2) * LANES]
    z = jnp.zeros_like(a)
    stack = jnp.concatenate([jnp.where(lo_half, a, z), jnp.where(lo_half, z, a),
                             jnp.where(lo_half, b, z), jnp.where(lo_half, z, b)], axis=0).astype(BF16)
    s = _dot_nt(stack, kd)
    if bias is not None:
        s = s + bias
    sk = jnp.concatenate([jnp.full((rows, 1), sink_ref[A_GROUP * j + g], F32) for g in range(A_GROUP)], axis=0)
    m = jnp.maximum(jnp.max(s, axis=-1, keepdims=True), sk)
    p = jnp.exp(s - m)
    r = _dot(p.astype(BF16), vx)
    de = jnp.exp(sk - m)
    r_lo, r_hi = r[:, :LANES], r[:, LANES:]

    def slab(g0):
        e = slice(g0 * rows, (g0 + 1) * rows)
        o = slice((g0 + 1) * rows, (g0 + 2) * rows)
        return jnp.where(lo_half, r_lo[e] / (r_hi[e] + de[e]), r_hi[o] / (r_lo[o] + de[o]))

    return slab(0), slab(2)


def _gla_inputs(xn, win_ref, wgu_ref, bg_ref, block):
    rows = xn.shape[0]
    qb = _dot(xn, win_ref[:, C_QB:C_KB])
    kb = _dot(xn, win_ref[:, C_KB:C_VB])
    vb = _dot(xn, win_ref[:, C_VB:C_GB])
    ab = _dot(xn, win_ref[:, C_AB:IN_COLS_PAD])
    logg = _log_sigmoid(_dot(ab.astype(BF16), wgu_ref[...]) + bg_ref[...]) * (1.0 / GATE_TAU)
    g_hi, g_lo = _split_bf16(logg)
    tri = _block_mask(rows, block, lower=True)
    ones = _block_mask(rows, block, lower=False)
    b = _dot(tri, g_hi) + _dot(tri, g_lo)
    b_last = _dot(ones, g_hi) + _dot(ones, g_lo)
    qt = qb * (jnp.exp(b) * (B_KEY_DIM ** -0.5))
    kt = kb * jnp.exp(-b)
    ke = kb * jnp.exp(b_last - b)
    return qt, kt, ke, vb, b_last, logg


def _head_stack(x, head_mask):
    z = jnp.zeros_like(x)
    return jnp.concatenate([jnp.where(head_mask[h], x, z) for h in range(B_HEADS)], axis=0).astype(BF16)


def _finish_layer(h, oa, ob, xn, p, win_ref, glag_ref, wout_ref, peg_ref, wpe_ref, wpg_ref):
    ga = _dot(xn, win_ref[:, C_GA:C_QB])
    gb = _dot(xn, win_ref[:, C_GB:C_AB])
    obn = jnp.concatenate(
        [_rms(ob[:, hh * B_VAL_DIM:(hh + 1) * B_VAL_DIM], glag_ref[...]) for hh in range(B_HEADS)], axis=1)
    mix = jnp.concatenate([oa * _silu(ga), obn * _silu(gb)], axis=1).astype(BF16)
    h1 = h + _dot(mix, wout_ref[...])
    gate = _sigmoid(_dot(_rms(h1, peg_ref[...]).astype(BF16), wpg_ref[...]))
    return h1 + gate * _dot(p.astype(BF16), wpe_ref[...])


def _prompt_kernel(sink_ref, h_ref, p_ref, rope_ref, ng_ref, win_ref, gqk_ref, wgu_ref, bg_ref, glag_ref,
                   wout_ref, peg_ref, wpe_ref, wpg_ref,
                   ho_ref, nk_ref, nv_ref, st_ref,
                   kd_scr, vx_scr, stt_scr):
    tile = PROMPT_TILE
    n_chunks = tile // CHUNK
    band = WINDOW + CHUNK
    i = pl.program_id(0)

    @pl.when(i == 0)
    def _():
        kd_scr[:, 0:WINDOW, :] = jnp.zeros((A_KV_HEADS, WINDOW, LANES), BF16)
        vx_scr[:, 0:WINDOW, :] = jnp.zeros((A_KV_HEADS, WINDOW, 2 * LANES), BF16)
        stt_scr[...] = jnp.zeros_like(stt_scr)

    h = h_ref[...]
    xn = _rms(h, ng_ref[...]).astype(BF16)
    lo_half = lax.broadcasted_iota(jnp.int32, (1, LANES), 1) < A_HEAD_DIM

    q, k, v = _project_a(xn, rope_ref, win_ref, gqk_ref)
    nk_ref[...] = k[tile - WINDOW:, :]
    nv_ref[...] = v[tile - WINDOW:, :]
    kd, vx = _kv_operands(k, v, lo_half)
    for j in range(A_KV_HEADS):
        kd_scr[j, WINDOW:, :] = kd[j]
        vx_scr[j, WINDOW:, :] = vx[j]
    key_blk = lax.broadcasted_iota(jnp.int32, (1, band), 1) // CHUNK
    oa_rows = []
    for c in range(n_chunks):
        first_key_chunk = i * n_chunks + c - WINDOW // CHUNK
        bias = jnp.where(key_blk + first_key_chunk >= 0, 0.0, NEG_INF)
        q_blk = q[c * CHUNK:(c + 1) * CHUNK, :]
        slabs = []
        for j in range(A_KV_HEADS):
            slabs.extend(_attn_block(q_blk, j, kd_scr[j, c * CHUNK:c * CHUNK + band, :],
                                     vx_scr[j, c * CHUNK:c * CHUNK + band, :], bias, sink_ref, lo_half))
        oa_rows.append(jnp.concatenate(slabs, axis=1))
    oa = jnp.concatenate(oa_rows, axis=0)
    for j in range(A_KV_HEADS):
        kd_scr[j, 0:WINDOW, :] = kd_scr[j, tile:tile + WINDOW, :]
        vx_scr[j, 0:WINDOW, :] = vx_scr[j, tile:tile + WINDOW, :]

    qt, kt, ke, vb, b_last, _ = _gla_inputs(xn, win_ref, wgu_ref, bg_ref, CHUNK)
    head_of_lane = lax.broadcasted_iota(jnp.int32, (1, B_QK_WIDTH), 1) // B_KEY_DIM
    head_mask = [head_of_lane == hh for hh in range(B_HEADS)]
    causal = (lax.broadcasted_iota(jnp.int32, (CHUNK, CHUNK), 0)
              >= lax.broadcasted_iota(jnp.int32, (CHUNK, CHUNK), 1))
    stt = stt_scr[...]
    ob_rows = []
    for c in range(n_chunks):
        rs = slice(c * CHUNK, (c + 1) * CHUNK)
        q_stack = _head_stack(qt[rs], head_mask)
        att = _dot_nt(q_stack, kt[rs].astype(BF16))
        inter = _dot_nt(q_stack, stt.astype(BF16))
        vb_c = vb[rs].astype(BF16)
        outs = []
        for hh in range(B_HEADS):
            hr = slice(hh * CHUNK, (hh + 1) * CHUNK)
            a_h = jnp.where(causal, att[hr], 0.0).astype(BF16)
            outs.append(_dot(a_h, vb_c[:, hh * B_VAL_DIM:(hh + 1) * B_VAL_DIM]) + inter[hr])
        ob_rows.append(jnp.concatenate(outs, axis=1))
        upd_full = _dot_tn(vb_c, ke[rs].astype(BF16))
        upd = jnp.zeros_like(stt)
        for hh in range(B_HEADS):
            upd = upd + jnp.where(head_mask[hh], upd_full[hh * B_VAL_DIM:(hh + 1) * B_VAL_DIM], 0.0)
        stt = stt * jnp.exp(b_last[c * CHUNK:c * CHUNK + 1, :]) + upd
    ob = jnp.concatenate(ob_rows, axis=0)
    stt_scr[...] = stt
    st_ref[...] = stt.T

    ho_ref[...] = _finish_layer(h, oa, ob, xn, p_ref[...], win_ref, glag_ref, wout_ref, peg_ref,
                                wpe_ref, wpg_ref)


def _sample_kernel(sink_ref, h_ref, p_ref, rope_ref, ck_ref, cv_ref, s0_ref, ng_ref, win_ref, gqk_ref, wgu_ref,
                   bg_ref, glag_ref, wout_ref, peg_ref, wpe_ref, wpg_ref,
                   ho_ref, nk_ref, nv_ref, st_ref, *, dec_seq):
    nb = SAMPLE_BATCH_TILE
    h = h_ref[...]
    xn = _rms(h, ng_ref[...]).astype(BF16)
    lo_half = lax.broadcasted_iota(jnp.int32, (1, LANES), 1) < A_HEAD_DIM

    q, k, v = _project_a(xn, rope_ref, win_ref, gqk_ref)
    nk_ref[...] = k
    nv_ref[...] = v
    kd_new, vx_new = _kv_operands(k, v, lo_half)
    oa_rows = []
    for bi in range(nb):
        rs = slice(bi * dec_seq, (bi + 1) * dec_seq)
        kd_old, vx_old = _kv_operands(ck_ref[bi], cv_ref[bi], lo_half)
        slabs = []
        for j in range(A_KV_HEADS):
            kd = jnp.concatenate([kd_old[j], kd_new[j][rs]], axis=0)
            vx = jnp.concatenate([vx_old[j], vx_new[j][rs]], axis=0)
            slabs.extend(_attn_block(q[rs], j, kd, vx, None, sink_ref, lo_half))
        oa_rows.append(jnp.concatenate(slabs, axis=1))
    oa = jnp.concatenate(oa_rows, axis=0)

    qt, kt, ke, vb, _, g_all = _gla_inputs(xn, win_ref, wgu_ref, bg_ref, dec_seq)
    head_of_lane = lax.broadcasted_iota(jnp.int32, (1, B_QK_WIDTH), 1) // B_KEY_DIM
    head_mask = [head_of_lane == hh for hh in range(B_HEADS)]
    causal = (lax.broadcasted_iota(jnp.int32, (dec_seq, dec_seq), 0)
              >= lax.broadcasted_iota(jnp.int32, (dec_seq, dec_seq), 1))
    ones_rhs = jnp.ones((dec_seq, LANES), BF16)
    ob_rows = []
    for bi in range(nb):
        rs = slice(bi * dec_seq, (bi + 1) * dec_seq)
        s_stack = s0_ref[bi]
        q_stack = _head_stack(qt[rs], head_mask)
        att = _dot_nt(q_stack, kt[rs].astype(BF16))
        inter = _dot(q_stack, s_stack.astype(BF16))
        att = jnp.concatenate([jnp.where(causal, att[hh * dec_seq:(hh + 1) * dec_seq], 0.0)
                               for hh in range(B_HEADS)], axis=0).astype(BF16)
        vb_b = vb[rs].astype(BF16)
        intra = _dot(att, vb_b)
        ob_rows.append(jnp.concatenate(
            [intra[hh * dec_seq:(hh + 1) * dec_seq, hh * B_VAL_DIM:(hh + 1) * B_VAL_DIM]
             + inter[hh * dec_seq:(hh + 1) * dec_seq] for hh in range(B_HEADS)], axis=1))
        upd_full = _dot_tn(ke[rs].astype(BF16), vb_b)
        upd = jnp.concatenate(
            [upd_full[hh * B_KEY_DIM:(hh + 1) * B_KEY_DIM, hh * B_VAL_DIM:(hh + 1) * B_VAL_DIM]
             for hh in range(B_HEADS)], axis=0)
        g_hi, g_lo = _split_bf16(g_all[rs])
        total = _dot_tn(g_hi, ones_rhs) + _dot_tn(g_lo, ones_rhs)
        st_ref[bi] = s_stack * jnp.exp(total) + upd
    ob = jnp.concatenate(ob_rows, axis=0)

    ho_ref[...] = _finish_layer(h, oa, ob, xn, p_ref[...], win_ref, glag_ref, wout_ref, peg_ref,
                                wpe_ref, wpg_ref)


def _rope_table(pos):
    half = ROT_DIM // 2
    inv = jnp.power(jnp.float32(ROPE_THETA), -jnp.arange(half, dtype=F32) * (2.0 / ROT_DIM))
    ang = pos.astype(F32)[:, None] * inv[None, :]
    cos, sin = jnp.cos(ang), jnp.sin(ang)
    n = pos.shape[0]
    pad = A_HEAD_DIM - ROT_DIM
    c = jnp.concatenate([cos, cos, jnp.ones((n, pad), F32)], axis=1)
    sa = jnp.concatenate([-sin, jnp.zeros((n, half + pad), F32)], axis=1)
    sb = jnp.concatenate([jnp.zeros((n, half), F32), sin, jnp.zeros((n, pad), F32)], axis=1)
    return jnp.concatenate([c, c, sa, sa, sb, sb], axis=1)


def _const_spec(shape):
    return pl.BlockSpec(shape, lambda i: (0,) * len(shape))


def _weight_specs():
    return [
        _const_spec((1, D_MODEL)),
        _const_spec((D_MODEL, IN_COLS_PAD)),
        _const_spec((1, C_VA)),
        _const_spec((LANES, B_QK_WIDTH)),
        _const_spec((1, B_QK_WIDTH)),
        _const_spec((1, B_VAL_DIM)),
        _const_spec((A_WIDTH + B_WIDTH, D_MODEL)),
        _const_spec((1, D_MODEL)),
        _const_spec((D_PLE, D_MODEL)),
        _const_spec((D_MODEL, D_MODEL)),
    ]


def _prompt_layer(h, p, rope, sinks, weights):
    seq = h.shape[0]
    tile = PROMPT_TILE
    assert seq % tile == 0 and tile % CHUNK == 0 and tile >= WINDOW
    row = lambda w: pl.BlockSpec((tile, w), lambda i: (i, 0))
    return pl.pallas_call(
        _prompt_kernel,
        grid=(seq // tile,),
        in_specs=[pl.BlockSpec(memory_space=pltpu.SMEM), row(D_MODEL), row(D_PLE), row(3 * LANES)]
        + _weight_specs(),
        out_specs=[row(D_MODEL), _const_spec((WINDOW, A_KV_WIDTH)), _const_spec((WINDOW, A_KV_WIDTH)),
                   _const_spec((B_QK_WIDTH, B_VAL_DIM))],
        out_shape=[jax.ShapeDtypeStruct((seq, D_MODEL), F32),
                   jax.ShapeDtypeStruct((WINDOW, A_KV_WIDTH), F32),
                   jax.ShapeDtypeStruct((WINDOW, A_KV_WIDTH), F32),
                   jax.ShapeDtypeStruct((B_QK_WIDTH, B_VAL_DIM), F32)],
        scratch_shapes=[pltpu.VMEM((A_KV_HEADS, WINDOW + tile, LANES), BF16),
                        pltpu.VMEM((A_KV_HEADS, WINDOW + tile, 2 * LANES), BF16),
                        pltpu.VMEM((B_VAL_DIM, B_QK_WIDTH), F32)],
        compiler_params=pltpu.CompilerParams(dimension_semantics=("arbitrary",),
                                             vmem_limit_bytes=VMEM_LIMIT_BYTES),
        name="prompt_layer",
    )(sinks, h, p, rope, *weights)


def _sample_layer(h, p, rope, ck, cv, s0, sinks, weights, dec_seq):
    n_seq = ck.shape[0]
    nb = SAMPLE_BATCH_TILE
    assert n_seq % nb == 0
    rows = nb * dec_seq
    row = lambda w: pl.BlockSpec((rows, w), lambda i: (i, 0))
    per_seq = lambda a, b: pl.BlockSpec((nb, a, b), lambda i: (i, 0, 0))
    return pl.pallas_call(
        functools.partial(_sample_kernel, dec_seq=dec_seq),
        grid=(n_seq // nb,),
        in_specs=[pl.BlockSpec(memory_space=pltpu.SMEM), row(D_MODEL), row(D_PLE), _const_spec((rows, 3 * LANES)),
                  per_seq(WINDOW, A_KV_WIDTH), per_seq(WINDOW, A_KV_WIDTH), per_seq(B_QK_WIDTH, B_VAL_DIM)]
        + _weight_specs(),
        out_specs=[row(D_MODEL), row(A_KV_WIDTH), row(A_KV_WIDTH), per_seq(B_QK_WIDTH, B_VAL_DIM)],
        out_shape=[jax.ShapeDtypeStruct((n_seq * dec_seq, D_MODEL), F32),
                   jax.ShapeDtypeStruct((n_seq * dec_seq, A_KV_WIDTH), F32),
                   jax.ShapeDtypeStruct((n_seq * dec_seq, A_KV_WIDTH), F32),
                   jax.ShapeDtypeStruct((n_seq, B_QK_WIDTH, B_VAL_DIM), F32)],
        compiler_params=pltpu.CompilerParams(dimension_semantics=("arbitrary",),
                                             vmem_limit_bytes=VMEM_LIMIT_BYTES),
        name="sample_layer",
    )(sinks, h, p, rope, ck, cv, s0, *weights)


def kernel(x_prompt, x_sample, cache_k, cache_v, state_gla, p_prompt, p_sample, norm_g, w_in, q_norm_g, k_norm_g,
           sinks, w_gate_up, b_gate, gla_norm_g, w_out, pe_norm_g, w_pe, w_pg):
    batch, seq, _ = x_prompt.shape
    n_seq, dec_seq, _ = x_sample.shape
    assert batch == 1 and cache_k.shape[2] == WINDOW

    rope_p = _rope_table(jnp.arange(seq, dtype=jnp.int32))
    rope_s = jnp.tile(_rope_table(PAST_LEN + jnp.arange(dec_seq, dtype=jnp.int32)), (SAMPLE_BATCH_TILE, 1))

    w_in_b = jnp.pad(w_in, ((0, 0), (0, 0), (0, IN_COLS_PAD - w_in.shape[2]))).astype(BF16)
    wgu_b = jnp.pad(w_gate_up, ((0, 0), (0, LANES - GATE_RANK), (0, 0))).astype(BF16)
    w_out_b, w_pe_b, w_pg_b = w_out.astype(BF16), w_pe.astype(BF16), w_pg.astype(BF16)
    gqk = jnp.concatenate([jnp.tile(q_norm_g, (1, A_HEADS)) * (A_HEAD_DIM ** -0.5),
                           jnp.tile(k_norm_g, (1, A_KV_HEADS))], axis=1)

    hp = x_prompt.reshape(seq, D_MODEL)
    hs = x_sample.reshape(n_seq * dec_seq, D_MODEL)
    ck = cache_k.reshape(DEPTH, n_seq, WINDOW, A_KV_WIDTH)
    cv = cache_v.reshape(DEPTH, n_seq, WINDOW, A_KV_WIDTH)
    s0 = state_gla.reshape(DEPTH, n_seq, B_QK_WIDTH, B_VAL_DIM)
    pp = p_prompt.reshape(DEPTH, seq, D_PLE)
    ps = p_sample.reshape(DEPTH, n_seq * dec_seq, D_PLE)

    kp_l, vp_l, sp_l, ks_l, vs_l, ss_l = [], [], [], [], [], []
    for l in range(DEPTH):
        weights = (norm_g[l][None], w_in_b[l], gqk[l][None], wgu_b[l], b_gate[l][None], gla_norm_g[l][None],
                   w_out_b[l], pe_norm_g[l][None], w_pe_b[l], w_pg_b[l])
        hp, kn, vn, sn = _prompt_layer(hp, pp[l], rope_p, sinks[l], weights)
        kp_l.append(kn); vp_l.append(vn); sp_l.append(sn)
        hs, kn, vn, sn = _sample_layer(hs, ps[l], rope_s, ck[l], cv[l], s0[l], sinks[l], weights, dec_seq)
        ks_l.append(kn); vs_l.append(vn); ss_l.append(sn)

    return (hp.reshape(batch, seq, D_MODEL),
            hs.reshape(n_seq, dec_seq, D_MODEL),
            jnp.stack(kp_l).reshape(DEPTH, batch, WINDOW, A_KV_HEADS, A_HEAD_DIM),
            jnp.stack(vp_l).reshape(DEPTH, batch, WINDOW, A_KV_HEADS, A_HEAD_DIM),
            jnp.stack(sp_l).reshape(DEPTH, batch, B_HEADS, B_KEY_DIM, B_VAL_DIM),
            jnp.stack(ks_l).reshape(DEPTH, n_seq, dec_seq, A_KV_HEADS, A_HEAD_DIM),
            jnp.stack(vs_l).reshape(DEPTH, n_seq, dec_seq, A_KV_HEADS, A_HEAD_DIM),
            jnp.stack(ss_l).reshape(DEPTH, n_seq, B_HEADS, B_KEY_DIM, B_VAL_DIM))
```

```python
import functools

import jax
import jax.numpy as jnp
from jax import lax
from jax.experimental import pallas as pl
from jax.experimental.pallas import tpu as pltpu

F32 = jnp.float32
BF16 = jnp.bfloat16

D_MODEL = 1024
DEPTH = 4
PAST_LEN = 1024
CHUNK = 64
D_PLE = 256
A_HEADS = 8
A_KV_HEADS = 2
A_HEAD_DIM = 64
A_GROUP = A_HEADS // A_KV_HEADS
A_WIDTH = A_HEADS * A_HEAD_DIM
A_KV_WIDTH = A_KV_HEADS * A_HEAD_DIM
WINDOW = 128
ROT_DIM = A_HEAD_DIM // 4
ROPE_THETA = 500000.0
B_HEADS = 4
B_KEY_DIM = 64
B_VAL_DIM = 128
B_QK_WIDTH = B_HEADS * B_KEY_DIM
B_WIDTH = B_HEADS * B_VAL_DIM
GATE_RANK = 16
GATE_TAU = 16.0
NEG_INF = -1e30
EPS = 1e-6

LANES = 128

C_QA = 0
C_KA = C_QA + A_WIDTH
C_VA = C_KA + A_KV_WIDTH
C_GA = C_VA + A_KV_WIDTH
C_QB = C_GA + A_WIDTH
C_KB = C_QB + B_QK_WIDTH
C_VB = C_KB + B_QK_WIDTH
C_GB = C_VB + B_WIDTH
C_AB = C_GB + B_WIDTH
IN_COLS_PAD = C_AB + LANES

PROMPT_TILE = 512
GLA_SPAN = 256
SAMPLE_BATCH_TILE = 8
VMEM_LIMIT_BYTES = 56 * 1024 * 1024


def _dot(a, b):
    return jnp.dot(a, b, preferred_element_type=F32)


def _dot_nt(a, b):
    return lax.dot_general(a, b, (((1,), (1,)), ((), ())), preferred_element_type=F32)


def _dot_tn(a, b):
    return lax.dot_general(a, b, (((0,), (0,)), ((), ())), preferred_element_type=F32)


def _split_bf16(x):
    hi = x.astype(BF16)
    lo = (x - hi.astype(F32)).astype(BF16)
    return hi, lo


def _rms(x, g):
    ms = jnp.mean(x * x, axis=-1, keepdims=True)
    return x * lax.rsqrt(ms + EPS) * g


def _sigmoid(x):
    return 1.0 / (1.0 + jnp.exp(-x))


def _silu(x):
    return x * _sigmoid(x)


def _log_sigmoid(x):
    return jnp.minimum(x, 0.0) - jnp.log(1.0 + jnp.exp(-jnp.abs(x)))


def _block_mask(n, block, lower):
    r = lax.broadcasted_iota(jnp.int32, (n, n), 0)
    c = lax.broadcasted_iota(jnp.int32, (n, n), 1)
    m = (r // block) == (c // block)
    if lower:
        m = m & (r >= c)
    return jnp.where(m, 1.0, 0.0).astype(BF16)


def _project_a(xn, rope_ref, win_ref, gqk_ref):
    qk = _dot(xn, win_ref[:, C_QA:C_VA])
    seg = _block_mask(256, A_HEAD_DIM, lower=False)
    hi, lo = _split_bf16(qk * qk)
    parts = []
    for c0 in range(0, C_VA, 256):
        c1 = min(c0 + 256, C_VA)
        sg = seg[: c1 - c0, : c1 - c0]
        parts.append(_dot(hi[:, c0:c1], sg) + _dot(lo[:, c0:c1], sg))
    ss = jnp.concatenate(parts, axis=1)
    qkn = qk * lax.rsqrt(ss * (1.0 / A_HEAD_DIM) + EPS) * gqk_ref[...]
    tab = rope_ref[...]
    head_lane = lax.broadcasted_iota(jnp.int32, (1, LANES), 1) % A_HEAD_DIM
    cos = jnp.where(head_lane < ROT_DIM, tab, 1.0)
    sin = jnp.where(head_lane < ROT_DIM, pltpu.roll(tab, LANES - ROT_DIM, 1), 0.0)
    first_half = head_lane < ROT_DIM // 2
    slabs = []
    for s in range(C_VA // LANES):
        x = qkn[:, s * LANES:(s + 1) * LANES]
        up = pltpu.roll(x, LANES - ROT_DIM // 2, 1)
        dn = pltpu.roll(x, ROT_DIM // 2, 1)
        slabs.append(x * cos + jnp.where(first_half, up, dn) * sin)
    q = jnp.concatenate(slabs[:A_WIDTH // LANES], axis=1)
    k = slabs[A_WIDTH // LANES]
    v = _dot(xn, win_ref[:, C_VA:C_GA])
    return q, k, v


def _kv_operands(k, v, lo_half):
    ksw = pltpu.roll(k, A_HEAD_DIM, 1)
    vsw = pltpu.roll(v, A_HEAD_DIM, 1)
    one = jnp.ones_like(v)
    kd = (jnp.where(lo_half, k, ksw).astype(BF16), jnp.where(lo_half, ksw, k).astype(BF16))
    vx = (jnp.concatenate([jnp.where(lo_half, v, one), jnp.where(lo_half, one, vsw)], axis=1).astype(BF16),
          jnp.concatenate([jnp.where(lo_half, vsw, one), jnp.where(lo_half, one, v)], axis=1).astype(BF16))
    return kd, vx


def _attn_scores(q_blk, j, kd, lo_half):
    a = q_blk[:, (2 * j) * LANES:(2 * j + 1) * LANES]
    b = q_blk[:, (2 * j + 1) * LANES:(2 * j + 2) * LANES]
    z = jnp.zeros_like(a)
    stack = jnp.concatenate([jnp.where(lo_half, a, z), jnp.where(lo_half, z, a),
                             jnp.where(lo_half, b, z), jnp.where(lo_half, z, b)], axis=0).astype(BF16)
    return _dot_nt(stack, kd)


def _attn_probs(s, j, bias, sink_ref, layer):
    rows = s.shape[0] // A_GROUP
    if bias is not None:
        s = s + bias
    sk = jnp.concatenate([jnp.full((rows, 1), sink_ref[layer, A_GROUP * j + g], F32) for g in range(A_GROUP)],
                         axis=0)
    m = jnp.maximum(jnp.max(s, axis=-1, keepdims=True), sk)
    return jnp.exp(s - m).astype(BF16), jnp.exp(sk - m)


def _attn_values(p, de, vx, lo_half):
    rows = p.shape[0] // A_GROUP
    r = _dot(p, vx)
    r_lo, r_hi = r[:, :LANES], r[:, LANES:]

    def slab(g0):
        e = slice(g0 * rows, (g0 + 1) * rows)
        o = slice((g0 + 1) * rows, (g0 + 2) * rows)
        return jnp.where(lo_half, r_lo[e] / (r_hi[e] + de[e]), r_hi[o] / (r_lo[o] + de[o]))

    return slab(0), slab(2)


def _attn_block(q_blk, j, kd, vx, bias, sink_ref, layer, lo_half):
    p, de = _attn_probs(_attn_scores(q_blk, j, kd, lo_half), j, bias, sink_ref, layer)
    return _attn_values(p, de, vx, lo_half)


def _gla_inputs(xn, win_ref, wgu_ref, bg_ref, block):
    rows = xn.shape[0]
    qb = _dot(xn, win_ref[:, C_QB:C_KB])
    kb = _dot(xn, win_ref[:, C_KB:C_VB])
    vb = _dot(xn, win_ref[:, C_VB:C_GB])
    ab = _dot(xn, win_ref[:, C_AB:IN_COLS_PAD])
    logg = _log_sigmoid(_dot(ab.astype(BF16), wgu_ref[...]) + bg_ref[...]) * (1.0 / GATE_TAU)
    g_hi, g_lo = _split_bf16(logg)
    tri = _block_mask(rows, block, lower=True)
    ones = _block_mask(rows, block, lower=False)
    b = _dot(tri, g_hi) + _dot(tri, g_lo)
    b_last = _dot(ones, g_hi) + _dot(ones, g_lo)
    qt = qb * (jnp.exp(b) * (B_KEY_DIM ** -0.5))
    kt = kb * jnp.exp(-b)
    ke = kb * jnp.exp(b_last - b)
    return qt, kt, ke, vb, logg


def _head_stack(x, head_mask):
    z = jnp.zeros_like(x)
    return jnp.concatenate([jnp.where(head_mask[h], x, z) for h in range(B_HEADS)], axis=0).astype(BF16)


def _finish_layer(h, oa, ob, xn, p, win_ref, glag_ref, wout_ref, peg_ref, wpe_ref, wpg_ref):
    ga = _dot(xn, win_ref[:, C_GA:C_QB])
    gb = _dot(xn, win_ref[:, C_GB:C_AB])
    obn = jnp.concatenate(
        [_rms(ob[:, hh * B_VAL_DIM:(hh + 1) * B_VAL_DIM], glag_ref[...]) for hh in range(B_HEADS)], axis=1)
    mix = jnp.concatenate([oa * _silu(ga), obn * _silu(gb)], axis=1).astype(BF16)
    h1 = h + _dot(mix, wout_ref[...])
    gate = _sigmoid(_dot(_rms(h1, peg_ref[...]).astype(BF16), wpg_ref[...]))
    return h1 + gate * _dot(p.astype(BF16), wpe_ref[...])


def _gla_decays(kbt, abt, wgut_ref, bgc_ref):
    tile = kbt.shape[1]
    loggt = _log_sigmoid(_dot(wgut_ref[...], abt) + bgc_ref[...]) * (1.0 / GATE_TAU)
    g_hi, g_lo = _split_bf16(loggt)
    r = lax.broadcasted_iota(jnp.int32, (GLA_SPAN, GLA_SPAN), 0)
    c = lax.broadcasted_iota(jnp.int32, (GLA_SPAN, GLA_SPAN), 1)
    upper = jnp.where(((r // CHUNK) == (c // CHUNK)) & (r <= c), 1.0, 0.0).astype(BF16)
    bt = jnp.concatenate(
        [_dot(g_hi[:, s0:s0 + GLA_SPAN], upper) + _dot(g_lo[:, s0:s0 + GLA_SPAN], upper)
         for s0 in range(0, tile, GLA_SPAN)], axis=1)
    ktt = (kbt * jnp.exp(-bt)).astype(BF16)
    pair = 2 * CHUNK
    cols = [bt[:, (c + 1) * CHUNK - 1:(c + 1) * CHUNK] for c in range(tile // CHUNK)]
    lane = lax.broadcasted_iota(jnp.int32, (1, pair), 1)
    ket = jnp.concatenate(
        [kbt[:, pc * pair:(pc + 1) * pair]
         * jnp.exp(jnp.where(lane < CHUNK, cols[2 * pc], cols[2 * pc + 1]) - bt[:, pc * pair:(pc + 1) * pair])
         for pc in range(tile // pair)], axis=1)
    return bt, ktt, ket, cols


def _gla_chunks(qt, vb, ktt, ket, cols, stk):
    tile = qt.shape[0]
    n_chunks = tile // CHUNK
    pair = 2 * CHUNK
    head_of_lane = lax.broadcasted_iota(jnp.int32, (1, B_QK_WIDTH), 1) // B_KEY_DIM
    head_mask = [head_of_lane == hh for hh in range(B_HEADS)]
    lane = lax.broadcasted_iota(jnp.int32, (1, pair), 1)
    t_row = lax.broadcasted_iota(jnp.int32, (CHUNK, pair), 0)
    s_col = lax.broadcasted_iota(jnp.int32, (CHUNK, pair), 1)
    own = [lane < CHUNK, lane >= CHUNK]
    causal = [s_col <= t_row, (s_col >= CHUNK) & (s_col - CHUNK <= t_row)]
    v_heads = [[vb[(c // 2) * pair:(c // 2 + 1) * pair, hh * B_VAL_DIM:(hh + 1) * B_VAL_DIM]
                for hh in range(B_HEADS)] for c in range(n_chunks)]

    updates = []
    for c in range(n_chunks):
        ps = slice((c // 2) * pair, (c // 2 + 1) * pair)
        ket_c = jnp.where(own[c % 2], ket[:, ps], 0.0).astype(BF16)
        updates.append(jnp.concatenate(
            [_dot(ket_c[hh * B_KEY_DIM:(hh + 1) * B_KEY_DIM], v_heads[c][hh]) for hh in range(B_HEADS)], axis=0))
    states = []
    for c in range(n_chunks):
        states.append(stk.astype(BF16))
        stk = stk * jnp.exp(cols[c]) + updates[c]
    results = []
    for c in range(n_chunks):
        ps = slice((c // 2) * pair, (c // 2 + 1) * pair)
        q_stack = _head_stack(qt[c * CHUNK:(c + 1) * CHUNK], head_mask)
        results.append(_dot(q_stack, jnp.concatenate([states[c], ktt[:, ps]], axis=1)))
    ob_rows = []
    for c in range(n_chunks):
        outs = []
        for hh in range(B_HEADS):
            hr = slice(hh * CHUNK, (hh + 1) * CHUNK)
            a_h = jnp.where(causal[c % 2], results[c][hr, B_VAL_DIM:], 0.0).astype(BF16)
            outs.append(_dot(a_h, v_heads[c][hh]) + results[c][hr, :B_VAL_DIM])
        ob_rows.append(jnp.concatenate(outs, axis=1))
    return jnp.concatenate(ob_rows, axis=0), stk


def _prompt_kernel(sink_ref, h_ref, p_ref, rope_ref, ng_ref, win_ref, gqk_ref, wgu_ref, bg_ref, glag_ref,
                   wout_ref, peg_ref, wpe_ref, wpg_ref, wkbt_ref, wabt_ref, wgut_ref, bgc_ref,
                   ho_ref, nk_ref, nv_ref, st_ref,
                   kd_scr, vx_scr, st_scr, *, layer):
    tile = PROMPT_TILE
    n_chunks = tile // CHUNK
    band = WINDOW + CHUNK
    i = pl.program_id(0)
    rd = (i + 1) % 2
    wr = i % 2

    @pl.when(i == 0)
    def _():
        kd_scr[rd] = jnp.zeros(kd_scr.shape[1:], BF16)
        vx_scr[rd] = jnp.zeros(vx_scr.shape[1:], BF16)
        st_scr[rd] = jnp.zeros(st_scr.shape[1:], F32)

    h = h_ref[...]
    xn = _rms(h, ng_ref[...]).astype(BF16)
    lo_half = lax.broadcasted_iota(jnp.int32, (1, LANES), 1) < A_HEAD_DIM

    q, k, v = _project_a(xn, rope_ref, win_ref, gqk_ref)
    nk_ref[...] = k[tile - WINDOW:, :]
    nv_ref[...] = v[tile - WINDOW:, :]
    kd, vx = _kv_operands(k, v, lo_half)
    kd_ext = [jnp.concatenate([kd_scr[rd, j], kd[j]], axis=0) for j in range(A_KV_HEADS)]
    vx_ext = [jnp.concatenate([vx_scr[rd, j], vx[j]], axis=0) for j in range(A_KV_HEADS)]
    key_blk = lax.broadcasted_iota(jnp.int32, (1, band), 1) // CHUNK
    blocks = [(c, j) for c in range(n_chunks) for j in range(A_KV_HEADS)]
    scores = [_attn_scores(q[c * CHUNK:(c + 1) * CHUNK, :], j, kd_ext[j][c * CHUNK:c * CHUNK + band, :], lo_half)
              for c, j in blocks]
    proj_cols = [(C_GA, C_GA + 256), (C_GA + 256, C_QB), (C_GB, C_GB + 256), (C_GB + 256, C_AB),
                 (C_QB, C_KB), (C_VB, C_VB + 256), (C_VB + 256, C_GB)]
    proj = []
    probs = []
    for n, ((c, j), s) in enumerate(zip(blocks, scores)):
        first_key_chunk = i * n_chunks + c - WINDOW // CHUNK
        bias = jnp.where(key_blk + first_key_chunk >= 0, 0.0, NEG_INF)
        probs.append(_attn_probs(s, j, bias, sink_ref, layer))
        if n % 2 == 1 and len(proj) < len(proj_cols):
            c0, c1 = proj_cols[len(proj)]
            proj.append(_dot(xn, win_ref[:, c0:c1]))
    while len(proj) < len(proj_cols):
        c0, c1 = proj_cols[len(proj)]
        proj.append(_dot(xn, win_ref[:, c0:c1]))
    ga = jnp.concatenate(proj[0:2], axis=1)
    gb = jnp.concatenate(proj[2:4], axis=1)
    qb = proj[4]
    vb = jnp.concatenate(proj[5:7], axis=1).astype(BF16)
    kbt = _dot_nt(wkbt_ref[...], xn)
    abt = _dot_nt(wabt_ref[...], xn).astype(BF16)
    slabs = [_attn_values(p, de, vx_ext[j][c * CHUNK:c * CHUNK + band, :], lo_half)
             for (c, j), (p, de) in zip(blocks, probs)]
    oa = jnp.concatenate(
        [jnp.concatenate([sl for j in range(A_KV_HEADS) for sl in slabs[c * A_KV_HEADS + j]], axis=1)
         for c in range(n_chunks)], axis=0)
    for j in range(A_KV_HEADS):
        kd_scr[wr, j] = kd[j][tile - WINDOW:, :]
        vx_scr[wr, j] = vx[j][tile - WINDOW:, :]

    bt, ktt, ket, cols = _gla_decays(kbt, abt, wgut_ref, bgc_ref)
    mix_a = (oa * _silu(ga)).astype(BF16)
    h1 = h + _dot(mix_a, wout_ref[0:A_WIDTH, :])
    qt = qb * (jnp.exp(bt.T) * (B_KEY_DIM ** -0.5))
    pe = _dot(p_ref[...].astype(BF16), wpe_ref[...])
    ob, stk = _gla_chunks(qt, vb, ktt, ket, cols, st_scr[rd])
    st_scr[wr] = stk
    st_ref[...] = stk

    obn = jnp.concatenate(
        [_rms(ob[:, hh * B_VAL_DIM:(hh + 1) * B_VAL_DIM], glag_ref[...]) for hh in range(B_HEADS)], axis=1)
    h1 = h1 + _dot((obn * _silu(gb)).astype(BF16), wout_ref[A_WIDTH:, :])
    gate = _sigmoid(_dot(_rms(h1, peg_ref[...]).astype(BF16), wpg_ref[...]))
    ho_ref[...] = h1 + gate * pe


def _sample_kernel(sink_ref, h_ref, p_ref, rope_ref, ck_ref, cv_ref, s0_ref, ng_ref, win_ref, gqk_ref, wgu_ref,
                   bg_ref, glag_ref, wout_ref, peg_ref, wpe_ref, wpg_ref,
                   ho_ref, nk_ref, nv_ref, st_ref, *, layer, dec_seq):
    nb = SAMPLE_BATCH_TILE
    h = h_ref[...]
    xn = _rms(h, ng_ref[...]).astype(BF16)
    lo_half = lax.broadcasted_iota(jnp.int32, (1, LANES), 1) < A_HEAD_DIM

    q, k, v = _project_a(xn, rope_ref, win_ref, gqk_ref)
    nk_ref[...] = k
    nv_ref[...] = v
    kd_new, vx_new = _kv_operands(k, v, lo_half)
    oa_rows = []
    for bi in range(nb):
        rs = slice(bi * dec_seq, (bi + 1) * dec_seq)
        kd_old, vx_old = _kv_operands(ck_ref[bi], cv_ref[bi], lo_half)
        slabs = []
        for j in range(A_KV_HEADS):
            kd = jnp.concatenate([kd_old[j], kd_new[j][rs]], axis=0)
            vx = jnp.concatenate([vx_old[j], vx_new[j][rs]], axis=0)
            slabs.extend(_attn_block(q[rs], j, kd, vx, None, sink_ref, layer, lo_half))
        oa_rows.append(jnp.concatenate(slabs, axis=1))
    oa = jnp.concatenate(oa_rows, axis=0)

    qt, kt, ke, vb, g_all = _gla_inputs(xn, win_ref, wgu_ref, bg_ref, dec_seq)
    head_of_lane = lax.broadcasted_iota(jnp.int32, (1, B_QK_WIDTH), 1) // B_KEY_DIM
    head_mask = [head_of_lane == hh for hh in range(B_HEADS)]
    causal = (lax.broadcasted_iota(jnp.int32, (dec_seq, dec_seq), 0)
              >= lax.broadcasted_iota(jnp.int32, (dec_seq, dec_seq), 1))
    ones_rhs = jnp.ones((dec_seq, LANES), BF16)
    ob_rows = []
    for bi in range(nb):
        rs = slice(bi * dec_seq, (bi + 1) * dec_seq)
        s_stack = s0_ref[bi]
        q_stack = _head_stack(qt[rs], head_mask)
        att = _dot_nt(q_stack, kt[rs].astype(BF16))
        inter = _dot(q_stack, s_stack.astype(BF16))
        att = jnp.concatenate([jnp.where(causal, att[hh * dec_seq:(hh + 1) * dec_seq], 0.0)
                               for hh in range(B_HEADS)], axis=0).astype(BF16)
        vb_b = vb[rs].astype(BF16)
        intra = _dot(att, vb_b)
        ob_rows.append(jnp.concatenate(
            [intra[hh * dec_seq:(hh + 1) * dec_seq, hh * B_VAL_DIM:(hh + 1) * B_VAL_DIM]
             + inter[hh * dec_seq:(hh + 1) * dec_seq] for hh in range(B_HEADS)], axis=1))
        upd_full = _dot_tn(ke[rs].astype(BF16), vb_b)
        upd = jnp.concatenate(
            [upd_full[hh * B_KEY_DIM:(hh + 1) * B_KEY_DIM, hh * B_VAL_DIM:(hh + 1) * B_VAL_DIM]
             for hh in range(B_HEADS)], axis=0)
        g_hi, g_lo = _split_bf16(g_all[rs])
        total = _dot_tn(g_hi, ones_rhs) + _dot_tn(g_lo, ones_rhs)
        st_ref[bi] = s_stack * jnp.exp(total) + upd
    ob = jnp.concatenate(ob_rows, axis=0)

    ho_ref[...] = _finish_layer(h, oa, ob, xn, p_ref[...], win_ref, glag_ref, wout_ref, peg_ref,
                                wpe_ref, wpg_ref)


def _rope_table(pos):
    half = ROT_DIM // 2
    inv = jnp.power(jnp.float32(ROPE_THETA), -jnp.arange(half, dtype=F32) * (2.0 / ROT_DIM))
    ang = pos.astype(F32)[:, None] * inv[None, :]
    cos, sin = jnp.cos(ang), jnp.sin(ang)
    n = pos.shape[0]
    head = jnp.concatenate([cos, cos, -sin, sin, jnp.zeros((n, A_HEAD_DIM - 2 * ROT_DIM), F32)], axis=1)
    return jnp.concatenate([head, head], axis=1)


def _layer_spec(layer, shape):
    return pl.BlockSpec((None,) + shape, lambda i: (layer,) + (0,) * len(shape))


def _weight_specs(layer):
    return [
        _layer_spec(layer, (1, D_MODEL)),
        _layer_spec(layer, (D_MODEL, IN_COLS_PAD)),
        _layer_spec(layer, (1, C_VA)),
        _layer_spec(layer, (LANES, B_QK_WIDTH)),
        _layer_spec(layer, (1, B_QK_WIDTH)),
        _layer_spec(layer, (1, B_VAL_DIM)),
        _layer_spec(layer, (A_WIDTH + B_WIDTH, D_MODEL)),
        _layer_spec(layer, (1, D_MODEL)),
        _layer_spec(layer, (D_PLE, D_MODEL)),
        _layer_spec(layer, (D_MODEL, D_MODEL)),
    ]


def _prompt_layer(layer, h, p, rope, sinks, weights, weights_t):
    seq = h.shape[0]
    tile = PROMPT_TILE
    assert seq % tile == 0 and tile % GLA_SPAN == 0 and GLA_SPAN % (2 * CHUNK) == 0 and tile >= WINDOW
    row = lambda w: pl.BlockSpec((tile, w), lambda i: (i, 0))
    const = lambda shape: pl.BlockSpec(shape, lambda i: (0,) * len(shape))
    return pl.pallas_call(
        functools.partial(_prompt_kernel, layer=layer),
        grid=(seq // tile,),
        in_specs=[pl.BlockSpec(memory_space=pltpu.SMEM), row(D_MODEL),
                  pl.BlockSpec((None, tile, D_PLE), lambda i: (layer, i, 0)), row(LANES)]
        + _weight_specs(layer)
        + [_layer_spec(layer, (B_QK_WIDTH, D_MODEL)),
           _layer_spec(layer, (GATE_RANK, D_MODEL)),
           _layer_spec(layer, (B_QK_WIDTH, GATE_RANK)),
           _layer_spec(layer, (B_QK_WIDTH, 1))],
        out_specs=[row(D_MODEL), const((WINDOW, A_KV_WIDTH)), const((WINDOW, A_KV_WIDTH)),
                   const((B_QK_WIDTH, B_VAL_DIM))],
        out_shape=[jax.ShapeDtypeStruct((seq, D_MODEL), F32),
                   jax.ShapeDtypeStruct((WINDOW, A_KV_WIDTH), F32),
                   jax.ShapeDtypeStruct((WINDOW, A_KV_WIDTH), F32),
                   jax.ShapeDtypeStruct((B_QK_WIDTH, B_VAL_DIM), F32)],
        scratch_shapes=[pltpu.VMEM((2, A_KV_HEADS, WINDOW, LANES), BF16),
                        pltpu.VMEM((2, A_KV_HEADS, WINDOW, 2 * LANES), BF16),
                        pltpu.VMEM((2, B_QK_WIDTH, B_VAL_DIM), F32)],
        compiler_params=pltpu.CompilerParams(dimension_semantics=("arbitrary",),
                                             vmem_limit_bytes=VMEM_LIMIT_BYTES),
        name="prompt_layer",
    )(sinks, h, p, rope, *weights, *weights_t)


def _sample_layer(layer, h, p, rope, ck, cv, s0, sinks, weights, dec_seq):
    n_seq = ck.shape[1]
    nb = SAMPLE_BATCH_TILE
    assert n_seq % nb == 0
    rows = nb * dec_seq
    row = lambda w: pl.BlockSpec((rows, w), lambda i: (i, 0))
    per_seq_in = lambda a, b: pl.BlockSpec((None, nb, a, b), lambda i: (layer, i, 0, 0))
    return pl.pallas_call(
        functools.partial(_sample_kernel, layer=layer, dec_seq=dec_seq),
        grid=(n_seq // nb,),
        in_specs=[pl.BlockSpec(memory_space=pltpu.SMEM), row(D_MODEL),
                  pl.BlockSpec((None, rows, D_PLE), lambda i: (layer, i, 0)),
                  pl.BlockSpec((rows, LANES), lambda i: (0, 0)),
                  per_seq_in(WINDOW, A_KV_WIDTH), per_seq_in(WINDOW, A_KV_WIDTH),
                  per_seq_in(B_QK_WIDTH, B_VAL_DIM)]
        + _weight_specs(layer),
        out_specs=[row(D_MODEL), row(A_KV_WIDTH), row(A_KV_WIDTH),
                   pl.BlockSpec((nb, B_QK_WIDTH, B_VAL_DIM), lambda i: (i, 0, 0))],
        out_shape=[jax.ShapeDtypeStruct((n_seq * dec_seq, D_MODEL), F32),
                   jax.ShapeDtypeStruct((n_seq * dec_seq, A_KV_WIDTH), F32),
                   jax.ShapeDtypeStruct((n_seq * dec_seq, A_KV_WIDTH), F32),
                   jax.ShapeDtypeStruct((n_seq, B_QK_WIDTH, B_VAL_DIM), F32)],
        compiler_params=pltpu.CompilerParams(dimension_semantics=("arbitrary",),
                                             vmem_limit_bytes=VMEM_LIMIT_BYTES),
        name="sample_layer",
    )(sinks, h, p, rope, ck, cv, s0, *weights)


def kernel(x_prompt, x_sample, cache_k, cache_v, state_gla, p_prompt, p_sample, norm_g, w_in, q_norm_g, k_norm_g,
           sinks, w_gate_up, b_gate, gla_norm_g, w_out, pe_norm_g, w_pe, w_pg):
    batch, seq, _ = x_prompt.shape
    n_seq, dec_seq, _ = x_sample.shape
    assert batch == 1 and cache_k.shape[2] == WINDOW

    rope_p = _rope_table(jnp.arange(seq, dtype=jnp.int32))
    rope_s = jnp.tile(_rope_table(PAST_LEN + jnp.arange(dec_seq, dtype=jnp.int32)), (SAMPLE_BATCH_TILE, 1))

    w_in_b = jnp.pad(w_in, ((0, 0), (0, 0), (0, IN_COLS_PAD - w_in.shape[2]))).astype(BF16)
    wgu_b = jnp.pad(w_gate_up, ((0, 0), (0, LANES - GATE_RANK), (0, 0))).astype(BF16)
    gqk = jnp.concatenate([jnp.tile(q_norm_g, (1, A_HEADS)) * (A_HEAD_DIM ** -0.5),
                           jnp.tile(k_norm_g, (1, A_KV_HEADS))], axis=1)
    vec = lambda a: a[:, None, :]
    weights = (vec(norm_g), w_in_b, vec(gqk), wgu_b, vec(b_gate), vec(gla_norm_g), w_out.astype(BF16),
               vec(pe_norm_g), w_pe.astype(BF16), w_pg.astype(BF16))
    weights_t = (jnp.swapaxes(w_in[:, :, C_KB:C_VB], 1, 2).astype(BF16),
                 jnp.swapaxes(w_in[:, :, C_AB:C_AB + GATE_RANK], 1, 2).astype(BF16),
                 jnp.swapaxes(w_gate_up, 1, 2).astype(BF16),
                 b_gate[:, :, None])

    hp = x_prompt.reshape(seq, D_MODEL)
    hs = x_sample.reshape(n_seq * dec_seq, D_MODEL)
    ck = cache_k.reshape(DEPTH, n_seq, WINDOW, A_KV_WIDTH)
    cv = cache_v.reshape(DEPTH, n_seq, WINDOW, A_KV_WIDTH)
    s0 = state_gla.reshape(DEPTH, n_seq, B_QK_WIDTH, B_VAL_DIM)
    pp = p_prompt.reshape(DEPTH, seq, D_PLE)
    ps = p_sample.reshape(DEPTH, n_seq * dec_seq, D_PLE)

    kp_l, vp_l, sp_l, ks_l, vs_l, ss_l = [], [], [], [], [], []
    for layer in range(DEPTH):
        hp, kn, vn, sn = _prompt_layer(layer, hp, pp, rope_p, sinks, weights, weights_t)
        kp_l.append(kn); vp_l.append(vn); sp_l.append(sn)
        hs, kn, vn, sn = _sample_layer(layer, hs, ps, rope_s, ck, cv, s0, sinks, weights, dec_seq)
        ks_l.append(kn); vs_l.append(vn); ss_l.append(sn)

    return (hp.reshape(batch, seq, D_MODEL),
            hs.reshape(n_seq, dec_seq, D_MODEL),
            jnp.stack(kp_l).reshape(DEPTH, batch, WINDOW, A_KV_HEADS, A_HEAD_DIM),
            jnp.stack(vp_l).reshape(DEPTH, batch, WINDOW, A_KV_HEADS, A_HEAD_DIM),
            jnp.stack(sp_l).reshape(DEPTH, batch, B_HEADS, B_KEY_DIM, B_VAL_DIM),
            jnp.stack(ks_l).reshape(DEPTH, n_seq, dec_seq, A_KV_HEADS, A_HEAD_DIM),
            jnp.stack(vs_l).reshape(DEPTH, n_seq, dec_seq, A_KV_HEADS, A_HEAD_DIM),
            jnp.stack(ss_l).reshape(DEPTH, n_seq, B_HEADS, B_KEY_DIM, B_VAL_DIM))
```

```python
import functools

import jax
import jax.numpy as jnp
from jax import lax
from jax.experimental import pallas as pl
from jax.experimental.pallas import tpu as pltpu

F32 = jnp.float32
BF16 = jnp.bfloat16

D_MODEL = 1024
DEPTH = 4
PAST_LEN = 1024
CHUNK = 64
D_PLE = 256
A_HEADS = 8
A_KV_HEADS = 2
A_HEAD_DIM = 64
A_GROUP = A_HEADS // A_KV_HEADS
A_WIDTH = A_HEADS * A_HEAD_DIM
A_KV_WIDTH = A_KV_HEADS * A_HEAD_DIM
WINDOW = 128
ROT_DIM = A_HEAD_DIM // 4
ROPE_THETA = 500000.0
B_HEADS = 4
B_KEY_DIM = 64
B_VAL_DIM = 128
B_QK_WIDTH = B_HEADS * B_KEY_DIM
B_WIDTH = B_HEADS * B_VAL_DIM
GATE_RANK = 16
GATE_TAU = 16.0
NEG_INF = -1e30
EPS = 1e-6

LANES = 128

C_QA = 0
C_KA = C_QA + A_WIDTH
C_VA = C_KA + A_KV_WIDTH
C_GA = C_VA + A_KV_WIDTH
C_QB = C_GA + A_WIDTH
C_KB = C_QB + B_QK_WIDTH
C_VB = C_KB + B_QK_WIDTH
C_GB = C_VB + B_WIDTH
C_AB = C_GB + B_WIDTH
IN_COLS = C_AB + GATE_RANK

PROMPT_TILE = 512
GLA_SPAN = 256
SAMPLE_BATCH_TILE = 16
VMEM_LIMIT_BYTES = 56 * 1024 * 1024


def _dot(a, b):
    return jnp.dot(a, b, preferred_element_type=F32)


def _dot_nt(a, b):
    return lax.dot_general(a, b, (((1,), (1,)), ((), ())), preferred_element_type=F32)


def _dot_tn(a, b):
    return lax.dot_general(a, b, (((0,), (0,)), ((), ())), preferred_element_type=F32)


def _split_bf16(x):
    hi = x.astype(BF16)
    lo = (x - hi.astype(F32)).astype(BF16)
    return hi, lo


def _rms(x, g):
    ms = jnp.mean(x * x, axis=-1, keepdims=True)
    return x * lax.rsqrt(ms + EPS) * g


def _sigmoid(x):
    return 1.0 / (1.0 + jnp.exp(-x))


def _silu(x):
    return x * _sigmoid(x)


def _log_sigmoid(x):
    return jnp.minimum(x, 0.0) - jnp.log(1.0 + jnp.exp(-jnp.abs(x)))


def _block_mask(n, block, lower):
    r = lax.broadcasted_iota(jnp.int32, (n, n), 0)
    c = lax.broadcasted_iota(jnp.int32, (n, n), 1)
    m = (r // block) == (c // block)
    if lower:
        m = m & (r >= c)
    return jnp.where(m, 1.0, 0.0).astype(BF16)


def _project_a(xn, tab, win_ref, gqk_ref):
    qk = _dot(xn, win_ref[:, C_QA:C_VA])
    seg = _block_mask(256, A_HEAD_DIM, lower=False)
    hi, lo = _split_bf16(qk * qk)
    parts = []
    for c0 in range(0, C_VA, 256):
        c1 = min(c0 + 256, C_VA)
        sg = seg[: c1 - c0, : c1 - c0]
        parts.append(_dot(hi[:, c0:c1], sg) + _dot(lo[:, c0:c1], sg))
    ss = jnp.concatenate(parts, axis=1)
    qkn = qk * lax.rsqrt(ss * (1.0 / A_HEAD_DIM) + EPS) * gqk_ref[...]
    head_lane = lax.broadcasted_iota(jnp.int32, (1, LANES), 1) % A_HEAD_DIM
    cos = jnp.where(head_lane < ROT_DIM, tab, 1.0)
    sin = jnp.where(head_lane < ROT_DIM, pltpu.roll(tab, LANES - ROT_DIM, 1), 0.0)
    first_half = head_lane < ROT_DIM // 2
    slabs = []
    for s in range(C_VA // LANES):
        x = qkn[:, s * LANES:(s + 1) * LANES]
        up = pltpu.roll(x, LANES - ROT_DIM // 2, 1)
        dn = pltpu.roll(x, ROT_DIM // 2, 1)
        slabs.append(x * cos + jnp.where(first_half, up, dn) * sin)
    q = jnp.concatenate(slabs[:A_WIDTH // LANES], axis=1)
    k = slabs[A_WIDTH // LANES]
    v = _dot(xn, win_ref[:, C_VA:C_GA])
    return q, k, v


def _kv_operands(k, v, lo_half):
    ksw = pltpu.roll(k, A_HEAD_DIM, 1)
    vsw = pltpu.roll(v, A_HEAD_DIM, 1)
    one = jnp.ones_like(v)
    kd = (jnp.where(lo_half, k, ksw).astype(BF16), jnp.where(lo_half, ksw, k).astype(BF16))
    vx = (jnp.concatenate([jnp.where(lo_half, v, one), jnp.where(lo_half, one, vsw)], axis=1).astype(BF16),
          jnp.concatenate([jnp.where(lo_half, vsw, one), jnp.where(lo_half, one, v)], axis=1).astype(BF16))
    return kd, vx


def _attn_scores(q_blk, j, kd, lo_half):
    a = q_blk[:, (2 * j) * LANES:(2 * j + 1) * LANES]
    b = q_blk[:, (2 * j + 1) * LANES:(2 * j + 2) * LANES]
    z = jnp.zeros_like(a)
    stack = jnp.concatenate([jnp.where(lo_half, a, z), jnp.where(lo_half, z, a),
                             jnp.where(lo_half, b, z), jnp.where(lo_half, z, b)], axis=0).astype(BF16)
    return _dot_nt(stack, kd)


def _attn_probs(s, j, bias, sink_ref, layer):
    rows = s.shape[0] // A_GROUP
    if bias is not None:
        s = s + bias
    sk = jnp.concatenate([jnp.full((rows, 1), sink_ref[layer, A_GROUP * j + g], F32) for g in range(A_GROUP)],
                         axis=0)
    m = jnp.maximum(jnp.max(s, axis=-1, keepdims=True), sk)
    return jnp.exp(s - m).astype(BF16), jnp.exp(sk - m)


def _attn_values(p, de, vx, lo_half):
    rows = p.shape[0] // A_GROUP
    r = _dot(p, vx)
    r_lo, r_hi = r[:, :LANES], r[:, LANES:]

    def slab(g0):
        e = slice(g0 * rows, (g0 + 1) * rows)
        o = slice((g0 + 1) * rows, (g0 + 2) * rows)
        return jnp.where(lo_half, r_lo[e] / (r_hi[e] + de[e]), r_hi[o] / (r_lo[o] + de[o]))

    return slab(0), slab(2)


def _attn_block(q_blk, j, kd, vx, bias, sink_ref, layer, lo_half):
    p, de = _attn_probs(_attn_scores(q_blk, j, kd, lo_half), j, bias, sink_ref, layer)
    return _attn_values(p, de, vx, lo_half)


def _gla_inputs(xn, win_ref, wab_ref, wgu_ref, bg_ref, block):
    rows = xn.shape[0]
    qb = _dot(xn, win_ref[:, C_QB:C_KB])
    kb = _dot(xn, win_ref[:, C_KB:C_VB])
    vb = _dot(xn, win_ref[:, C_VB:C_GB])
    ab = _dot(xn, wab_ref[...])
    logg = _log_sigmoid(_dot(ab.astype(BF16), wgu_ref[...]) + bg_ref[...]) * (1.0 / GATE_TAU)
    g_hi, g_lo = _split_bf16(logg)
    tri = _block_mask(rows, block, lower=True)
    ones = _block_mask(rows, block, lower=False)
    b = _dot(tri, g_hi) + _dot(tri, g_lo)
    b_last = _dot(ones, g_hi) + _dot(ones, g_lo)
    qt = qb * (jnp.exp(b) * (B_KEY_DIM ** -0.5))
    kt = kb * jnp.exp(-b)
    ke = kb * jnp.exp(b_last - b)
    return qt, kt, ke, vb, logg


def _head_stack(x, head_mask):
    z = jnp.zeros_like(x)
    return jnp.concatenate([jnp.where(head_mask[h], x, z) for h in range(B_HEADS)], axis=0).astype(BF16)


def _finish_layer(h, oa, ob, xn, p, win_ref, glag_ref, wout_ref, peg_ref, wpe_ref, wpg_ref):
    ga = _dot(xn, win_ref[:, C_GA:C_QB])
    gb = _dot(xn, win_ref[:, C_GB:C_AB])
    obn = jnp.concatenate(
        [_rms(ob[:, hh * B_VAL_DIM:(hh + 1) * B_VAL_DIM], glag_ref[...]) for hh in range(B_HEADS)], axis=1)
    mix = jnp.concatenate([oa * _silu(ga), obn * _silu(gb)], axis=1).astype(BF16)
    h1 = h + _dot(mix, wout_ref[...])
    gate = _sigmoid(_dot(_rms(h1, peg_ref[...]).astype(BF16), wpg_ref[...]))
    return h1 + gate * _dot(p.astype(BF16), wpe_ref[...])


def _gla_decays(kbt, abt, wgut_ref, bgc_ref):
    tile = kbt.shape[1]
    loggt = _log_sigmoid(_dot(wgut_ref[...], abt) + bgc_ref[...]) * (1.0 / GATE_TAU)
    g_hi, g_lo = _split_bf16(loggt)
    r = lax.broadcasted_iota(jnp.int32, (GLA_SPAN, GLA_SPAN), 0)
    c = lax.broadcasted_iota(jnp.int32, (GLA_SPAN, GLA_SPAN), 1)
    upper = jnp.where(((r // CHUNK) == (c // CHUNK)) & (r <= c), 1.0, 0.0).astype(BF16)
    bt = jnp.concatenate(
        [_dot(g_hi[:, s0:s0 + GLA_SPAN], upper) + _dot(g_lo[:, s0:s0 + GLA_SPAN], upper)
         for s0 in range(0, tile, GLA_SPAN)], axis=1)
    ktt = (kbt * jnp.exp(-bt)).astype(BF16)
    pair = 2 * CHUNK
    cols = [bt[:, (c + 1) * CHUNK - 1:(c + 1) * CHUNK] for c in range(tile // CHUNK)]
    lane = lax.broadcasted_iota(jnp.int32, (1, pair), 1)
    ket = jnp.concatenate(
        [kbt[:, pc * pair:(pc + 1) * pair]
         * jnp.exp(jnp.where(lane < CHUNK, cols[2 * pc], cols[2 * pc + 1]) - bt[:, pc * pair:(pc + 1) * pair])
         for pc in range(tile // pair)], axis=1)
    return bt, ktt, ket, cols


def _gla_chunks(qt, vb, ktt, ket, cols, stk):
    tile = qt.shape[0]
    n_chunks = tile // CHUNK
    pair = 2 * CHUNK
    head_of_lane = lax.broadcasted_iota(jnp.int32, (1, B_QK_WIDTH), 1) // B_KEY_DIM
    head_mask = [head_of_lane == hh for hh in range(B_HEADS)]
    lane = lax.broadcasted_iota(jnp.int32, (1, pair), 1)
    t_row = lax.broadcasted_iota(jnp.int32, (CHUNK, pair), 0)
    s_col = lax.broadcasted_iota(jnp.int32, (CHUNK, pair), 1)
    own = [lane < CHUNK, lane >= CHUNK]
    causal = [s_col <= t_row, (s_col >= CHUNK) & (s_col - CHUNK <= t_row)]
    v_heads = [[vb[(c // 2) * pair:(c // 2 + 1) * pair, hh * B_VAL_DIM:(hh + 1) * B_VAL_DIM]
                for hh in range(B_HEADS)] for c in range(n_chunks)]

    updates = []
    for c in range(n_chunks):
        ps = slice((c // 2) * pair, (c // 2 + 1) * pair)
        ket_c = jnp.where(own[c % 2], ket[:, ps], 0.0).astype(BF16)
        updates.append(jnp.concatenate(
            [_dot(ket_c[hh * B_KEY_DIM:(hh + 1) * B_KEY_DIM], v_heads[c][hh]) for hh in range(B_HEADS)], axis=0))
    states = []
    for c in range(n_chunks):
        states.append(stk.astype(BF16))
        stk = stk * jnp.exp(cols[c]) + updates[c]
    results = []
    for c in range(n_chunks):
        ps = slice((c // 2) * pair, (c // 2 + 1) * pair)
        q_stack = _head_stack(qt[c * CHUNK:(c + 1) * CHUNK], head_mask)
        results.append(_dot(q_stack, jnp.concatenate([states[c], ktt[:, ps]], axis=1)))
    ob_rows = []
    for c in range(n_chunks):
        outs = []
        for hh in range(B_HEADS):
            hr = slice(hh * CHUNK, (hh + 1) * CHUNK)
            a_h = jnp.where(causal[c % 2], results[c][hr, B_VAL_DIM:], 0.0).astype(BF16)
            outs.append(_dot(a_h, v_heads[c][hh]) + results[c][hr, :B_VAL_DIM])
        ob_rows.append(jnp.concatenate(outs, axis=1))
    return jnp.concatenate(ob_rows, axis=0), stk


def _prompt_kernel(sink_ref, h_ref, p_ref, rope_ref, ng_ref, win_ref, wab_ref, gqk_ref, wgu_ref, bg_ref, glag_ref,
                   wout_ref, peg_ref, wpe_ref, wpg_ref, wkbt_ref, wabt_ref, wgut_ref, bgc_ref,
                   ho_ref, nk_ref, nv_ref, st_ref,
                   kd_scr, vx_scr, st_scr, *, layer):
    tile = PROMPT_TILE
    n_chunks = tile // CHUNK
    band = WINDOW + CHUNK
    i = pl.program_id(0)
    rd = (i + 1) % 2
    wr = i % 2

    @pl.when(i == 0)
    def _():
        kd_scr[rd] = jnp.zeros(kd_scr.shape[1:], BF16)
        vx_scr[rd] = jnp.zeros(vx_scr.shape[1:], BF16)
        st_scr[rd] = jnp.zeros(st_scr.shape[1:], F32)

    h = h_ref[...]
    xn = _rms(h, ng_ref[...]).astype(BF16)
    lo_half = lax.broadcasted_iota(jnp.int32, (1, LANES), 1) < A_HEAD_DIM

    q, k, v = _project_a(xn, rope_ref[...].T, win_ref, gqk_ref)
    nk_ref[...] = k[tile - WINDOW:, :]
    nv_ref[...] = v[tile - WINDOW:, :]
    kd, vx = _kv_operands(k, v, lo_half)
    kd_ext = [jnp.concatenate([kd_scr[rd, j], kd[j]], axis=0) for j in range(A_KV_HEADS)]
    vx_ext = [jnp.concatenate([vx_scr[rd, j], vx[j]], axis=0) for j in range(A_KV_HEADS)]
    key_blk = lax.broadcasted_iota(jnp.int32, (1, band), 1) // CHUNK
    blocks = [(c, j) for c in range(n_chunks) for j in range(A_KV_HEADS)]
    scores = [_attn_scores(q[c * CHUNK:(c + 1) * CHUNK, :], j, kd_ext[j][c * CHUNK:c * CHUNK + band, :], lo_half)
              for c, j in blocks]
    proj_cols = [(C_GA, C_GA + 256), (C_GA + 256, C_QB), (C_GB, C_GB + 256), (C_GB + 256, C_AB),
                 (C_QB, C_KB), (C_VB, C_VB + 256), (C_VB + 256, C_GB)]
    proj = []
    probs = []
    for n, ((c, j), s) in enumerate(zip(blocks, scores)):
        first_key_chunk = i * n_chunks + c - WINDOW // CHUNK
        bias = jnp.where(key_blk + first_key_chunk >= 0, 0.0, NEG_INF)
        probs.append(_attn_probs(s, j, bias, sink_ref, layer))
        if n % 2 == 1 and len(proj) < len(proj_cols):
            c0, c1 = proj_cols[len(proj)]
            proj.append(_dot(xn, win_ref[:, c0:c1]))
    while len(proj) < len(proj_cols):
        c0, c1 = proj_cols[len(proj)]
        proj.append(_dot(xn, win_ref[:, c0:c1]))
    ga = jnp.concatenate(proj[0:2], axis=1)
    gb = jnp.concatenate(proj[2:4], axis=1)
    qb = proj[4]
    vb = jnp.concatenate(proj[5:7], axis=1).astype(BF16)
    kbt = _dot_nt(wkbt_ref[...], xn)
    abt = _dot_nt(wabt_ref[...], xn).astype(BF16)
    slabs = [_attn_values(p, de, vx_ext[j][c * CHUNK:c * CHUNK + band, :], lo_half)
             for (c, j), (p, de) in zip(blocks, probs)]
    oa = jnp.concatenate(
        [jnp.concatenate([sl for j in range(A_KV_HEADS) for sl in slabs[c * A_KV_HEADS + j]], axis=1)
         for c in range(n_chunks)], axis=0)
    for j in range(A_KV_HEADS):
        kd_scr[wr, j] = kd[j][tile - WINDOW:, :]
        vx_scr[wr, j] = vx[j][tile - WINDOW:, :]

    bt, ktt, ket, cols = _gla_decays(kbt, abt, wgut_ref, bgc_ref)
    mix_a = (oa * _silu(ga)).astype(BF16)
    h1 = h + _dot(mix_a, wout_ref[0:A_WIDTH, :])
    qt = qb * (jnp.exp(bt.T) * (B_KEY_DIM ** -0.5))
    pe = _dot(p_ref[...].astype(BF16), wpe_ref[...])
    ob, stk = _gla_chunks(qt, vb, ktt, ket, cols, st_scr[rd])
    st_scr[wr] = stk
    st_ref[...] = stk

    obn = jnp.concatenate(
        [_rms(ob[:, hh * B_VAL_DIM:(hh + 1) * B_VAL_DIM], glag_ref[...]) for hh in range(B_HEADS)], axis=1)
    h1 = h1 + _dot((obn * _silu(gb)).astype(BF16), wout_ref[A_WIDTH:, :])
    gate = _sigmoid(_dot(_rms(h1, peg_ref[...]).astype(BF16), wpg_ref[...]))
    ho_ref[...] = h1 + gate * pe


def _sample_kernel(sink_ref, h_ref, p_ref, rope_ref, ck_ref, cv_ref, s0_ref, ng_ref, win_ref, wab_ref, gqk_ref, wgu_ref,
                   bg_ref, glag_ref, wout_ref, peg_ref, wpe_ref, wpg_ref,
                   ho_ref, nk_ref, nv_ref, st_ref, *, layer, dec_seq):
    nb = SAMPLE_BATCH_TILE
    h = h_ref[...]
    xn = _rms(h, ng_ref[...]).astype(BF16)
    lo_half = lax.broadcasted_iota(jnp.int32, (1, LANES), 1) < A_HEAD_DIM

    q, k, v = _project_a(xn, rope_ref[...].T, win_ref, gqk_ref)
    nk_ref[...] = k
    nv_ref[...] = v
    kd_new, vx_new = _kv_operands(k, v, lo_half)
    rows_of = [slice(bi * dec_seq, (bi + 1) * dec_seq) for bi in range(nb)]
    blocks = [(bi, j) for bi in range(nb) for j in range(A_KV_HEADS)]
    old = [_kv_operands(ck_ref[bi], cv_ref[bi], lo_half) for bi in range(nb)]
    scores = [_attn_scores(q[rows_of[bi]], j,
                           jnp.concatenate([old[bi][0][j], kd_new[j][rows_of[bi]]], axis=0), lo_half)
              for bi, j in blocks]
    probs = [_attn_probs(s, j, None, sink_ref, layer) for (bi, j), s in zip(blocks, scores)]
    slabs = [_attn_values(p, de, jnp.concatenate([old[bi][1][j], vx_new[j][rows_of[bi]]], axis=0), lo_half)
             for (bi, j), (p, de) in zip(blocks, probs)]
    oa = jnp.concatenate(
        [jnp.concatenate([sl for j in range(A_KV_HEADS) for sl in slabs[bi * A_KV_HEADS + j]], axis=1)
         for bi in range(nb)], axis=0)

    qt, kt, ke, vb, g_all = _gla_inputs(xn, win_ref, wab_ref, wgu_ref, bg_ref, dec_seq)
    head_of_lane = lax.broadcasted_iota(jnp.int32, (1, B_QK_WIDTH), 1) // B_KEY_DIM
    head_mask = [head_of_lane == hh for hh in range(B_HEADS)]
    causal = (lax.broadcasted_iota(jnp.int32, (dec_seq, dec_seq), 0)
              >= lax.broadcasted_iota(jnp.int32, (dec_seq, dec_seq), 1))
    ones_rhs = jnp.ones((dec_seq, LANES), BF16)
    vb = vb.astype(BF16)
    g_hi, g_lo = _split_bf16(g_all)
    q_stacks = [_head_stack(qt[rs], head_mask) for rs in rows_of]
    atts = [_dot_nt(q_stacks[bi], kt[rows_of[bi]].astype(BF16)) for bi in range(nb)]
    inters = [_dot(q_stacks[bi], s0_ref[bi].astype(BF16)) for bi in range(nb)]
    totals = [_dot_tn(g_hi[rs], ones_rhs) + _dot_tn(g_lo[rs], ones_rhs) for rs in rows_of]
    upd_fulls = [_dot_tn(ke[rs].astype(BF16), vb[rs]) for rs in rows_of]
    atts = [jnp.concatenate([jnp.where(causal, a[hh * dec_seq:(hh + 1) * dec_seq], 0.0)
                             for hh in range(B_HEADS)], axis=0).astype(BF16) for a in atts]
    intras = [_dot(atts[bi], vb[rows_of[bi]]) for bi in range(nb)]
    ob = jnp.concatenate(
        [jnp.concatenate([intras[bi][hh * dec_seq:(hh + 1) * dec_seq, hh * B_VAL_DIM:(hh + 1) * B_VAL_DIM]
                          + inters[bi][hh * dec_seq:(hh + 1) * dec_seq] for hh in range(B_HEADS)], axis=1)
         for bi in range(nb)], axis=0)
    for bi in range(nb):
        upd = jnp.concatenate(
            [upd_fulls[bi][hh * B_KEY_DIM:(hh + 1) * B_KEY_DIM, hh * B_VAL_DIM:(hh + 1) * B_VAL_DIM]
             for hh in range(B_HEADS)], axis=0)
        st_ref[bi] = s0_ref[bi] * jnp.exp(totals[bi]) + upd

    ho_ref[...] = _finish_layer(h, oa, ob, xn, p_ref[...], win_ref, glag_ref, wout_ref, peg_ref,
                                wpe_ref, wpg_ref)


def _rope_table(pos):
    half = ROT_DIM // 2
    inv = jnp.power(jnp.float32(ROPE_THETA), -jnp.arange(half, dtype=F32) * (2.0 / ROT_DIM))
    ang = inv[:, None] * pos.astype(F32)[None, :]
    cos, sin = jnp.cos(ang), jnp.sin(ang)
    head = jnp.concatenate([cos, cos, -sin, sin, jnp.zeros((A_HEAD_DIM - 2 * ROT_DIM, pos.shape[0]), F32)], axis=0)
    return jnp.concatenate([head, head], axis=0)


def _layer_spec(layer, shape):
    return pl.BlockSpec((None,) + shape, lambda i: (layer,) + (0,) * len(shape))


def _weight_specs(layer):
    return [
        _layer_spec(layer, (1, D_MODEL)),
        pl.BlockSpec((None, D_MODEL, C_AB), lambda i: (layer, 0, 0)),
        _layer_spec(layer, (D_MODEL, LANES)),
        _layer_spec(layer, (1, C_VA)),
        _layer_spec(layer, (LANES, B_QK_WIDTH)),
        _layer_spec(layer, (1, B_QK_WIDTH)),
        _layer_spec(layer, (1, B_VAL_DIM)),
        _layer_spec(layer, (A_WIDTH + B_WIDTH, D_MODEL)),
        _layer_spec(layer, (1, D_MODEL)),
        _layer_spec(layer, (D_PLE, D_MODEL)),
        _layer_spec(layer, (D_MODEL, D_MODEL)),
    ]


def _prompt_layer(layer, h, p, rope, sinks, weights, weights_t):
    seq = h.shape[0]
    tile = PROMPT_TILE
    assert seq % tile == 0 and tile % GLA_SPAN == 0 and GLA_SPAN % (2 * CHUNK) == 0 and tile >= WINDOW
    row = lambda w: pl.BlockSpec((tile, w), lambda i: (i, 0))
    const = lambda shape: pl.BlockSpec(shape, lambda i: (0,) * len(shape))
    return pl.pallas_call(
        functools.partial(_prompt_kernel, layer=layer),
        grid=(seq // tile,),
        in_specs=[pl.BlockSpec(memory_space=pltpu.SMEM), row(D_MODEL),
                  pl.BlockSpec((None, tile, D_PLE), lambda i: (layer, i, 0)),
                  pl.BlockSpec((LANES, tile), lambda i: (0, i))]
        + _weight_specs(layer)
        + [_layer_spec(layer, (B_QK_WIDTH, D_MODEL)),
           _layer_spec(layer, (GATE_RANK, D_MODEL)),
           _layer_spec(layer, (B_QK_WIDTH, GATE_RANK)),
           _layer_spec(layer, (B_QK_WIDTH, 1))],
        out_specs=[row(D_MODEL), const((WINDOW, A_KV_WIDTH)), const((WINDOW, A_KV_WIDTH)),
                   const((B_QK_WIDTH, B_VAL_DIM))],
        out_shape=[jax.ShapeDtypeStruct((seq, D_MODEL), F32),
                   jax.ShapeDtypeStruct((WINDOW, A_KV_WIDTH), F32),
                   jax.ShapeDtypeStruct((WINDOW, A_KV_WIDTH), F32),
                   jax.ShapeDtypeStruct((B_QK_WIDTH, B_VAL_DIM), F32)],
        scratch_shapes=[pltpu.VMEM((2, A_KV_HEADS, WINDOW, LANES), BF16),
                        pltpu.VMEM((2, A_KV_HEADS, WINDOW, 2 * LANES), BF16),
                        pltpu.VMEM((2, B_QK_WIDTH, B_VAL_DIM), F32)],
        compiler_params=pltpu.CompilerParams(dimension_semantics=("arbitrary",),
                                             vmem_limit_bytes=VMEM_LIMIT_BYTES),
        name="prompt_layer",
    )(sinks, h, p, rope, *weights, *weights_t)


def _sample_layer(layer, h, p, rope, ck, cv, s0, sinks, weights, dec_seq):
    n_seq = ck.shape[1]
    nb = SAMPLE_BATCH_TILE
    assert n_seq % nb == 0
    rows = nb * dec_seq
    row = lambda w: pl.BlockSpec((rows, w), lambda i: (i, 0))
    per_seq_in = lambda a, b: pl.BlockSpec((None, nb, a, b), lambda i: (layer, i, 0, 0))
    return pl.pallas_call(
        functools.partial(_sample_kernel, layer=layer, dec_seq=dec_seq),
        grid=(n_seq // nb,),
        in_specs=[pl.BlockSpec(memory_space=pltpu.SMEM), row(D_MODEL),
                  pl.BlockSpec((None, rows, D_PLE), lambda i: (layer, i, 0)),
                  pl.BlockSpec((LANES, rows), lambda i: (0, 0)),
                  per_seq_in(WINDOW, A_KV_WIDTH), per_seq_in(WINDOW, A_KV_WIDTH),
                  per_seq_in(B_QK_WIDTH, B_VAL_DIM)]
        + _weight_specs(layer),
        out_specs=[row(D_MODEL), row(A_KV_WIDTH), row(A_KV_WIDTH),
                   pl.BlockSpec((nb, B_QK_WIDTH, B_VAL_DIM), lambda i: (i, 0, 0))],
        out_shape=[jax.ShapeDtypeStruct((n_seq * dec_seq, D_MODEL), F32),
                   jax.ShapeDtypeStruct((n_seq * dec_seq, A_KV_WIDTH), F32),
                   jax.ShapeDtypeStruct((n_seq * dec_seq, A_KV_WIDTH), F32),
                   jax.ShapeDtypeStruct((n_seq, B_QK_WIDTH, B_VAL_DIM), F32)],
        compiler_params=pltpu.CompilerParams(dimension_semantics=("arbitrary",),
                                             vmem_limit_bytes=VMEM_LIMIT_BYTES),
        name="sample_layer",
    )(sinks, h, p, rope, ck, cv, s0, *weights)


def kernel(x_prompt, x_sample, cache_k, cache_v, state_gla, p_prompt, p_sample, norm_g, w_in, q_norm_g, k_norm_g,
           sinks, w_gate_up, b_gate, gla_norm_g, w_out, pe_norm_g, w_pe, w_pg):
    batch, seq, _ = x_prompt.shape
    n_seq, dec_seq, _ = x_sample.shape
    assert batch == 1 and cache_k.shape[2] == WINDOW

    rope_p = _rope_table(jnp.arange(seq, dtype=jnp.int32))
    rope_s = jnp.tile(_rope_table(PAST_LEN + jnp.arange(dec_seq, dtype=jnp.int32)), (1, SAMPLE_BATCH_TILE))

    w_ab = w_in[:, :, C_AB:IN_COLS]
    wab_b = jnp.pad(w_ab, ((0, 0), (0, 0), (0, LANES - GATE_RANK))).astype(BF16)
    wgu_b = jnp.pad(w_gate_up, ((0, 0), (0, LANES - GATE_RANK), (0, 0))).astype(BF16)
    gqk = jnp.concatenate([jnp.tile(q_norm_g, (1, A_HEADS)) * (A_HEAD_DIM ** -0.5),
                           jnp.tile(k_norm_g, (1, A_KV_HEADS))], axis=1)
    vec = lambda a: a[:, None, :]
    weights = (vec(norm_g), w_in.astype(BF16), wab_b, vec(gqk), wgu_b, vec(b_gate), vec(gla_norm_g), w_out.astype(BF16),
               vec(pe_norm_g), w_pe.astype(BF16), w_pg.astype(BF16))
    weights_t = (jnp.swapaxes(w_in[:, :, C_KB:C_VB], 1, 2).astype(BF16),
                 jnp.swapaxes(w_ab, 1, 2).astype(BF16),
                 jnp.swapaxes(w_gate_up, 1, 2).astype(BF16),
                 b_gate[:, :, None])

    hp = x_prompt.reshape(seq, D_MODEL)
    hs = x_sample.reshape(n_seq * dec_seq, D_MODEL)
    ck = cache_k.reshape(DEPTH, n_seq, WINDOW, A_KV_WIDTH)
    cv = cache_v.reshape(DEPTH, n_seq, WINDOW, A_KV_WIDTH)
    s0 = state_gla.reshape(DEPTH, n_seq, B_QK_WIDTH, B_VAL_DIM)
    pp = p_prompt.reshape(DEPTH, seq, D_PLE)
    ps = p_sample.reshape(DEPTH, n_seq * dec_seq, D_PLE)

    kp_l, vp_l, sp_l, ks_l, vs_l, ss_l = [], [], [], [], [], []
    for layer in range(DEPTH):
        hp, kn, vn, sn = _prompt_layer(layer, hp, pp, rope_p, sinks, weights, weights_t)
        kp_l.append(kn); vp_l.append(vn); sp_l.append(sn)
        hs, kn, vn, sn = _sample_layer(layer, hs, ps, rope_s, ck, cv, s0, sinks, weights, dec_seq)
        ks_l.append(kn); vs_l.append(vn); ss_l.append(sn)

    return (hp.reshape(batch, seq, D_MODEL),
            hs.reshape(n_seq, dec_seq, D_MODEL),
            jnp.stack(kp_l).reshape(DEPTH, batch, WINDOW, A_KV_HEADS, A_HEAD_DIM),
            jnp.stack(vp_l).reshape(DEPTH, batch, WINDOW, A_KV_HEADS, A_HEAD_DIM),
            jnp.stack(sp_l).reshape(DEPTH, batch, B_HEADS, B_KEY_DIM, B_VAL_DIM),
            jnp.stack(ks_l).reshape(DEPTH, n_seq, dec_seq, A_KV_HEADS, A_HEAD_DIM),
            jnp.stack(vs_l).reshape(DEPTH, n_seq, dec_seq, A_KV_HEADS, A_HEAD_DIM),
            jnp.stack(ss_l).reshape(DEPTH, n_seq, B_HEADS, B_KEY_DIM, B_VAL_DIM))
```

```python
import functools

import jax
import jax.numpy as jnp
from jax import lax
from jax.experimental import pallas as pl
from jax.experimental.pallas import tpu as pltpu

F32 = jnp.float32
BF16 = jnp.bfloat16

D_MODEL = 1024
DEPTH = 4
PAST_LEN = 1024
CHUNK = 64
D_PLE = 256
A_HEADS = 8
A_KV_HEADS = 2
A_HEAD_DIM = 64
A_GROUP = A_HEADS // A_KV_HEADS
A_WIDTH = A_HEADS * A_HEAD_DIM
A_KV_WIDTH = A_KV_HEADS * A_HEAD_DIM
WINDOW = 128
ROT_DIM = A_HEAD_DIM // 4
ROPE_THETA = 500000.0
B_HEADS = 4
B_KEY_DIM = 64
B_VAL_DIM = 128
B_QK_WIDTH = B_HEADS * B_KEY_DIM
B_WIDTH = B_HEADS * B_VAL_DIM
GATE_RANK = 16
GATE_TAU = 16.0
NEG_INF = -1e30
EPS = 1e-6

LANES = 128

C_QA = 0
C_KA = C_QA + A_WIDTH
C_VA = C_KA + A_KV_WIDTH
C_GA = C_VA + A_KV_WIDTH
C_QB = C_GA + A_WIDTH
C_KB = C_QB + B_QK_WIDTH
C_VB = C_KB + B_QK_WIDTH
C_GB = C_VB + B_WIDTH
C_AB = C_GB + B_WIDTH
IN_COLS = C_AB + GATE_RANK

PROMPT_TILE = 512
GLA_SPAN = 256
FINISH_ROWS = 256
SAMPLE_BATCH_TILE = 16
VMEM_LIMIT_BYTES = 56 * 1024 * 1024


def _dot(a, b):
    return jnp.dot(a, b, preferred_element_type=F32)


def _dot_nt(a, b):
    return lax.dot_general(a, b, (((1,), (1,)), ((), ())), preferred_element_type=F32)


def _dot_tn(a, b):
    return lax.dot_general(a, b, (((0,), (0,)), ((), ())), preferred_element_type=F32)


def _split_bf16(x):
    hi = x.astype(BF16)
    lo = (x - hi.astype(F32)).astype(BF16)
    return hi, lo


def _rms(x, g):
    ms = jnp.mean(x * x, axis=-1, keepdims=True)
    return x * lax.rsqrt(ms + EPS) * g


def _sigmoid(x):
    return 0.5 + 0.5 * jnp.tanh(0.5 * x)


def _silu(x):
    return x * _sigmoid(x)


def _log_sigmoid(x):
    return jnp.minimum(x, 0.0) - jnp.log(1.0 + jnp.exp(-jnp.abs(x)))


def _block_mask(n, block, lower):
    r = lax.broadcasted_iota(jnp.int32, (n, n), 0)
    c = lax.broadcasted_iota(jnp.int32, (n, n), 1)
    m = (r // block) == (c // block)
    if lower:
        m = m & (r >= c)
    return jnp.where(m, 1.0, 0.0).astype(BF16)


def _project_a(xn, tab, win_ref, gqk_ref):
    return _norm_rope(_dot(xn, win_ref[:, C_QA:C_GA]), tab, gqk_ref)


def _norm_rope(qkv, tab, gqk_ref):
    qk, v = qkv[:, :C_VA], qkv[:, C_VA:]
    seg = _block_mask(256, A_HEAD_DIM, lower=False)
    hi, lo = _split_bf16(qk * qk)
    parts = []
    for c0 in range(0, C_VA, 256):
        c1 = min(c0 + 256, C_VA)
        sg = seg[: c1 - c0, : c1 - c0]
        parts.append(_dot(hi[:, c0:c1], sg) + _dot(lo[:, c0:c1], sg))
    ss = jnp.concatenate(parts, axis=1)
    qkn = qk * lax.rsqrt(ss * (1.0 / A_HEAD_DIM) + EPS) * gqk_ref[...]
    head_lane = lax.broadcasted_iota(jnp.int32, (1, LANES), 1) % A_HEAD_DIM
    cos = jnp.where(head_lane < ROT_DIM, tab, 1.0)
    sin = jnp.where(head_lane < ROT_DIM, pltpu.roll(tab, LANES - ROT_DIM, 1), 0.0)
    first_half = head_lane < ROT_DIM // 2
    slabs = []
    for s in range(C_VA // LANES):
        x = qkn[:, s * LANES:(s + 1) * LANES]
        up = pltpu.roll(x, LANES - ROT_DIM // 2, 1)
        dn = pltpu.roll(x, ROT_DIM // 2, 1)
        slabs.append(x * cos + jnp.where(first_half, up, dn) * sin)
    q = jnp.concatenate(slabs[:A_WIDTH // LANES], axis=1)
    k = slabs[A_WIDTH // LANES]
    return q, k, v


def _kv_operands(k, v, lo_half):
    ksw = pltpu.roll(k, A_HEAD_DIM, 1)
    vsw = pltpu.roll(v, A_HEAD_DIM, 1)
    one = jnp.ones_like(v)
    kd = (jnp.where(lo_half, k, ksw).astype(BF16), jnp.where(lo_half, ksw, k).astype(BF16))
    vx = (jnp.concatenate([jnp.where(lo_half, v, one), jnp.where(lo_half, one, vsw)], axis=1).astype(BF16),
          jnp.concatenate([jnp.where(lo_half, vsw, one), jnp.where(lo_half, one, v)], axis=1).astype(BF16))
    return kd, vx


def _attn_scores(q_blk, j, kd, lo_half):
    a = q_blk[:, (2 * j) * LANES:(2 * j + 1) * LANES]
    b = q_blk[:, (2 * j + 1) * LANES:(2 * j + 2) * LANES]
    z = jnp.zeros_like(a)
    stack = jnp.concatenate([jnp.where(lo_half, a, z), jnp.where(lo_half, z, a),
                             jnp.where(lo_half, b, z), jnp.where(lo_half, z, b)], axis=0).astype(BF16)
    return _dot_nt(stack, kd)


def _attn_probs(s, j, bias, sink_ref, layer):
    rows = s.shape[0] // A_GROUP
    if bias is not None:
        s = s + bias
    sk = jnp.concatenate([jnp.full((rows, 1), sink_ref[layer, A_GROUP * j + g], F32) for g in range(A_GROUP)],
                         axis=0)
    m = jnp.maximum(jnp.max(s, axis=-1, keepdims=True), sk)
    return jnp.exp(s - m).astype(BF16), jnp.exp(sk - m)


def _attn_values(p, de, vx, lo_half):
    rows = p.shape[0] // A_GROUP
    r = _dot(p, vx)
    r_lo, r_hi = r[:, :LANES], r[:, LANES:]

    def slab(g0):
        e = slice(g0 * rows, (g0 + 1) * rows)
        o = slice((g0 + 1) * rows, (g0 + 2) * rows)
        return jnp.where(lo_half, r_lo[e] / (r_hi[e] + de[e]), r_hi[o] / (r_lo[o] + de[o]))

    return slab(0), slab(2)


def _attn_block(q_blk, j, kd, vx, bias, sink_ref, layer, lo_half):
    p, de = _attn_probs(_attn_scores(q_blk, j, kd, lo_half), j, bias, sink_ref, layer)
    return _attn_values(p, de, vx, lo_half)


def _gla_inputs(xn, win_ref, wab_ref, wgu_ref, bg_ref, block):
    rows = xn.shape[0]
    qb = _dot(xn, win_ref[:, C_QB:C_KB])
    kb = _dot(xn, win_ref[:, C_KB:C_VB])
    vb = _dot(xn, win_ref[:, C_VB:C_GB])
    ab = _dot(xn, wab_ref[...])
    logg = _log_sigmoid(_dot(ab.astype(BF16), wgu_ref[...]) + bg_ref[...]) * (1.0 / GATE_TAU)
    g_hi, g_lo = _split_bf16(logg)
    tri = _block_mask(rows, block, lower=True)
    ones = _block_mask(rows, block, lower=False)
    b = _dot(tri, g_hi) + _dot(tri, g_lo)
    b_last = _dot(ones, g_hi) + _dot(ones, g_lo)
    qt = qb * (jnp.exp(b) * (B_KEY_DIM ** -0.5))
    kt = kb * jnp.exp(-b)
    ke = kb * jnp.exp(b_last - b)
    return qt, kt, ke, vb, logg


def _head_stack(x, head_mask):
    z = jnp.zeros_like(x)
    return jnp.concatenate([jnp.where(head_mask[h], x, z) for h in range(B_HEADS)], axis=0).astype(BF16)


def _finish_layer(h, oa, ob, xn, p, win_ref, glag_ref, wout_ref, peg_ref, wpe_ref, wpg_ref):
    ga = _dot(xn, win_ref[:, C_GA:C_QB])
    gb = _dot(xn, win_ref[:, C_GB:C_AB])
    obn = jnp.concatenate(
        [_rms(ob[:, hh * B_VAL_DIM:(hh + 1) * B_VAL_DIM], glag_ref[...]) for hh in range(B_HEADS)], axis=1)
    mix = jnp.concatenate([oa * _silu(ga), obn * _silu(gb)], axis=1).astype(BF16)
    h1 = h + _dot(mix, wout_ref[...])
    gate = _sigmoid(_dot(_rms(h1, peg_ref[...]).astype(BF16), wpg_ref[...]))
    return h1 + gate * _dot(p.astype(BF16), wpe_ref[...])


def _gla_decays(kbt, abt, wgut_ref, bgc_ref):
    tile = kbt.shape[1]
    loggt = _log_sigmoid(_dot(wgut_ref[...], abt) + bgc_ref[...]) * (1.0 / GATE_TAU)
    g_hi, g_lo = _split_bf16(loggt)
    r = lax.broadcasted_iota(jnp.int32, (GLA_SPAN, GLA_SPAN), 0)
    c = lax.broadcasted_iota(jnp.int32, (GLA_SPAN, GLA_SPAN), 1)
    upper = jnp.where(((r // CHUNK) == (c // CHUNK)) & (r <= c), 1.0, 0.0).astype(BF16)
    bt = jnp.concatenate(
        [_dot(g_hi[:, s0:s0 + GLA_SPAN], upper) + _dot(g_lo[:, s0:s0 + GLA_SPAN], upper)
         for s0 in range(0, tile, GLA_SPAN)], axis=1)
    ktt = (kbt * jnp.exp(-bt)).astype(BF16)
    pair = 2 * CHUNK
    cols = [bt[:, (c + 1) * CHUNK - 1:(c + 1) * CHUNK] for c in range(tile // CHUNK)]
    lane = lax.broadcasted_iota(jnp.int32, (1, pair), 1)
    ket = jnp.concatenate(
        [kbt[:, pc * pair:(pc + 1) * pair]
         * jnp.exp(jnp.where(lane < CHUNK, cols[2 * pc], cols[2 * pc + 1]) - bt[:, pc * pair:(pc + 1) * pair])
         for pc in range(tile // pair)], axis=1)
    return bt, ktt, ket, cols


def _gla_chunk_jobs(qt, vb, ktt, ket, cols, stk, out):
    tile = qt.shape[0]
    n_chunks = tile // CHUNK
    pair = 2 * CHUNK
    head_of_lane = lax.broadcasted_iota(jnp.int32, (1, B_QK_WIDTH), 1) // B_KEY_DIM
    head_mask = [head_of_lane == hh for hh in range(B_HEADS)]
    lane = lax.broadcasted_iota(jnp.int32, (1, pair), 1)
    t_row = lax.broadcasted_iota(jnp.int32, (CHUNK, pair), 0)
    s_col = lax.broadcasted_iota(jnp.int32, (CHUNK, pair), 1)
    own = [lane < CHUNK, lane >= CHUNK]
    causal = [s_col <= t_row, (s_col >= CHUNK) & (s_col - CHUNK <= t_row)]
    v_heads = [[vb[(c // 2) * pair:(c // 2 + 1) * pair, hh * B_VAL_DIM:(hh + 1) * B_VAL_DIM]
                for hh in range(B_HEADS)] for c in range(n_chunks)]
    updates, states, results, ob_rows = {}, {}, {}, {}

    def update(c):
        ps = slice((c // 2) * pair, (c // 2 + 1) * pair)
        ket_c = jnp.where(own[c % 2], ket[:, ps], 0.0).astype(BF16)
        updates[c] = jnp.concatenate(
            [_dot(ket_c[hh * B_KEY_DIM:(hh + 1) * B_KEY_DIM], v_heads[c][hh]) for hh in range(B_HEADS)], axis=0)

    def chain():
        s = stk
        for c in range(n_chunks):
            states[c] = s.astype(BF16)
            s = s * jnp.exp(cols[c]) + updates[c]
        out["state"] = s

    def result(c):
        ps = slice((c // 2) * pair, (c // 2 + 1) * pair)
        q_stack = _head_stack(qt[c * CHUNK:(c + 1) * CHUNK], head_mask)
        results[c] = _dot(q_stack, jnp.concatenate([states[c], ktt[:, ps]], axis=1))

    def output(c):
        outs = []
        for hh in range(B_HEADS):
            hr = slice(hh * CHUNK, (hh + 1) * CHUNK)
            a_h = jnp.where(causal[c % 2], results[c][hr, B_VAL_DIM:], 0.0).astype(BF16)
            outs.append(_dot(a_h, v_heads[c][hh]) + results[c][hr, :B_VAL_DIM])
        ob_rows[c] = jnp.concatenate(outs, axis=1)
        if c == n_chunks - 1:
            out["ob"] = jnp.concatenate([ob_rows[cc] for cc in range(n_chunks)], axis=0)

    bind = lambda f, c: (lambda: f(c))
    return ([bind(update, c) for c in range(n_chunks)] + [chain]
            + [bind(result, c) for c in range(n_chunks)] + [bind(output, c) for c in range(n_chunks)])


def _trace_interleaved(jobs_a, jobs_b):
    na, nb = len(jobs_a), len(jobs_b)
    ia = ib = 0
    while ia < na or ib < nb:
        if ib >= nb or (ia < na and ia * nb <= ib * na):
            jobs_a[ia]()
            ia += 1
        else:
            jobs_b[ib]()
            ib += 1


def _prompt_kernel(sink_ref, h_ref, p_ref, rope_ref, ng_ref, win_ref, wab_ref, gqk_ref, wgu_ref, bg_ref, glag_ref,
                   wout_ref, peg_ref, wpe_ref, wpg_ref, wkbt_ref, wabt_ref, wgut_ref, bgc_ref,
                   ho_ref, nk_ref, nv_ref, st_ref,
                   kd_scr, vx_scr, st_scr, *, layer):
    tile = PROMPT_TILE
    n_chunks = tile // CHUNK
    band = WINDOW + CHUNK
    i = pl.program_id(0)
    rd = (i + 1) % 2
    wr = i % 2

    @pl.when(i == 0)
    def _():
        kd_scr[rd] = jnp.zeros(kd_scr.shape[1:], BF16)
        vx_scr[rd] = jnp.zeros(vx_scr.shape[1:], BF16)
        st_scr[rd] = jnp.zeros(st_scr.shape[1:], F32)

    h = h_ref[...]
    xn = _rms(h, ng_ref[...]).astype(BF16)
    lo_half = lax.broadcasted_iota(jnp.int32, (1, LANES), 1) < A_HEAD_DIM
    key_blk = lax.broadcasted_iota(jnp.int32, (1, band), 1) // CHUNK
    blocks = [(c, j) for c in range(n_chunks) for j in range(A_KV_HEADS)]
    n_blocks = len(blocks)
    v, scores, probs, slabs, gla = {}, {}, {}, {}, {}
    bind = lambda f, n: (lambda: f(n))

    qkv = _dot(xn, win_ref[:, C_QA:C_GA])

    def rope_tab():
        tab = rope_ref[...].T
        head_lane = lax.broadcasted_iota(jnp.int32, (1, LANES), 1) % A_HEAD_DIM
        v["cos"] = jnp.where(head_lane < ROT_DIM, tab, 1.0)
        v["sin"] = jnp.where(head_lane < ROT_DIM, pltpu.roll(tab, LANES - ROT_DIM, 1), 0.0)
        v["first_half"] = head_lane < ROT_DIM // 2

    def norm(g):
        width = min(256, C_VA - 256 * g)
        x = qkv[:, 256 * g:256 * g + width]
        seg = _block_mask(width, A_HEAD_DIM, lower=False)
        hi, lo = _split_bf16(x * x)
        ss = _dot(hi, seg) + _dot(lo, seg)
        xs = x * lax.rsqrt(ss * (1.0 / A_HEAD_DIM) + EPS) * gqk_ref[:, 256 * g:256 * g + width]
        for s in range(width // LANES):
            x1 = xs[:, s * LANES:(s + 1) * LANES]
            up = pltpu.roll(x1, LANES - ROT_DIM // 2, 1)
            dn = pltpu.roll(x1, ROT_DIM // 2, 1)
            v[f"slab{2 * g + s}"] = x1 * v["cos"] + jnp.where(v["first_half"], up, dn) * v["sin"]

    def kv_ops():
        k, vv = v["slab4"], qkv[:, C_VA:]
        nk_ref[...] = k[tile - WINDOW:, :]
        nv_ref[...] = vv[tile - WINDOW:, :]
        kd, vx = _kv_operands(k, vv, lo_half)
        v["kd_ext"] = [jnp.concatenate([kd_scr[rd, j], kd[j]], axis=0) for j in range(A_KV_HEADS)]
        v["vx_ext"] = [jnp.concatenate([vx_scr[rd, j], vx[j]], axis=0) for j in range(A_KV_HEADS)]
        for j in range(A_KV_HEADS):
            kd_scr[wr, j] = kd[j][tile - WINDOW:, :]
            vx_scr[wr, j] = vx[j][tile - WINDOW:, :]

    def proj_k():
        v["kbt"] = _dot_nt(wkbt_ref[...], xn)
        v["abt"] = _dot_nt(wabt_ref[...], xn).astype(BF16)

    def proj_qv():
        v["qb"] = _dot(xn, win_ref[:, C_QB:C_KB])
        v["vb"] = _dot(xn, win_ref[:, C_VB:C_GB]).astype(BF16)

    _trace_interleaved([rope_tab, bind(norm, 2), kv_ops, bind(norm, 0), bind(norm, 1)], [proj_k, proj_qv])

    def score(n):
        c, j = blocks[n]
        cs = slice(c * CHUNK, (c + 1) * CHUNK)
        q_blk = jnp.concatenate([v[f"slab{2 * j}"][cs], v[f"slab{2 * j + 1}"][cs]], axis=1)
        scores[n] = _attn_scores(q_blk, 0, v["kd_ext"][j][c * CHUNK:c * CHUNK + band, :], lo_half)

    def decays():
        gla["bt"], gla["ktt"], gla["ket"], gla["cols"] = _gla_decays(v["kbt"], v["abt"], wgut_ref, bgc_ref)

    def scale_q():
        gla["qt"] = v["qb"] * (jnp.exp(gla["bt"].T) * (B_KEY_DIM ** -0.5))

    _trace_interleaved([bind(score, n) for n in range(n_blocks)], [decays, scale_q])

    def prob(n):
        c, j = blocks[n]
        first_key_chunk = i * n_chunks + c - WINDOW // CHUNK
        bias = jnp.where(key_blk + first_key_chunk >= 0, 0.0, NEG_INF)
        probs[n] = _attn_probs(scores.pop(n), j, bias, sink_ref, layer)

    def value(n):
        c, j = blocks[n]
        slabs[n] = _attn_values(*probs.pop(n), v["vx_ext"][j][c * CHUNK:c * CHUNK + band, :], lo_half)

    def proj_ga():
        v["ga"] = _dot(xn, win_ref[:, C_GA:C_QB])

    def proj_gb():
        v["gb"] = _dot(xn, win_ref[:, C_GB:C_AB])

    def embed():
        v["pe"] = _dot(p_ref[...].astype(BF16), wpe_ref[...])

    gla_jobs = _gla_chunk_jobs(gla["qt"], v["vb"], gla["ktt"], gla["ket"], gla["cols"], st_scr[rd], gla)
    n_first = n_chunks + 1
    _trace_interleaved([bind(prob, n) for n in range(n_blocks)], [proj_ga, proj_gb, embed] + gla_jobs[:n_first])
    _trace_interleaved([bind(value, n) for n in range(n_blocks)], gla_jobs[n_first:])
    st_scr[wr] = gla["state"]
    st_ref[...] = gla["state"]
    ob, pe, ga, gb = gla["ob"], v["pe"], v["ga"], v["gb"]

    for r0 in range(0, tile, FINISH_ROWS):
        rs = slice(r0, r0 + FINISH_ROWS)
        oa = jnp.concatenate(
            [jnp.concatenate([sl for j in range(A_KV_HEADS) for sl in slabs[c * A_KV_HEADS + j]], axis=1)
             for c in range(r0 // CHUNK, (r0 + FINISH_ROWS) // CHUNK)], axis=0)
        h1 = h[rs] + _dot((oa * _silu(ga[rs])).astype(BF16), wout_ref[0:A_WIDTH, :])
        obn = jnp.concatenate(
            [_rms(ob[rs, hh * B_VAL_DIM:(hh + 1) * B_VAL_DIM], glag_ref[...]) for hh in range(B_HEADS)], axis=1)
        h2 = h1 + _dot((obn * _silu(gb[rs])).astype(BF16), wout_ref[A_WIDTH:, :])
        gate = _sigmoid(_dot(_rms(h2, peg_ref[...]).astype(BF16), wpg_ref[...]))
        ho_ref[rs, :] = h2 + gate * pe[rs]


def _sample_kernel(sink_ref, h_ref, p_ref, rope_ref, ck_ref, cv_ref, s0_ref, ng_ref, win_ref, wab_ref, gqk_ref, wgu_ref,
                   bg_ref, glag_ref, wout_ref, peg_ref, wpe_ref, wpg_ref,
                   ho_ref, nk_ref, nv_ref, st_ref, *, layer, dec_seq):
    nb = SAMPLE_BATCH_TILE
    h = h_ref[...]
    xn = _rms(h, ng_ref[...]).astype(BF16)
    lo_half = lax.broadcasted_iota(jnp.int32, (1, LANES), 1) < A_HEAD_DIM

    q, k, v = _project_a(xn, rope_ref[...].T, win_ref, gqk_ref)
    nk_ref[...] = k
    nv_ref[...] = v
    kd_new, vx_new = _kv_operands(k, v, lo_half)
    rows_of = [slice(bi * dec_seq, (bi + 1) * dec_seq) for bi in range(nb)]
    blocks = [(bi, j) for bi in range(nb) for j in range(A_KV_HEADS)]
    old = [_kv_operands(ck_ref[bi], cv_ref[bi], lo_half) for bi in range(nb)]
    scores = [_attn_scores(q[rows_of[bi]], j,
                           jnp.concatenate([old[bi][0][j], kd_new[j][rows_of[bi]]], axis=0), lo_half)
              for bi, j in blocks]
    probs = [_attn_probs(s, j, None, sink_ref, layer) for (bi, j), s in zip(blocks, scores)]
    slabs = [_attn_values(p, de, jnp.concatenate([old[bi][1][j], vx_new[j][rows_of[bi]]], axis=0), lo_half)
             for (bi, j), (p, de) in zip(blocks, probs)]
    oa = jnp.concatenate(
        [jnp.concatenate([sl for j in range(A_KV_HEADS) for sl in slabs[bi * A_KV_HEADS + j]], axis=1)
         for bi in range(nb)], axis=0)

    qt, kt, ke, vb, g_all = _gla_inputs(xn, win_ref, wab_ref, wgu_ref, bg_ref, dec_seq)
    head_of_lane = lax.broadcasted_iota(jnp.int32, (1, B_QK_WIDTH), 1) // B_KEY_DIM
    head_mask = [head_of_lane == hh for hh in range(B_HEADS)]
    causal = (lax.broadcasted_iota(jnp.int32, (dec_seq, dec_seq), 0)
              >= lax.broadcasted_iota(jnp.int32, (dec_seq, dec_seq), 1))
    ones_rhs = jnp.ones((dec_seq, LANES), BF16)
    vb = vb.astype(BF16)
    g_hi, g_lo = _split_bf16(g_all)
    q_stacks = [_head_stack(qt[rs], head_mask) for rs in rows_of]
    atts = [_dot_nt(q_stacks[bi], kt[rows_of[bi]].astype(BF16)) for bi in range(nb)]
    inters = [_dot(q_stacks[bi], s0_ref[bi].astype(BF16)) for bi in range(nb)]
    totals = [_dot_tn(g_hi[rs], ones_rhs) + _dot_tn(g_lo[rs], ones_rhs) for rs in rows_of]
    upd_fulls = [_dot_tn(ke[rs].astype(BF16), vb[rs]) for rs in rows_of]
    atts = [jnp.concatenate([jnp.where(causal, a[hh * dec_seq:(hh + 1) * dec_seq], 0.0)
                             for hh in range(B_HEADS)], axis=0).astype(BF16) for a in atts]
    intras = [_dot(atts[bi], vb[rows_of[bi]]) for bi in range(nb)]
    ob = jnp.concatenate(
        [jnp.concatenate([intras[bi][hh * dec_seq:(hh + 1) * dec_seq, hh * B_VAL_DIM:(hh + 1) * B_VAL_DIM]
                          + inters[bi][hh * dec_seq:(hh + 1) * dec_seq] for hh in range(B_HEADS)], axis=1)
         for bi in range(nb)], axis=0)
    for bi in range(nb):
        upd = jnp.concatenate(
            [upd_fulls[bi][hh * B_KEY_DIM:(hh + 1) * B_KEY_DIM, hh * B_VAL_DIM:(hh + 1) * B_VAL_DIM]
             for hh in range(B_HEADS)], axis=0)
        st_ref[bi] = s0_ref[bi] * jnp.exp(totals[bi]) + upd

    ho_ref[...] = _finish_layer(h, oa, ob, xn, p_ref[...], win_ref, glag_ref, wout_ref, peg_ref,
                                wpe_ref, wpg_ref)


def _rope_table(pos):
    half = ROT_DIM // 2
    inv = jnp.power(jnp.float32(ROPE_THETA), -jnp.arange(half, dtype=F32) * (2.0 / ROT_DIM))
    ang = inv[:, None] * pos.astype(F32)[None, :]
    cos, sin = jnp.cos(ang), jnp.sin(ang)
    head = jnp.concatenate([cos, cos, -sin, sin, jnp.zeros((A_HEAD_DIM - 2 * ROT_DIM, pos.shape[0]), F32)], axis=0)
    return jnp.concatenate([head, head], axis=0)


def _layer_spec(layer, shape):
    return pl.BlockSpec((None,) + shape, lambda i: (layer,) + (0,) * len(shape))


def _weight_specs(layer):
    return [
        _layer_spec(layer, (1, D_MODEL)),
        pl.BlockSpec((None, D_MODEL, C_AB), lambda i: (layer, 0, 0)),
        _layer_spec(layer, (D_MODEL, LANES)),
        _layer_spec(layer, (1, C_VA)),
        _layer_spec(layer, (LANES, B_QK_WIDTH)),
        _layer_spec(layer, (1, B_QK_WIDTH)),
        _layer_spec(layer, (1, B_VAL_DIM)),
        _layer_spec(layer, (A_WIDTH + B_WIDTH, D_MODEL)),
        _layer_spec(layer, (1, D_MODEL)),
        _layer_spec(layer, (D_PLE, D_MODEL)),
        _layer_spec(layer, (D_MODEL, D_MODEL)),
    ]


def _prompt_layer(layer, h, p, rope, sinks, weights, weights_t):
    seq = h.shape[0]
    tile = PROMPT_TILE
    assert seq % tile == 0 and tile % GLA_SPAN == 0 and GLA_SPAN % (2 * CHUNK) == 0 and tile >= WINDOW
    row = lambda w: pl.BlockSpec((tile, w), lambda i: (i, 0))
    const = lambda shape: pl.BlockSpec(shape, lambda i: (0,) * len(shape))
    return pl.pallas_call(
        functools.partial(_prompt_kernel, layer=layer),
        grid=(seq // tile,),
        in_specs=[pl.BlockSpec(memory_space=pltpu.SMEM), row(D_MODEL),
                  pl.BlockSpec((None, tile, D_PLE), lambda i: (layer, i, 0)),
                  pl.BlockSpec((LANES, tile), lambda i: (0, i))]
        + _weight_specs(layer)
        + [_layer_spec(layer, (B_QK_WIDTH, D_MODEL)),
           _layer_spec(layer, (GATE_RANK, D_MODEL)),
           _layer_spec(layer, (B_QK_WIDTH, GATE_RANK)),
           _layer_spec(layer, (B_QK_WIDTH, 1))],
        out_specs=[row(D_MODEL), const((WINDOW, A_KV_WIDTH)), const((WINDOW, A_KV_WIDTH)),
                   const((B_QK_WIDTH, B_VAL_DIM))],
        out_shape=[jax.ShapeDtypeStruct((seq, D_MODEL), F32),
                   jax.ShapeDtypeStruct((WINDOW, A_KV_WIDTH), F32),
                   jax.ShapeDtypeStruct((WINDOW, A_KV_WIDTH), F32),
                   jax.ShapeDtypeStruct((B_QK_WIDTH, B_VAL_DIM), F32)],
        scratch_shapes=[pltpu.VMEM((2, A_KV_HEADS, WINDOW, LANES), BF16),
                        pltpu.VMEM((2, A_KV_HEADS, WINDOW, 2 * LANES), BF16),
                        pltpu.VMEM((2, B_QK_WIDTH, B_VAL_DIM), F32)],
        compiler_params=pltpu.CompilerParams(dimension_semantics=("arbitrary",),
                                             vmem_limit_bytes=VMEM_LIMIT_BYTES),
        name="prompt_layer",
    )(sinks, h, p, rope, *weights, *weights_t)


def _sample_layer(layer, h, p, rope, ck, cv, s0, sinks, weights, dec_seq):
    n_seq = ck.shape[1]
    nb = SAMPLE_BATCH_TILE
    assert n_seq % nb == 0
    rows = nb * dec_seq
    row = lambda w: pl.BlockSpec((rows, w), lambda i: (i, 0))
    per_seq_in = lambda a, b: pl.BlockSpec((None, nb, a, b), lambda i: (layer, i, 0, 0))
    return pl.pallas_call(
        functools.partial(_sample_kernel, layer=layer, dec_seq=dec_seq),
        grid=(n_seq // nb,),
        in_specs=[pl.BlockSpec(memory_space=pltpu.SMEM), row(D_MODEL),
                  pl.BlockSpec((None, rows, D_PLE), lambda i: (layer, i, 0)),
                  pl.BlockSpec((LANES, rows), lambda i: (0, 0)),
                  per_seq_in(WINDOW, A_KV_WIDTH), per_seq_in(WINDOW, A_KV_WIDTH),
                  per_seq_in(B_QK_WIDTH, B_VAL_DIM)]
        + _weight_specs(layer),
        out_specs=[row(D_MODEL), row(A_KV_WIDTH), row(A_KV_WIDTH),
                   pl.BlockSpec((nb, B_QK_WIDTH, B_VAL_DIM), lambda i: (i, 0, 0))],
        out_shape=[jax.ShapeDtypeStruct((n_seq * dec_seq, D_MODEL), F32),
                   jax.ShapeDtypeStruct((n_seq * dec_seq, A_KV_WIDTH), F32),
                   jax.ShapeDtypeStruct((n_seq * dec_seq, A_KV_WIDTH), F32),
                   jax.ShapeDtypeStruct((n_seq, B_QK_WIDTH, B_VAL_DIM), F32)],
        compiler_params=pltpu.CompilerParams(dimension_semantics=("arbitrary",),
                                             vmem_limit_bytes=VMEM_LIMIT_BYTES),
        name="sample_layer",
    )(sinks, h, p, rope, ck, cv, s0, *weights)


def kernel(x_prompt, x_sample, cache_k, cache_v, state_gla, p_prompt, p_sample, norm_g, w_in, q_norm_g, k_norm_g,
           sinks, w_gate_up, b_gate, gla_norm_g, w_out, pe_norm_g, w_pe, w_pg):
    batch, seq, _ = x_prompt.shape
    n_seq, dec_seq, _ = x_sample.shape
    assert batch == 1 and cache_k.shape[2] == WINDOW

    rope_p = _rope_table(jnp.arange(seq, dtype=jnp.int32))
    rope_s = jnp.tile(_rope_table(PAST_LEN + jnp.arange(dec_seq, dtype=jnp.int32)), (1, SAMPLE_BATCH_TILE))

    w_ab = w_in[:, :, C_AB:IN_COLS]
    wab_b = jnp.pad(w_ab, ((0, 0), (0, 0), (0, LANES - GATE_RANK))).astype(BF16)
    wgu_b = jnp.pad(w_gate_up, ((0, 0), (0, LANES - GATE_RANK), (0, 0))).astype(BF16)
    gqk = jnp.concatenate([jnp.tile(q_norm_g, (1, A_HEADS)) * (A_HEAD_DIM ** -0.5),
                           jnp.tile(k_norm_g, (1, A_KV_HEADS))], axis=1)
    vec = lambda a: a[:, None, :]
    weights = (vec(norm_g), w_in.astype(BF16), wab_b, vec(gqk), wgu_b, vec(b_gate), vec(gla_norm_g), w_out.astype(BF16),
               vec(pe_norm_g), w_pe.astype(BF16), w_pg.astype(BF16))
    weights_t = (jnp.swapaxes(w_in[:, :, C_KB:C_VB], 1, 2).astype(BF16),
                 jnp.swapaxes(w_ab, 1, 2).astype(BF16),
                 jnp.swapaxes(w_gate_up, 1, 2).astype(BF16),
                 b_gate[:, :, None])

    hp = x_prompt.reshape(seq, D_MODEL)
    hs = x_sample.reshape(n_seq * dec_seq, D_MODEL)
    ck = cache_k.reshape(DEPTH, n_seq, WINDOW, A_KV_WIDTH)
    cv = cache_v.reshape(DEPTH, n_seq, WINDOW, A_KV_WIDTH)
    s0 = state_gla.reshape(DEPTH, n_seq, B_QK_WIDTH, B_VAL_DIM)
    pp = p_prompt.reshape(DEPTH, seq, D_PLE)
    ps = p_sample.reshape(DEPTH, n_seq * dec_seq, D_PLE)

    kp_l, vp_l, sp_l, ks_l, vs_l, ss_l = [], [], [], [], [], []
    for layer in range(DEPTH):
        hp, kn, vn, sn = _prompt_layer(layer, hp, pp, rope_p, sinks, weights, weights_t)
        kp_l.append(kn); vp_l.append(vn); sp_l.append(sn)
        hs, kn, vn, sn = _sample_layer(layer, hs, ps, rope_s, ck, cv, s0, sinks, weights, dec_seq)
        ks_l.append(kn); vs_l.append(vn); ss_l.append(sn)

    return (hp.reshape(batch, seq, D_MODEL),
            hs.reshape(n_seq, dec_seq, D_MODEL),
            jnp.stack(kp_l).reshape(DEPTH, batch, WINDOW, A_KV_HEADS, A_HEAD_DIM),
            jnp.stack(vp_l).reshape(DEPTH, batch, WINDOW, A_KV_HEADS, A_HEAD_DIM),
            jnp.stack(sp_l).reshape(DEPTH, batch, B_HEADS, B_KEY_DIM, B_VAL_DIM),
            jnp.stack(ks_l).reshape(DEPTH, n_seq, dec_seq, A_KV_HEADS, A_HEAD_DIM),
            jnp.stack(vs_l).reshape(DEPTH, n_seq, dec_seq, A_KV_HEADS, A_HEAD_DIM),
            jnp.stack(ss_l).reshape(DEPTH, n_seq, B_HEADS, B_KEY_DIM, B_VAL_DIM))
```

```python
import functools

import jax
import jax.numpy as jnp
from jax import lax
from jax.experimental import pallas as pl
from jax.experimental.pallas import tpu as pltpu

F32 = jnp.float32
BF16 = jnp.bfloat16

D_MODEL = 1024
DEPTH = 4
PAST_LEN = 1024
CHUNK = 64
D_PLE = 256
A_HEADS = 8
A_KV_HEADS = 2
A_HEAD_DIM = 64
A_GROUP = A_HEADS // A_KV_HEADS
A_WIDTH = A_HEADS * A_HEAD_DIM
A_KV_WIDTH = A_KV_HEADS * A_HEAD_DIM
WINDOW = 128
ROT_DIM = A_HEAD_DIM // 4
ROPE_THETA = 500000.0
B_HEADS = 4
B_KEY_DIM = 64
B_VAL_DIM = 128
B_QK_WIDTH = B_HEADS * B_KEY_DIM
B_WIDTH = B_HEADS * B_VAL_DIM
GATE_RANK = 16
GATE_TAU = 16.0
NEG_INF = -1e30
EPS = 1e-6

LANES = 128

C_QA = 0
C_KA = C_QA + A_WIDTH
C_VA = C_KA + A_KV_WIDTH
C_GA = C_VA + A_KV_WIDTH
C_QB = C_GA + A_WIDTH
C_KB = C_QB + B_QK_WIDTH
C_VB = C_KB + B_QK_WIDTH
C_GB = C_VB + B_WIDTH
C_AB = C_GB + B_WIDTH
IN_COLS = C_AB + GATE_RANK

PROMPT_TILE = 1024
GLA_SPAN = 256
FINISH_ROWS = 256
SAMPLE_BATCH_TILE = 16
VMEM_LIMIT_BYTES = 56 * 1024 * 1024


def _dot(a, b):
    return jnp.dot(a, b, preferred_element_type=F32)


def _dot_nt(a, b):
    return lax.dot_general(a, b, (((1,), (1,)), ((), ())), preferred_element_type=F32)


def _dot_tn(a, b):
    return lax.dot_general(a, b, (((0,), (0,)), ((), ())), preferred_element_type=F32)


def _split_bf16(x):
    hi = x.astype(BF16)
    lo = (x - hi.astype(F32)).astype(BF16)
    return hi, lo


def _rms(x, g):
    ms = jnp.mean(x * x, axis=-1, keepdims=True)
    return x * lax.rsqrt(ms + EPS) * g


def _sigmoid(x):
    return 0.5 + 0.5 * jnp.tanh(0.5 * x)


def _silu(x):
    return x * _sigmoid(x)


def _log_sigmoid(x):
    return jnp.minimum(x, 0.0) - jnp.log(1.0 + jnp.exp(-jnp.abs(x)))


def _block_mask(n, block, lower):
    r = lax.broadcasted_iota(jnp.int32, (n, n), 0)
    c = lax.broadcasted_iota(jnp.int32, (n, n), 1)
    m = (r // block) == (c // block)
    if lower:
        m = m & (r >= c)
    return jnp.where(m, 1.0, 0.0).astype(BF16)


def _project_a(xn, tab, win_ref, gqk_ref):
    return _norm_rope(_dot(xn, win_ref[:, C_QA:C_GA]), tab, gqk_ref)


def _norm_rope(qkv, tab, gqk_ref):
    qk, v = qkv[:, :C_VA], qkv[:, C_VA:]
    seg = _block_mask(256, A_HEAD_DIM, lower=False)
    hi, lo = _split_bf16(qk * qk)
    parts = []
    for c0 in range(0, C_VA, 256):
        c1 = min(c0 + 256, C_VA)
        sg = seg[: c1 - c0, : c1 - c0]
        parts.append(_dot(hi[:, c0:c1], sg) + _dot(lo[:, c0:c1], sg))
    ss = jnp.concatenate(parts, axis=1)
    qkn = qk * lax.rsqrt(ss * (1.0 / A_HEAD_DIM) + EPS) * gqk_ref[...]
    head_lane = lax.broadcasted_iota(jnp.int32, (1, LANES), 1) % A_HEAD_DIM
    cos = jnp.where(head_lane < ROT_DIM, tab, 1.0)
    sin = jnp.where(head_lane < ROT_DIM, pltpu.roll(tab, LANES - ROT_DIM, 1), 0.0)
    first_half = head_lane < ROT_DIM // 2
    slabs = []
    for s in range(C_VA // LANES):
        x = qkn[:, s * LANES:(s + 1) * LANES]
        up = pltpu.roll(x, LANES - ROT_DIM // 2, 1)
        dn = pltpu.roll(x, ROT_DIM // 2, 1)
        slabs.append(x * cos + jnp.where(first_half, up, dn) * sin)
    q = jnp.concatenate(slabs[:A_WIDTH // LANES], axis=1)
    k = slabs[A_WIDTH // LANES]
    return q, k, v


def _kv_operands(k, v, lo_half):
    ksw = pltpu.roll(k, A_HEAD_DIM, 1)
    vsw = pltpu.roll(v, A_HEAD_DIM, 1)
    one = jnp.ones_like(v)
    kd = (jnp.where(lo_half, k, ksw).astype(BF16), jnp.where(lo_half, ksw, k).astype(BF16))
    vx = (jnp.concatenate([jnp.where(lo_half, v, one), jnp.where(lo_half, one, vsw)], axis=1).astype(BF16),
          jnp.concatenate([jnp.where(lo_half, vsw, one), jnp.where(lo_half, one, v)], axis=1).astype(BF16))
    return kd, vx


def _attn_scores(q_blk, j, kd, lo_half):
    a = q_blk[:, (2 * j) * LANES:(2 * j + 1) * LANES]
    b = q_blk[:, (2 * j + 1) * LANES:(2 * j + 2) * LANES]
    z = jnp.zeros_like(a)
    stack = jnp.concatenate([jnp.where(lo_half, a, z), jnp.where(lo_half, z, a),
                             jnp.where(lo_half, b, z), jnp.where(lo_half, z, b)], axis=0).astype(BF16)
    return _dot_nt(stack, kd)


def _attn_probs(s, j, bias, sink_ref, layer):
    rows = s.shape[0] // A_GROUP
    if bias is not None:
        s = s + bias
    sk = jnp.concatenate([jnp.full((rows, 1), sink_ref[layer, A_GROUP * j + g], F32) for g in range(A_GROUP)],
                         axis=0)
    m = jnp.maximum(jnp.max(s, axis=-1, keepdims=True), sk)
    return jnp.exp(s - m).astype(BF16), jnp.exp(sk - m)


def _attn_values(p, de, vx, lo_half):
    rows = p.shape[0] // A_GROUP
    r = _dot(p, vx)
    r_lo, r_hi = r[:, :LANES], r[:, LANES:]

    def slab(g0):
        e = slice(g0 * rows, (g0 + 1) * rows)
        o = slice((g0 + 1) * rows, (g0 + 2) * rows)
        return jnp.where(lo_half, r_lo[e] / (r_hi[e] + de[e]), r_hi[o] / (r_lo[o] + de[o]))

    return slab(0), slab(2)


def _attn_block(q_blk, j, kd, vx, bias, sink_ref, layer, lo_half):
    p, de = _attn_probs(_attn_scores(q_blk, j, kd, lo_half), j, bias, sink_ref, layer)
    return _attn_values(p, de, vx, lo_half)


def _gla_inputs(xn, win_ref, wab_ref, wgu_ref, bg_ref, block):
    rows = xn.shape[0]
    qb = _dot(xn, win_ref[:, C_QB:C_KB])
    kb = _dot(xn, win_ref[:, C_KB:C_VB])
    vb = _dot(xn, win_ref[:, C_VB:C_GB])
    ab = _dot(xn, wab_ref[...])
    logg = _log_sigmoid(_dot(ab.astype(BF16), wgu_ref[...]) + bg_ref[...]) * (1.0 / GATE_TAU)
    g_hi, g_lo = _split_bf16(logg)
    tri = _block_mask(rows, block, lower=True)
    ones = _block_mask(rows, block, lower=False)
    b = _dot(tri, g_hi) + _dot(tri, g_lo)
    b_last = _dot(ones, g_hi) + _dot(ones, g_lo)
    qt = qb * (jnp.exp(b) * (B_KEY_DIM ** -0.5))
    kt = kb * jnp.exp(-b)
    ke = kb * jnp.exp(b_last - b)
    return qt, kt, ke, vb, logg


def _head_stack(x, head_mask):
    z = jnp.zeros_like(x)
    return jnp.concatenate([jnp.where(head_mask[h], x, z) for h in range(B_HEADS)], axis=0).astype(BF16)


def _finish_layer(h, oa, ob, xn, p, win_ref, glag_ref, wout_ref, peg_ref, wpe_ref, wpg_ref):
    ga = _dot(xn, win_ref[:, C_GA:C_QB])
    gb = _dot(xn, win_ref[:, C_GB:C_AB])
    obn = jnp.concatenate(
        [_rms(ob[:, hh * B_VAL_DIM:(hh + 1) * B_VAL_DIM], glag_ref[...]) for hh in range(B_HEADS)], axis=1)
    mix = jnp.concatenate([oa * _silu(ga), obn * _silu(gb)], axis=1).astype(BF16)
    h1 = h + _dot(mix, wout_ref[...])
    gate = _sigmoid(_dot(_rms(h1, peg_ref[...]).astype(BF16), wpg_ref[...]))
    return h1 + gate * _dot(p.astype(BF16), wpe_ref[...])


def _gla_decays(kbt, abt, wgut_ref, bgc_ref):
    tile = kbt.shape[1]
    loggt = _log_sigmoid(_dot(wgut_ref[...], abt) + bgc_ref[...]) * (1.0 / GATE_TAU)
    g_hi, g_lo = _split_bf16(loggt)
    r = lax.broadcasted_iota(jnp.int32, (GLA_SPAN, GLA_SPAN), 0)
    c = lax.broadcasted_iota(jnp.int32, (GLA_SPAN, GLA_SPAN), 1)
    upper = jnp.where(((r // CHUNK) == (c // CHUNK)) & (r <= c), 1.0, 0.0).astype(BF16)
    bt = jnp.concatenate(
        [_dot(g_hi[:, s0:s0 + GLA_SPAN], upper) + _dot(g_lo[:, s0:s0 + GLA_SPAN], upper)
         for s0 in range(0, tile, GLA_SPAN)], axis=1)
    ktt = (kbt * jnp.exp(-bt)).astype(BF16)
    pair = 2 * CHUNK
    cols = [bt[:, (c + 1) * CHUNK - 1:(c + 1) * CHUNK] for c in range(tile // CHUNK)]
    lane = lax.broadcasted_iota(jnp.int32, (1, pair), 1)
    ket = jnp.concatenate(
        [kbt[:, pc * pair:(pc + 1) * pair]
         * jnp.exp(jnp.where(lane < CHUNK, cols[2 * pc], cols[2 * pc + 1]) - bt[:, pc * pair:(pc + 1) * pair])
         for pc in range(tile // pair)], axis=1)
    return bt, ktt, ket, cols


def _gla_chunk_jobs(qt, vb, ktt, ket, cols, stk, out):
    tile = qt.shape[0]
    n_chunks = tile // CHUNK
    pair = 2 * CHUNK
    head_of_lane = lax.broadcasted_iota(jnp.int32, (1, B_QK_WIDTH), 1) // B_KEY_DIM
    head_mask = [head_of_lane == hh for hh in range(B_HEADS)]
    lane = lax.broadcasted_iota(jnp.int32, (1, pair), 1)
    t_row = lax.broadcasted_iota(jnp.int32, (CHUNK, pair), 0)
    s_col = lax.broadcasted_iota(jnp.int32, (CHUNK, pair), 1)
    own = [lane < CHUNK, lane >= CHUNK]
    causal = [s_col <= t_row, (s_col >= CHUNK) & (s_col - CHUNK <= t_row)]
    v_heads = [[vb[(c // 2) * pair:(c // 2 + 1) * pair, hh * B_VAL_DIM:(hh + 1) * B_VAL_DIM]
                for hh in range(B_HEADS)] for c in range(n_chunks)]
    updates, states, results, ob_rows = {}, {}, {}, {}

    def update(c):
        ps = slice((c // 2) * pair, (c // 2 + 1) * pair)
        ket_c = jnp.where(own[c % 2], ket[:, ps], 0.0).astype(BF16)
        updates[c] = jnp.concatenate(
            [_dot(ket_c[hh * B_KEY_DIM:(hh + 1) * B_KEY_DIM], v_heads[c][hh]) for hh in range(B_HEADS)], axis=0)

    def chain():
        s = stk
        for c in range(n_chunks):
            states[c] = s.astype(BF16)
            s = s * jnp.exp(cols[c]) + updates[c]
        out["state"] = s

    def result(c):
        ps = slice((c // 2) * pair, (c // 2 + 1) * pair)
        q_stack = _head_stack(qt[c * CHUNK:(c + 1) * CHUNK], head_mask)
        results[c] = _dot(q_stack, jnp.concatenate([states[c], ktt[:, ps]], axis=1))

    def output(c):
        outs = []
        for hh in range(B_HEADS):
            hr = slice(hh * CHUNK, (hh + 1) * CHUNK)
            a_h = jnp.where(causal[c % 2], results[c][hr, B_VAL_DIM:], 0.0).astype(BF16)
            outs.append(_dot(a_h, v_heads[c][hh]) + results[c][hr, :B_VAL_DIM])
        ob_rows[c] = jnp.concatenate(outs, axis=1)
        if c == n_chunks - 1:
            out["ob"] = jnp.concatenate([ob_rows[cc] for cc in range(n_chunks)], axis=0)

    bind = lambda f, c: (lambda: f(c))
    return ([bind(update, c) for c in range(n_chunks)] + [chain]
            + [bind(result, c) for c in range(n_chunks)] + [bind(output, c) for c in range(n_chunks)])


def _trace_interleaved(jobs_a, jobs_b):
    na, nb = len(jobs_a), len(jobs_b)
    ia = ib = 0
    while ia < na or ib < nb:
        if ib >= nb or (ia < na and ia * nb <= ib * na):
            jobs_a[ia]()
            ia += 1
        else:
            jobs_b[ib]()
            ib += 1


def _prompt_kernel(sink_ref, h_ref, p_ref, rope_ref, ng_ref, win_ref, wab_ref, gqk_ref, wgu_ref, bg_ref, glag_ref,
                   wout_ref, peg_ref, wpe_ref, wpg_ref, wkbt_ref, wabt_ref, wgut_ref, bgc_ref,
                   ho_ref, nk_ref, nv_ref, st_ref,
                   kd_scr, vx_scr, st_scr, *, layer):
    tile = PROMPT_TILE
    n_chunks = tile // CHUNK
    band = WINDOW + CHUNK
    i = pl.program_id(0)
    rd = (i + 1) % 2
    wr = i % 2

    @pl.when(i == 0)
    def _():
        kd_scr[rd] = jnp.zeros(kd_scr.shape[1:], BF16)
        vx_scr[rd] = jnp.zeros(vx_scr.shape[1:], BF16)
        st_scr[rd] = jnp.zeros(st_scr.shape[1:], F32)

    h = h_ref[...]
    xn = _rms(h, ng_ref[...]).astype(BF16)
    lo_half = lax.broadcasted_iota(jnp.int32, (1, LANES), 1) < A_HEAD_DIM
    key_blk = lax.broadcasted_iota(jnp.int32, (1, band), 1) // CHUNK
    blocks = [(c, j) for c in range(n_chunks) for j in range(A_KV_HEADS)]
    n_blocks = len(blocks)
    v, scores, probs, slabs, gla = {}, {}, {}, {}, {}
    bind = lambda f, n: (lambda: f(n))

    qkv = _dot(xn, win_ref[:, C_QA:C_GA])

    def rope_tab():
        tab = rope_ref[...].T
        head_lane = lax.broadcasted_iota(jnp.int32, (1, LANES), 1) % A_HEAD_DIM
        v["cos"] = jnp.where(head_lane < ROT_DIM, tab, 1.0)
        v["sin"] = jnp.where(head_lane < ROT_DIM, pltpu.roll(tab, LANES - ROT_DIM, 1), 0.0)
        v["first_half"] = head_lane < ROT_DIM // 2

    def norm(g):
        width = min(256, C_VA - 256 * g)
        x = qkv[:, 256 * g:256 * g + width]
        seg = _block_mask(width, A_HEAD_DIM, lower=False)
        hi, lo = _split_bf16(x * x)
        ss = _dot(hi, seg) + _dot(lo, seg)
        xs = x * lax.rsqrt(ss * (1.0 / A_HEAD_DIM) + EPS) * gqk_ref[:, 256 * g:256 * g + width]
        for s in range(width // LANES):
            x1 = xs[:, s * LANES:(s + 1) * LANES]
            up = pltpu.roll(x1, LANES - ROT_DIM // 2, 1)
            dn = pltpu.roll(x1, ROT_DIM // 2, 1)
            v[f"slab{2 * g + s}"] = x1 * v["cos"] + jnp.where(v["first_half"], up, dn) * v["sin"]

    def kv_ops():
        k, vv = v["slab4"], qkv[:, C_VA:]
        nk_ref[...] = k[tile - WINDOW:, :]
        nv_ref[...] = vv[tile - WINDOW:, :]
        kd, vx = _kv_operands(k, vv, lo_half)
        v["kd_ext"] = [jnp.concatenate([kd_scr[rd, j], kd[j]], axis=0) for j in range(A_KV_HEADS)]
        v["vx_ext"] = [jnp.concatenate([vx_scr[rd, j], vx[j]], axis=0) for j in range(A_KV_HEADS)]
        for j in range(A_KV_HEADS):
            kd_scr[wr, j] = kd[j][tile - WINDOW:, :]
            vx_scr[wr, j] = vx[j][tile - WINDOW:, :]

    def proj_k():
        v["kbt"] = _dot_nt(wkbt_ref[...], xn)
        v["abt"] = _dot_nt(wabt_ref[...], xn).astype(BF16)

    def proj_qv():
        v["qb"] = _dot(xn, win_ref[:, C_QB:C_KB])
        v["vb"] = _dot(xn, win_ref[:, C_VB:C_GB]).astype(BF16)

    _trace_interleaved([rope_tab, bind(norm, 2), kv_ops, bind(norm, 0), bind(norm, 1)], [proj_k, proj_qv])

    def score(n):
        c, j = blocks[n]
        cs = slice(c * CHUNK, (c + 1) * CHUNK)
        q_blk = jnp.concatenate([v[f"slab{2 * j}"][cs], v[f"slab{2 * j + 1}"][cs]], axis=1)
        scores[n] = _attn_scores(q_blk, 0, v["kd_ext"][j][c * CHUNK:c * CHUNK + band, :], lo_half)

    def decays():
        gla["bt"], gla["ktt"], gla["ket"], gla["cols"] = _gla_decays(v["kbt"], v["abt"], wgut_ref, bgc_ref)

    def scale_q():
        gla["qt"] = v["qb"] * (jnp.exp(gla["bt"].T) * (B_KEY_DIM ** -0.5))

    _trace_interleaved([bind(score, n) for n in range(n_blocks)], [decays, scale_q])

    def prob(n):
        c, j = blocks[n]
        first_key_chunk = i * n_chunks + c - WINDOW // CHUNK
        bias = jnp.where(key_blk + first_key_chunk >= 0, 0.0, NEG_INF)
        probs[n] = _attn_probs(scores.pop(n), j, bias, sink_ref, layer)

    def value(n):
        c, j = blocks[n]
        slabs[n] = _attn_values(*probs.pop(n), v["vx_ext"][j][c * CHUNK:c * CHUNK + band, :], lo_half)

    def proj_ga():
        v["ga"] = _dot(xn, win_ref[:, C_GA:C_QB])

    def proj_gb():
        v["gb"] = _dot(xn, win_ref[:, C_GB:C_AB])

    def embed():
        v["pe"] = _dot(p_ref[...].astype(BF16), wpe_ref[...])

    gla_jobs = _gla_chunk_jobs(gla["qt"], v["vb"], gla["ktt"], gla["ket"], gla["cols"], st_scr[rd], gla)
    n_first = n_chunks + 1
    _trace_interleaved([bind(prob, n) for n in range(n_blocks)], [proj_ga, proj_gb, embed] + gla_jobs[:n_first])
    _trace_interleaved([bind(value, n) for n in range(n_blocks)], gla_jobs[n_first:])
    st_scr[wr] = gla["state"]
    st_ref[...] = gla["state"]
    ob, pe, ga, gb = gla["ob"], v["pe"], v["ga"], v["gb"]

    for r0 in range(0, tile, FINISH_ROWS):
        rs = slice(r0, r0 + FINISH_ROWS)
        oa = jnp.concatenate(
            [jnp.concatenate([sl for j in range(A_KV_HEADS) for sl in slabs[c * A_KV_HEADS + j]], axis=1)
             for c in range(r0 // CHUNK, (r0 + FINISH_ROWS) // CHUNK)], axis=0)
        h1 = h[rs] + _dot((oa * _silu(ga[rs])).astype(BF16), wout_ref[0:A_WIDTH, :])
        obn = jnp.concatenate(
            [_rms(ob[rs, hh * B_VAL_DIM:(hh + 1) * B_VAL_DIM], glag_ref[...]) for hh in range(B_HEADS)], axis=1)
        h2 = h1 + _dot((obn * _silu(gb[rs])).astype(BF16), wout_ref[A_WIDTH:, :])
        gate = _sigmoid(_dot(_rms(h2, peg_ref[...]).astype(BF16), wpg_ref[...]))
        ho_ref[rs, :] = h2 + gate * pe[rs]


def _sample_kernel(sink_ref, h_ref, p_ref, rope_ref, ck_ref, cv_ref, s0_ref, ng_ref, win_ref, wab_ref, gqk_ref, wgu_ref,
                   bg_ref, glag_ref, wout_ref, peg_ref, wpe_ref, wpg_ref,
                   ho_ref, nk_ref, nv_ref, st_ref, *, layer, dec_seq):
    nb = SAMPLE_BATCH_TILE
    h = h_ref[...]
    xn = _rms(h, ng_ref[...]).astype(BF16)
    lo_half = lax.broadcasted_iota(jnp.int32, (1, LANES), 1) < A_HEAD_DIM

    q, k, v = _project_a(xn, rope_ref[...].T, win_ref, gqk_ref)
    nk_ref[...] = k
    nv_ref[...] = v
    kd_new, vx_new = _kv_operands(k, v, lo_half)
    rows_of = [slice(bi * dec_seq, (bi + 1) * dec_seq) for bi in range(nb)]
    blocks = [(bi, j) for bi in range(nb) for j in range(A_KV_HEADS)]
    old = [_kv_operands(ck_ref[bi], cv_ref[bi], lo_half) for bi in range(nb)]
    scores = [_attn_scores(q[rows_of[bi]], j,
                           jnp.concatenate([old[bi][0][j], kd_new[j][rows_of[bi]]], axis=0), lo_half)
              for bi, j in blocks]
    probs = [_attn_probs(s, j, None, sink_ref, layer) for (bi, j), s in zip(blocks, scores)]
    slabs = [_attn_values(p, de, jnp.concatenate([old[bi][1][j], vx_new[j][rows_of[bi]]], axis=0), lo_half)
             for (bi, j), (p, de) in zip(blocks, probs)]
    oa = jnp.concatenate(
        [jnp.concatenate([sl for j in range(A_KV_HEADS) for sl in slabs[bi * A_KV_HEADS + j]], axis=1)
         for bi in range(nb)], axis=0)

    qt, kt, ke, vb, g_all = _gla_inputs(xn, win_ref, wab_ref, wgu_ref, bg_ref, dec_seq)
    head_of_lane = lax.broadcasted_iota(jnp.int32, (1, B_QK_WIDTH), 1) // B_KEY_DIM
    head_mask = [head_of_lane == hh for hh in range(B_HEADS)]
    causal = (lax.broadcasted_iota(jnp.int32, (dec_seq, dec_seq), 0)
              >= lax.broadcasted_iota(jnp.int32, (dec_seq, dec_seq), 1))
    ones_rhs = jnp.ones((dec_seq, LANES), BF16)
    vb = vb.astype(BF16)
    g_hi, g_lo = _split_bf16(g_all)
    q_stacks = [_head_stack(qt[rs], head_mask) for rs in rows_of]
    atts = [_dot_nt(q_stacks[bi], kt[rows_of[bi]].astype(BF16)) for bi in range(nb)]
    inters = [_dot(q_stacks[bi], s0_ref[bi].astype(BF16)) for bi in range(nb)]
    totals = [_dot_tn(g_hi[rs], ones_rhs) + _dot_tn(g_lo[rs], ones_rhs) for rs in rows_of]
    upd_fulls = [_dot_tn(ke[rs].astype(BF16), vb[rs]) for rs in rows_of]
    atts = [jnp.concatenate([jnp.where(causal, a[hh * dec_seq:(hh + 1) * dec_seq], 0.0)
                             for hh in range(B_HEADS)], axis=0).astype(BF16) for a in atts]
    intras = [_dot(atts[bi], vb[rows_of[bi]]) for bi in range(nb)]
    ob = jnp.concatenate(
        [jnp.concatenate([intras[bi][hh * dec_seq:(hh + 1) * dec_seq, hh * B_VAL_DIM:(hh + 1) * B_VAL_DIM]
                          + inters[bi][hh * dec_seq:(hh + 1) * dec_seq] for hh in range(B_HEADS)], axis=1)
         for bi in range(nb)], axis=0)
    for bi in range(nb):
        upd = jnp.concatenate(
            [upd_fulls[bi][hh * B_KEY_DIM:(hh + 1) * B_KEY_DIM, hh * B_VAL_DIM:(hh + 1) * B_VAL_DIM]
             for hh in range(B_HEADS)], axis=0)
        st_ref[bi] = s0_ref[bi] * jnp.exp(totals[bi]) + upd

    ho_ref[...] = _finish_layer(h, oa, ob, xn, p_ref[...], win_ref, glag_ref, wout_ref, peg_ref,
                                wpe_ref, wpg_ref)


def _rope_table(pos):
    half = ROT_DIM // 2
    inv = jnp.power(jnp.float32(ROPE_THETA), -jnp.arange(half, dtype=F32) * (2.0 / ROT_DIM))
    ang = inv[:, None] * pos.astype(F32)[None, :]
    cos, sin = jnp.cos(ang), jnp.sin(ang)
    head = jnp.concatenate([cos, cos, -sin, sin, jnp.zeros((A_HEAD_DIM - 2 * ROT_DIM, pos.shape[0]), F32)], axis=0)
    return jnp.concatenate([head, head], axis=0)


def _layer_spec(layer, shape):
    return pl.BlockSpec((None,) + shape, lambda i: (layer,) + (0,) * len(shape), pipeline_mode=pl.Buffered(1))


def _weight_specs(layer):
    return [
        _layer_spec(layer, (1, D_MODEL)),
        pl.BlockSpec((None, D_MODEL, C_AB), lambda i: (layer, 0, 0),
                     pipeline_mode=pl.Buffered(1)),
        _layer_spec(layer, (D_MODEL, LANES)),
        _layer_spec(layer, (1, C_VA)),
        _layer_spec(layer, (LANES, B_QK_WIDTH)),
        _layer_spec(layer, (1, B_QK_WIDTH)),
        _layer_spec(layer, (1, B_VAL_DIM)),
        _layer_spec(layer, (A_WIDTH + B_WIDTH, D_MODEL)),
        _layer_spec(layer, (1, D_MODEL)),
        _layer_spec(layer, (D_PLE, D_MODEL)),
        _layer_spec(layer, (D_MODEL, D_MODEL)),
    ]


def _prompt_layer(layer, h, p, rope, sinks, weights, weights_t):
    seq = h.shape[0]
    tile = PROMPT_TILE
    assert seq % tile == 0 and tile % GLA_SPAN == 0 and GLA_SPAN % (2 * CHUNK) == 0 and tile >= WINDOW
    row = lambda w: pl.BlockSpec((tile, w), lambda i: (i, 0))
    const = lambda shape: pl.BlockSpec(shape, lambda i: (0,) * len(shape))
    return pl.pallas_call(
        functools.partial(_prompt_kernel, layer=layer),
        grid=(seq // tile,),
        in_specs=[pl.BlockSpec(memory_space=pltpu.SMEM), row(D_MODEL),
                  pl.BlockSpec((None, tile, D_PLE), lambda i: (layer, i, 0)),
                  pl.BlockSpec((LANES, tile), lambda i: (0, i))]
        + _weight_specs(layer)
        + [_layer_spec(layer, (B_QK_WIDTH, D_MODEL)),
           _layer_spec(layer, (GATE_RANK, D_MODEL)),
           _layer_spec(layer, (B_QK_WIDTH, GATE_RANK)),
           _layer_spec(layer, (B_QK_WIDTH, 1))],
        out_specs=[row(D_MODEL), const((WINDOW, A_KV_WIDTH)), const((WINDOW, A_KV_WIDTH)),
                   const((B_QK_WIDTH, B_VAL_DIM))],
        out_shape=[jax.ShapeDtypeStruct((seq, D_MODEL), F32),
                   jax.ShapeDtypeStruct((WINDOW, A_KV_WIDTH), F32),
                   jax.ShapeDtypeStruct((WINDOW, A_KV_WIDTH), F32),
                   jax.ShapeDtypeStruct((B_QK_WIDTH, B_VAL_DIM), F32)],
        scratch_shapes=[pltpu.VMEM((2, A_KV_HEADS, WINDOW, LANES), BF16),
                        pltpu.VMEM((2, A_KV_HEADS, WINDOW, 2 * LANES), BF16),
                        pltpu.VMEM((2, B_QK_WIDTH, B_VAL_DIM), F32)],
        compiler_params=pltpu.CompilerParams(dimension_semantics=("arbitrary",),
                                             vmem_limit_bytes=VMEM_LIMIT_BYTES),
        name="prompt_layer",
    )(sinks, h, p, rope, *weights, *weights_t)


def _sample_layer(layer, h, p, rope, ck, cv, s0, sinks, weights, dec_seq):
    n_seq = ck.shape[1]
    nb = SAMPLE_BATCH_TILE
    assert n_seq % nb == 0
    rows = nb * dec_seq
    row = lambda w: pl.BlockSpec((rows, w), lambda i: (i, 0))
    per_seq_in = lambda a, b: pl.BlockSpec((None, nb, a, b), lambda i: (layer, i, 0, 0))
    return pl.pallas_call(
        functools.partial(_sample_kernel, layer=layer, dec_seq=dec_seq),
        grid=(n_seq // nb,),
        in_specs=[pl.BlockSpec(memory_space=pltpu.SMEM), row(D_MODEL),
                  pl.BlockSpec((None, rows, D_PLE), lambda i: (layer, i, 0)),
                  pl.BlockSpec((LANES, rows), lambda i: (0, 0)),
                  per_seq_in(WINDOW, A_KV_WIDTH), per_seq_in(WINDOW, A_KV_WIDTH),
                  per_seq_in(B_QK_WIDTH, B_VAL_DIM)]
        + _weight_specs(layer),
        out_specs=[row(D_MODEL), row(A_KV_WIDTH), row(A_KV_WIDTH),
                   pl.BlockSpec((nb, B_QK_WIDTH, B_VAL_DIM), lambda i: (i, 0, 0))],
        out_shape=[jax.ShapeDtypeStruct((n_seq * dec_seq, D_MODEL), F32),
                   jax.ShapeDtypeStruct((n_seq * dec_seq, A_KV_WIDTH), F32),
                   jax.ShapeDtypeStruct((n_seq * dec_seq, A_KV_WIDTH), F32),
                   jax.ShapeDtypeStruct((n_seq, B_QK_WIDTH, B_VAL_DIM), F32)],
        compiler_params=pltpu.CompilerParams(dimension_semantics=("arbitrary",),
                                             vmem_limit_bytes=VMEM_LIMIT_BYTES),
        name="sample_layer",
    )(sinks, h, p, rope, ck, cv, s0, *weights)


def kernel(x_prompt, x_sample, cache_k, cache_v, state_gla, p_prompt, p_sample, norm_g, w_in, q_norm_g, k_norm_g,
           sinks, w_gate_up, b_gate, gla_norm_g, w_out, pe_norm_g, w_pe, w_pg):
    batch, seq, _ = x_prompt.shape
    n_seq, dec_seq, _ = x_sample.shape
    assert batch == 1 and cache_k.shape[2] == WINDOW

    rope_p = _rope_table(jnp.arange(seq, dtype=jnp.int32))
    rope_s = jnp.tile(_rope_table(PAST_LEN + jnp.arange(dec_seq, dtype=jnp.int32)), (1, SAMPLE_BATCH_TILE))

    w_ab = w_in[:, :, C_AB:IN_COLS]
    wab_b = jnp.pad(w_ab, ((0, 0), (0, 0), (0, LANES - GATE_RANK))).astype(BF16)
    wgu_b = jnp.pad(w_gate_up, ((0, 0), (0, LANES - GATE_RANK), (0, 0))).astype(BF16)
    gqk = jnp.concatenate([jnp.tile(q_norm_g, (1, A_HEADS)) * (A_HEAD_DIM ** -0.5),
                           jnp.tile(k_norm_g, (1, A_KV_HEADS))], axis=1)
    vec = lambda a: a[:, None, :]
    weights = (vec(norm_g), w_in.astype(BF16), wab_b, vec(gqk), wgu_b, vec(b_gate), vec(gla_norm_g), w_out.astype(BF16),
               vec(pe_norm_g), w_pe.astype(BF16), w_pg.astype(BF16))
    weights_t = (jnp.swapaxes(w_in[:, :, C_KB:C_VB], 1, 2).astype(BF16),
                 jnp.swapaxes(w_ab, 1, 2).astype(BF16),
                 jnp.swapaxes(w_gate_up, 1, 2).astype(BF16),
                 b_gate[:, :, None])

    hp = x_prompt.reshape(seq, D_MODEL)
    hs = x_sample.reshape(n_seq * dec_seq, D_MODEL)
    ck = cache_k.reshape(DEPTH, n_seq, WINDOW, A_KV_WIDTH)
    cv = cache_v.reshape(DEPTH, n_seq, WINDOW, A_KV_WIDTH)
    s0 = state_gla.reshape(DEPTH, n_seq, B_QK_WIDTH, B_VAL_DIM)
    pp = p_prompt.reshape(DEPTH, seq, D_PLE)
    ps = p_sample.reshape(DEPTH, n_seq * dec_seq, D_PLE)

    kp_l, vp_l, sp_l, ks_l, vs_l, ss_l = [], [], [], [], [], []
    for layer in range(DEPTH):
        hp, kn, vn, sn = _prompt_layer(layer, hp, pp, rope_p, sinks, weights, weights_t)
        kp_l.append(kn); vp_l.append(vn); sp_l.append(sn)
        hs, kn, vn, sn = _sample_layer(layer, hs, ps, rope_s, ck, cv, s0, sinks, weights, dec_seq)
        ks_l.append(kn); vs_l.append(vn); ss_l.append(sn)

    return (hp.reshape(batch, seq, D_MODEL),
            hs.reshape(n_seq, dec_seq, D_MODEL),
            jnp.stack(kp_l).reshape(DEPTH, batch, WINDOW, A_KV_HEADS, A_HEAD_DIM),
            jnp.stack(vp_l).reshape(DEPTH, batch, WINDOW, A_KV_HEADS, A_HEAD_DIM),
            jnp.stack(sp_l).reshape(DEPTH, batch, B_HEADS, B_KEY_DIM, B_VAL_DIM),
            jnp.stack(ks_l).reshape(DEPTH, n_seq, dec_seq, A_KV_HEADS, A_HEAD_DIM),
            jnp.stack(vs_l).reshape(DEPTH, n_seq, dec_seq, A_KV_HEADS, A_HEAD_DIM),
            jnp.stack(ss_l).reshape(DEPTH, n_seq, B_HEADS, B_KEY_DIM, B_VAL_DIM))
```

```python
import functools

import jax
import jax.numpy as jnp
from jax import lax
from jax.experimental import pallas as pl
from jax.experimental.pallas import tpu as pltpu

F32 = jnp.float32
BF16 = jnp.bfloat16

D_MODEL = 1024
DEPTH = 4
PAST_LEN = 1024
CHUNK = 64
D_PLE = 256
A_HEADS = 8
A_KV_HEADS = 2
A_HEAD_DIM = 64
A_GROUP = A_HEADS // A_KV_HEADS
A_WIDTH = A_HEADS * A_HEAD_DIM
A_KV_WIDTH = A_KV_HEADS * A_HEAD_DIM
WINDOW = 128
ROT_DIM = A_HEAD_DIM // 4
ROPE_THETA = 500000.0
B_HEADS = 4
B_KEY_DIM = 64
B_VAL_DIM = 128
B_QK_WIDTH = B_HEADS * B_KEY_DIM
B_WIDTH = B_HEADS * B_VAL_DIM
GATE_RANK = 16
GATE_TAU = 16.0
NEG_INF = -1e30
EPS = 1e-6

LANES = 128

C_QA = 0
C_KA = C_QA + A_WIDTH
C_VA = C_KA + A_KV_WIDTH
C_GA = C_VA + A_KV_WIDTH
C_QB = C_GA + A_WIDTH
C_KB = C_QB + B_QK_WIDTH
C_VB = C_KB + B_QK_WIDTH
C_GB = C_VB + B_WIDTH
C_AB = C_GB + B_WIDTH
IN_COLS = C_AB + GATE_RANK

PROMPT_TILE = 512
GLA_SPAN = 256
GLA_SUB = 16
GLA_FAST_MAX_DECAY = 60.0
FINISH_ROWS = 256
SAMPLE_BATCH_TILE = 16
VMEM_LIMIT_BYTES = 56 * 1024 * 1024


def _dot(a, b):
    return jnp.dot(a, b, preferred_element_type=F32)


def _dot_nt(a, b):
    return lax.dot_general(a, b, (((1,), (1,)), ((), ())), preferred_element_type=F32)


def _dot_tn(a, b):
    return lax.dot_general(a, b, (((0,), (0,)), ((), ())), preferred_element_type=F32)


def _split_bf16(x):
    hi = x.astype(BF16)
    lo = (x - hi.astype(F32)).astype(BF16)
    return hi, lo


def _rms(x, g):
    ms = jnp.mean(x * x, axis=-1, keepdims=True)
    return x * lax.rsqrt(ms + EPS) * g


def _sigmoid(x):
    return 0.5 + 0.5 * jnp.tanh(0.5 * x)


def _silu(x):
    return x * _sigmoid(x)


def _log_sigmoid(x):
    return jnp.minimum(x, 0.0) - jnp.log(1.0 + jnp.exp(-jnp.abs(x)))


def _block_mask(n, block, lower):
    r = lax.broadcasted_iota(jnp.int32, (n, n), 0)
    c = lax.broadcasted_iota(jnp.int32, (n, n), 1)
    m = (r // block) == (c // block)
    if lower:
        m = m & (r >= c)
    return jnp.where(m, 1.0, 0.0).astype(BF16)


def _project_a(xn, tab, win_ref, gqk_ref):
    return _norm_rope(_dot(xn, win_ref[:, C_QA:C_GA]), tab, gqk_ref)


def _norm_rope(qkv, tab, gqk_ref):
    qk, v = qkv[:, :C_VA], qkv[:, C_VA:]
    seg = _block_mask(256, A_HEAD_DIM, lower=False)
    hi, lo = _split_bf16(qk * qk)
    parts = []
    for c0 in range(0, C_VA, 256):
        c1 = min(c0 + 256, C_VA)
        sg = seg[: c1 - c0, : c1 - c0]
        parts.append(_dot(hi[:, c0:c1], sg) + _dot(lo[:, c0:c1], sg))
    ss = jnp.concatenate(parts, axis=1)
    qkn = qk * lax.rsqrt(ss * (1.0 / A_HEAD_DIM) + EPS) * gqk_ref[...]
    head_lane = lax.broadcasted_iota(jnp.int32, (1, LANES), 1) % A_HEAD_DIM
    cos = jnp.where(head_lane < ROT_DIM, tab, 1.0)
    sin = jnp.where(head_lane < ROT_DIM, pltpu.roll(tab, LANES - ROT_DIM, 1), 0.0)
    first_half = head_lane < ROT_DIM // 2
    slabs = []
    for s in range(C_VA // LANES):
        x = qkn[:, s * LANES:(s + 1) * LANES]
        up = pltpu.roll(x, LANES - ROT_DIM // 2, 1)
        dn = pltpu.roll(x, ROT_DIM // 2, 1)
        slabs.append(x * cos + jnp.where(first_half, up, dn) * sin)
    q = jnp.concatenate(slabs[:A_WIDTH // LANES], axis=1)
    k = slabs[A_WIDTH // LANES]
    return q, k, v


def _kv_operands(k, v, lo_half):
    ksw = pltpu.roll(k, A_HEAD_DIM, 1)
    vsw = pltpu.roll(v, A_HEAD_DIM, 1)
    one = jnp.ones_like(v)
    kd = (jnp.where(lo_half, k, ksw).astype(BF16), jnp.where(lo_half, ksw, k).astype(BF16))
    vx = (jnp.concatenate([jnp.where(lo_half, v, one), jnp.where(lo_half, one, vsw)], axis=1).astype(BF16),
          jnp.concatenate([jnp.where(lo_half, vsw, one), jnp.where(lo_half, one, v)], axis=1).astype(BF16))
    return kd, vx


def _attn_scores(q_blk, j, kd, lo_half):
    a = q_blk[:, (2 * j) * LANES:(2 * j + 1) * LANES]
    b = q_blk[:, (2 * j + 1) * LANES:(2 * j + 2) * LANES]
    z = jnp.zeros_like(a)
    stack = jnp.concatenate([jnp.where(lo_half, a, z), jnp.where(lo_half, z, a),
                             jnp.where(lo_half, b, z), jnp.where(lo_half, z, b)], axis=0).astype(BF16)
    return _dot_nt(stack, kd)


def _attn_probs(s, j, bias, sink_ref, layer):
    rows = s.shape[0] // A_GROUP
    if bias is not None:
        s = s + bias
    sk = jnp.concatenate([jnp.full((rows, 1), sink_ref[layer, A_GROUP * j + g], F32) for g in range(A_GROUP)],
                         axis=0)
    m = jnp.maximum(jnp.max(s, axis=-1, keepdims=True), sk)
    return jnp.exp(s - m).astype(BF16), jnp.exp(sk - m)


def _attn_values(p, de, vx, lo_half):
    rows = p.shape[0] // A_GROUP
    r = _dot(p, vx)
    r_lo, r_hi = r[:, :LANES], r[:, LANES:]

    def slab(g0):
        e = slice(g0 * rows, (g0 + 1) * rows)
        o = slice((g0 + 1) * rows, (g0 + 2) * rows)
        return jnp.where(lo_half, r_lo[e] / (r_hi[e] + de[e]), r_hi[o] / (r_lo[o] + de[o]))

    return slab(0), slab(2)


def _attn_block(q_blk, j, kd, vx, bias, sink_ref, layer, lo_half):
    p, de = _attn_probs(_attn_scores(q_blk, j, kd, lo_half), j, bias, sink_ref, layer)
    return _attn_values(p, de, vx, lo_half)


def _gla_inputs(xn, win_ref, wab_ref, wgu_ref, bg_ref, block):
    rows = xn.shape[0]
    qb = _dot(xn, win_ref[:, C_QB:C_KB])
    kb = _dot(xn, win_ref[:, C_KB:C_VB])
    vb = _dot(xn, win_ref[:, C_VB:C_GB])
    ab = _dot(xn, wab_ref[...])
    logg = _log_sigmoid(_dot(ab.astype(BF16), wgu_ref[...]) + bg_ref[...]) * (1.0 / GATE_TAU)
    g_hi, g_lo = _split_bf16(logg)
    tri = _block_mask(rows, block, lower=True)
    ones = _block_mask(rows, block, lower=False)
    b = _dot(tri, g_hi) + _dot(tri, g_lo)
    b_last = _dot(ones, g_hi) + _dot(ones, g_lo)
    qt = qb * (jnp.exp(b) * (B_KEY_DIM ** -0.5))
    kt = kb * jnp.exp(-b)
    ke = kb * jnp.exp(b_last - b)
    return qt, kt, ke, vb, logg


def _head_stack(x, head_mask):
    z = jnp.zeros_like(x)
    return jnp.concatenate([jnp.where(head_mask[h], x, z) for h in range(B_HEADS)], axis=0).astype(BF16)


def _finish_layer(h, oa, ob, xn, p, win_ref, glag_ref, wout_ref, peg_ref, wpe_ref, wpg_ref):
    ga = _dot(xn, win_ref[:, C_GA:C_QB])
    gb = _dot(xn, win_ref[:, C_GB:C_AB])
    obn = jnp.concatenate(
        [_rms(ob[:, hh * B_VAL_DIM:(hh + 1) * B_VAL_DIM], glag_ref[...]) for hh in range(B_HEADS)], axis=1)
    mix = jnp.concatenate([oa * _silu(ga), obn * _silu(gb)], axis=1).astype(BF16)
    h1 = h + _dot(mix, wout_ref[...])
    gate = _sigmoid(_dot(_rms(h1, peg_ref[...]).astype(BF16), wpg_ref[...]))
    return h1 + gate * _dot(p.astype(BF16), wpe_ref[...])


def _gla_decays(kbt, abt, wgut_ref, bgc_ref):
    tile = kbt.shape[1]
    loggt = _log_sigmoid(_dot(wgut_ref[...], abt) + bgc_ref[...]) * (1.0 / GATE_TAU)
    g_hi, g_lo = _split_bf16(loggt)
    r = lax.broadcasted_iota(jnp.int32, (GLA_SPAN, GLA_SPAN), 0)
    c = lax.broadcasted_iota(jnp.int32, (GLA_SPAN, GLA_SPAN), 1)
    upper = jnp.where(((r // CHUNK) == (c // CHUNK)) & (r <= c), 1.0, 0.0).astype(BF16)
    bt = jnp.concatenate(
        [_dot(g_hi[:, s0:s0 + GLA_SPAN], upper) + _dot(g_lo[:, s0:s0 + GLA_SPAN], upper)
         for s0 in range(0, tile, GLA_SPAN)], axis=1)
    ktt = (kbt * jnp.exp(-bt)).astype(BF16)
    pair = 2 * CHUNK
    cols = [bt[:, (c + 1) * CHUNK - 1:(c + 1) * CHUNK] for c in range(tile // CHUNK)]
    lane = lax.broadcasted_iota(jnp.int32, (1, pair), 1)
    ket = jnp.concatenate(
        [kbt[:, pc * pair:(pc + 1) * pair]
         * jnp.exp(jnp.where(lane < CHUNK, cols[2 * pc], cols[2 * pc + 1]) - bt[:, pc * pair:(pc + 1) * pair])
         for pc in range(tile // pair)], axis=1)
    return bt, ktt, ket, cols


def _gla_chunk_jobs(qb, qt, vb, kbt, bt, ktt, ket, cols, stk, out):
    tile = qt.shape[0]
    n_chunks = tile // CHUNK
    pair = 2 * CHUNK
    head_of_lane = lax.broadcasted_iota(jnp.int32, (1, B_QK_WIDTH), 1) // B_KEY_DIM
    head_mask = [head_of_lane == hh for hh in range(B_HEADS)]
    lane = lax.broadcasted_iota(jnp.int32, (1, pair), 1)
    t_row = lax.broadcasted_iota(jnp.int32, (CHUNK, pair), 0)
    s_col = lax.broadcasted_iota(jnp.int32, (CHUNK, pair), 1)
    own = [lane < CHUNK, lane >= CHUNK]
    causal = [s_col <= t_row, (s_col >= CHUNK) & (s_col - CHUNK <= t_row)]
    v_heads = [[vb[(c // 2) * pair:(c // 2 + 1) * pair, hh * B_VAL_DIM:(hh + 1) * B_VAL_DIM]
                for hh in range(B_HEADS)] for c in range(n_chunks)]
    updates, states, results, ob_rows = {}, {}, {}, {}

    def update(c):
        ps = slice((c // 2) * pair, (c // 2 + 1) * pair)
        ket_c = jnp.where(own[c % 2], ket[:, ps], 0.0).astype(BF16)
        updates[c] = jnp.concatenate(
            [_dot(ket_c[hh * B_KEY_DIM:(hh + 1) * B_KEY_DIM], v_heads[c][hh]) for hh in range(B_HEADS)], axis=0)

    def chain():
        s = stk
        for c in range(n_chunks):
            states[c] = s
            s = s * jnp.exp(cols[c]) + updates[c]
        out["state"] = s

    def result(c):
        ps = slice((c // 2) * pair, (c // 2 + 1) * pair)
        q_stack = _head_stack(qt[c * CHUNK:(c + 1) * CHUNK], head_mask)
        results[c] = _dot(q_stack, jnp.concatenate([states[c].astype(BF16), ktt[:, ps]], axis=1))

    def chunk_output(c, res):
        outs = []
        for hh in range(B_HEADS):
            hr = slice(hh * CHUNK, (hh + 1) * CHUNK)
            a_h = jnp.where(causal[c % 2], res[hr, B_VAL_DIM:], 0.0).astype(BF16)
            outs.append(_dot(a_h, v_heads[c][hh]) + res[hr, :B_VAL_DIM])
        return jnp.concatenate(outs, axis=1)

    def output(c):
        ob_rows[c] = chunk_output(c, results.pop(c))
        if c == n_chunks - 1:
            out["ob"] = jnp.concatenate([ob_rows[cc] for cc in range(n_chunks)], axis=0)

    def wide_range_ob():
        n_sub = CHUNK // GLA_SUB
        rows = []
        for c in range(n_chunks):
            ps = slice((c // 2) * pair, (c // 2 + 1) * pair)
            first = (c % 2) * CHUNK
            bt_c = bt[:, c * CHUNK:(c + 1) * CHUNK]
            res = []
            for i in range(n_sub):
                r0 = c * CHUNK + i * GLA_SUB
                ref = bt_c[:, i * GLA_SUB - 1:i * GLA_SUB] if i else jnp.zeros((B_QK_WIDTH, 1), F32)
                rel_q = (bt_c[:, i * GLA_SUB:(i + 1) * GLA_SUB] - ref).T
                q_sub = qb[r0:r0 + GLA_SUB] * (jnp.exp(rel_q) * (B_KEY_DIM ** -0.5))
                visible = (lane >= first) & (lane < first + (i + 1) * GLA_SUB)
                keys = kbt[:, ps] * jnp.exp(jnp.where(visible, ref - bt[:, ps], NEG_INF))
                state = (states[c] * jnp.exp(ref)).astype(BF16)
                res.append(_dot(_head_stack(q_sub, head_mask),
                                jnp.concatenate([state, keys.astype(BF16)], axis=1)))
            rows.append(chunk_output(c, jnp.concatenate(
                [res[i][hh * GLA_SUB:(hh + 1) * GLA_SUB] for hh in range(B_HEADS) for i in range(n_sub)], axis=0)))
        return jnp.concatenate(rows, axis=0)

    out["wide_range_ob"] = wide_range_ob

    bind = lambda f, c: (lambda: f(c))
    return ([bind(update, c) for c in range(n_chunks)] + [chain]
            + [bind(result, c) for c in range(n_chunks)] + [bind(output, c) for c in range(n_chunks)])


def _trace_interleaved(jobs_a, jobs_b):
    na, nb = len(jobs_a), len(jobs_b)
    ia = ib = 0
    while ia < na or ib < nb:
        if ib >= nb or (ia < na and ia * nb <= ib * na):
            jobs_a[ia]()
            ia += 1
        else:
            jobs_b[ib]()
            ib += 1


def _prompt_kernel(sink_ref, h_ref, p_ref, rope_ref, ng_ref, win_ref, wab_ref, gqk_ref, wgu_ref, bg_ref, glag_ref,
                   wout_ref, peg_ref, wpe_ref, wpg_ref, wkbt_ref, wabt_ref, wgut_ref, bgc_ref,
                   ho_ref, nk_ref, nv_ref, st_ref,
                   kd_scr, vx_scr, st_scr, *, layer):
    tile = PROMPT_TILE
    n_chunks = tile // CHUNK
    band = WINDOW + CHUNK
    i = pl.program_id(0)
    rd = (i + 1) % 2
    wr = i % 2

    @pl.when(i == 0)
    def _():
        kd_scr[rd] = jnp.zeros(kd_scr.shape[1:], BF16)
        vx_scr[rd] = jnp.zeros(vx_scr.shape[1:], BF16)
        st_scr[rd] = jnp.zeros(st_scr.shape[1:], F32)

    h = h_ref[...]
    xn = _rms(h, ng_ref[...]).astype(BF16)
    lo_half = lax.broadcasted_iota(jnp.int32, (1, LANES), 1) < A_HEAD_DIM
    key_blk = lax.broadcasted_iota(jnp.int32, (1, band), 1) // CHUNK
    blocks = [(c, j) for c in range(n_chunks) for j in range(A_KV_HEADS)]
    n_blocks = len(blocks)
    v, scores, probs, slabs, gla = {}, {}, {}, {}, {}
    bind = lambda f, n: (lambda: f(n))

    qkv = _dot(xn, win_ref[:, C_QA:C_GA])

    def rope_tab():
        tab = rope_ref[...].T
        head_lane = lax.broadcasted_iota(jnp.int32, (1, LANES), 1) % A_HEAD_DIM
        v["cos"] = jnp.where(head_lane < ROT_DIM, tab, 1.0)
        v["sin"] = jnp.where(head_lane < ROT_DIM, pltpu.roll(tab, LANES - ROT_DIM, 1), 0.0)
        v["first_half"] = head_lane < ROT_DIM // 2

    def norm(g):
        width = min(256, C_VA - 256 * g)
        x = qkv[:, 256 * g:256 * g + width]
        seg = _block_mask(width, A_HEAD_DIM, lower=False)
        hi, lo = _split_bf16(x * x)
        ss = _dot(hi, seg) + _dot(lo, seg)
        xs = x * lax.rsqrt(ss * (1.0 / A_HEAD_DIM) + EPS) * gqk_ref[:, 256 * g:256 * g + width]
        for s in range(width // LANES):
            x1 = xs[:, s * LANES:(s + 1) * LANES]
            up = pltpu.roll(x1, LANES - ROT_DIM // 2, 1)
            dn = pltpu.roll(x1, ROT_DIM // 2, 1)
            v[f"slab{2 * g + s}"] = x1 * v["cos"] + jnp.where(v["first_half"], up, dn) * v["sin"]

    def kv_ops():
        k, vv = v["slab4"], qkv[:, C_VA:]
        nk_ref[...] = k[tile - WINDOW:, :]
        nv_ref[...] = vv[tile - WINDOW:, :]
        kd, vx = _kv_operands(k, vv, lo_half)
        v["kd_ext"] = [jnp.concatenate([kd_scr[rd, j], kd[j]], axis=0) for j in range(A_KV_HEADS)]
        v["vx_ext"] = [jnp.concatenate([vx_scr[rd, j], vx[j]], axis=0) for j in range(A_KV_HEADS)]
        for j in range(A_KV_HEADS):
            kd_scr[wr, j] = kd[j][tile - WINDOW:, :]
            vx_scr[wr, j] = vx[j][tile - WINDOW:, :]

    def proj_k():
        v["kbt"] = _dot_nt(wkbt_ref[...], xn)
        v["abt"] = _dot_nt(wabt_ref[...], xn).astype(BF16)

    def proj_qv():
        v["qb"] = _dot(xn, win_ref[:, C_QB:C_KB])
        v["vb"] = _dot(xn, win_ref[:, C_VB:C_GB]).astype(BF16)

    _trace_interleaved([rope_tab, bind(norm, 2), kv_ops, bind(norm, 0), bind(norm, 1)], [proj_k, proj_qv])

    def score(n):
        c, j = blocks[n]
        cs = slice(c * CHUNK, (c + 1) * CHUNK)
        q_blk = jnp.concatenate([v[f"slab{2 * j}"][cs], v[f"slab{2 * j + 1}"][cs]], axis=1)
        scores[n] = _attn_scores(q_blk, 0, v["kd_ext"][j][c * CHUNK:c * CHUNK + band, :], lo_half)

    def decays():
        gla["bt"], gla["ktt"], gla["ket"], gla["cols"] = _gla_decays(v["kbt"], v["abt"], wgut_ref, bgc_ref)

    def scale_q():
        gla["qt"] = v["qb"] * (jnp.exp(gla["bt"].T) * (B_KEY_DIM ** -0.5))

    _trace_interleaved([bind(score, n) for n in range(n_blocks)], [decays, scale_q])

    def prob(n):
        c, j = blocks[n]
        first_key_chunk = i * n_chunks + c - WINDOW // CHUNK
        bias = jnp.where(key_blk + first_key_chunk >= 0, 0.0, NEG_INF)
        probs[n] = _attn_probs(scores.pop(n), j, bias, sink_ref, layer)

    def value(n):
        c, j = blocks[n]
        slabs[n] = _attn_values(*probs.pop(n), v["vx_ext"][j][c * CHUNK:c * CHUNK + band, :], lo_half)

    def proj_ga():
        v["ga"] = _dot(xn, win_ref[:, C_GA:C_QB])

    def proj_gb():
        v["gb"] = _dot(xn, win_ref[:, C_GB:C_AB])

    def embed():
        v["pe"] = _dot(p_ref[...].astype(BF16), wpe_ref[...])

    gla_jobs = _gla_chunk_jobs(v["qb"], gla["qt"], v["vb"], v["kbt"], gla["bt"], gla["ktt"], gla["ket"], gla["cols"],
                               st_scr[rd], gla)
    n_first = n_chunks + 1
    _trace_interleaved([bind(prob, n) for n in range(n_blocks)], [proj_ga, proj_gb, embed] + gla_jobs[:n_first])
    _trace_interleaved([bind(value, n) for n in range(n_blocks)], gla_jobs[n_first:])
    st_scr[wr] = gla["state"]
    st_ref[...] = gla["state"]
    ob, pe, ga, gb = gla["ob"], v["pe"], v["ga"], v["gb"]

    def finish(ob):
        for r0 in range(0, tile, FINISH_ROWS):
            rs = slice(r0, r0 + FINISH_ROWS)
            oa = jnp.concatenate(
                [jnp.concatenate([sl for j in range(A_KV_HEADS) for sl in slabs[c * A_KV_HEADS + j]], axis=1)
                 for c in range(r0 // CHUNK, (r0 + FINISH_ROWS) // CHUNK)], axis=0)
            h1 = h[rs] + _dot((oa * _silu(ga[rs])).astype(BF16), wout_ref[0:A_WIDTH, :])
            obn = jnp.concatenate(
                [_rms(ob[rs, hh * B_VAL_DIM:(hh + 1) * B_VAL_DIM], glag_ref[...]) for hh in range(B_HEADS)], axis=1)
            h2 = h1 + _dot((obn * _silu(gb[rs])).astype(BF16), wout_ref[A_WIDTH:, :])
            gate = _sigmoid(_dot(_rms(h2, peg_ref[...]).astype(BF16), wpg_ref[...]))
            ho_ref[rs, :] = h2 + gate * pe[rs]

    finish(ob)

    strongest = gla["cols"][0]
    for c in range(1, n_chunks):
        strongest = jnp.minimum(strongest, gla["cols"][c])

    @pl.when(jnp.min(strongest) < -GLA_FAST_MAX_DECAY)
    def _():
        finish(gla["wide_range_ob"]())


def _sample_kernel(sink_ref, h_ref, p_ref, rope_ref, ck_ref, cv_ref, s0_ref, ng_ref, win_ref, wab_ref, gqk_ref, wgu_ref,
                   bg_ref, glag_ref, wout_ref, peg_ref, wpe_ref, wpg_ref,
                   ho_ref, nk_ref, nv_ref, st_ref, *, layer, dec_seq):
    nb = SAMPLE_BATCH_TILE
    h = h_ref[...]
    xn = _rms(h, ng_ref[...]).astype(BF16)
    lo_half = lax.broadcasted_iota(jnp.int32, (1, LANES), 1) < A_HEAD_DIM

    q, k, v = _project_a(xn, rope_ref[...].T, win_ref, gqk_ref)
    nk_ref[...] = k
    nv_ref[...] = v
    kd_new, vx_new = _kv_operands(k, v, lo_half)
    rows_of = [slice(bi * dec_seq, (bi + 1) * dec_seq) for bi in range(nb)]
    blocks = [(bi, j) for bi in range(nb) for j in range(A_KV_HEADS)]
    old = [_kv_operands(ck_ref[bi], cv_ref[bi], lo_half) for bi in range(nb)]
    scores = [_attn_scores(q[rows_of[bi]], j,
                           jnp.concatenate([old[bi][0][j], kd_new[j][rows_of[bi]]], axis=0), lo_half)
              for bi, j in blocks]
    probs = [_attn_probs(s, j, None, sink_ref, layer) for (bi, j), s in zip(blocks, scores)]
    slabs = [_attn_values(p, de, jnp.concatenate([old[bi][1][j], vx_new[j][rows_of[bi]]], axis=0), lo_half)
             for (bi, j), (p, de) in zip(blocks, probs)]
    oa = jnp.concatenate(
        [jnp.concatenate([sl for j in range(A_KV_HEADS) for sl in slabs[bi * A_KV_HEADS + j]], axis=1)
         for bi in range(nb)], axis=0)

    qt, kt, ke, vb, g_all = _gla_inputs(xn, win_ref, wab_ref, wgu_ref, bg_ref, dec_seq)
    head_of_lane = lax.broadcasted_iota(jnp.int32, (1, B_QK_WIDTH), 1) // B_KEY_DIM
    head_mask = [head_of_lane == hh for hh in range(B_HEADS)]
    causal = (lax.broadcasted_iota(jnp.int32, (dec_seq, dec_seq), 0)
              >= lax.broadcasted_iota(jnp.int32, (dec_seq, dec_seq), 1))
    ones_rhs = jnp.ones((dec_seq, LANES), BF16)
    vb = vb.astype(BF16)
    g_hi, g_lo = _split_bf16(g_all)
    q_stacks = [_head_stack(qt[rs], head_mask) for rs in rows_of]
    atts = [_dot_nt(q_stacks[bi], kt[rows_of[bi]].astype(BF16)) for bi in range(nb)]
    inters = [_dot(q_stacks[bi], s0_ref[bi].astype(BF16)) for bi in range(nb)]
    totals = [_dot_tn(g_hi[rs], ones_rhs) + _dot_tn(g_lo[rs], ones_rhs) for rs in rows_of]
    upd_fulls = [_dot_tn(ke[rs].astype(BF16), vb[rs]) for rs in rows_of]
    atts = [jnp.concatenate([jnp.where(causal, a[hh * dec_seq:(hh + 1) * dec_seq], 0.0)
                             for hh in range(B_HEADS)], axis=0).astype(BF16) for a in atts]
    intras = [_dot(atts[bi], vb[rows_of[bi]]) for bi in range(nb)]
    ob = jnp.concatenate(
        [jnp.concatenate([intras[bi][hh * dec_seq:(hh + 1) * dec_seq, hh * B_VAL_DIM:(hh + 1) * B_VAL_DIM]
                          + inters[bi][hh * dec_seq:(hh + 1) * dec_seq] for hh in range(B_HEADS)], axis=1)
         for bi in range(nb)], axis=0)
    for bi in range(nb):
        upd = jnp.concatenate(
            [upd_fulls[bi][hh * B_KEY_DIM:(hh + 1) * B_KEY_DIM, hh * B_VAL_DIM:(hh + 1) * B_VAL_DIM]
             for hh in range(B_HEADS)], axis=0)
        st_ref[bi] = s0_ref[bi] * jnp.exp(totals[bi]) + upd

    ho_ref[...] = _finish_layer(h, oa, ob, xn, p_ref[...], win_ref, glag_ref, wout_ref, peg_ref,
                                wpe_ref, wpg_ref)


def _rope_table(pos):
    half = ROT_DIM // 2
    inv = jnp.power(jnp.float32(ROPE_THETA), -jnp.arange(half, dtype=F32) * (2.0 / ROT_DIM))
    ang = inv[:, None] * pos.astype(F32)[None, :]
    cos, sin = jnp.cos(ang), jnp.sin(ang)
    head = jnp.concatenate([cos, cos, -sin, sin, jnp.zeros((A_HEAD_DIM - 2 * ROT_DIM, pos.shape[0]), F32)], axis=0)
    return jnp.concatenate([head, head], axis=0)


def _layer_spec(layer, shape):
    return pl.BlockSpec((None,) + shape, lambda i: (layer,) + (0,) * len(shape))


def _weight_specs(layer):
    return [
        _layer_spec(layer, (1, D_MODEL)),
        pl.BlockSpec((None, D_MODEL, C_AB), lambda i: (layer, 0, 0)),
        _layer_spec(layer, (D_MODEL, LANES)),
        _layer_spec(layer, (1, C_VA)),
        _layer_spec(layer, (LANES, B_QK_WIDTH)),
        _layer_spec(layer, (1, B_QK_WIDTH)),
        _layer_spec(layer, (1, B_VAL_DIM)),
        _layer_spec(layer, (A_WIDTH + B_WIDTH, D_MODEL)),
        _layer_spec(layer, (1, D_MODEL)),
        _layer_spec(layer, (D_PLE, D_MODEL)),
        _layer_spec(layer, (D_MODEL, D_MODEL)),
    ]


def _prompt_layer(layer, h, p, rope, sinks, weights, weights_t):
    seq = h.shape[0]
    tile = PROMPT_TILE
    assert seq % tile == 0 and tile % GLA_SPAN == 0 and GLA_SPAN % (2 * CHUNK) == 0 and tile >= WINDOW
    row = lambda w: pl.BlockSpec((tile, w), lambda i: (i, 0))
    const = lambda shape: pl.BlockSpec(shape, lambda i: (0,) * len(shape))
    return pl.pallas_call(
        functools.partial(_prompt_kernel, layer=layer),
        grid=(seq // tile,),
        in_specs=[pl.BlockSpec(memory_space=pltpu.SMEM), row(D_MODEL),
                  pl.BlockSpec((None, tile, D_PLE), lambda i: (layer, i, 0)),
                  pl.BlockSpec((LANES, tile), lambda i: (0, i))]
        + _weight_specs(layer)
        + [_layer_spec(layer, (B_QK_WIDTH, D_MODEL)),
           _layer_spec(layer, (GATE_RANK, D_MODEL)),
           _layer_spec(layer, (B_QK_WIDTH, GATE_RANK)),
           _layer_spec(layer, (B_QK_WIDTH, 1))],
        out_specs=[row(D_MODEL), const((WINDOW, A_KV_WIDTH)), const((WINDOW, A_KV_WIDTH)),
                   const((B_QK_WIDTH, B_VAL_DIM))],
        out_shape=[jax.ShapeDtypeStruct((seq, D_MODEL), F32),
                   jax.ShapeDtypeStruct((WINDOW, A_KV_WIDTH), F32),
                   jax.ShapeDtypeStruct((WINDOW, A_KV_WIDTH), F32),
                   jax.ShapeDtypeStruct((B_QK_WIDTH, B_VAL_DIM), F32)],
        scratch_shapes=[pltpu.VMEM((2, A_KV_HEADS, WINDOW, LANES), BF16),
                        pltpu.VMEM((2, A_KV_HEADS, WINDOW, 2 * LANES), BF16),
                        pltpu.VMEM((2, B_QK_WIDTH, B_VAL_DIM), F32)],
        compiler_params=pltpu.CompilerParams(dimension_semantics=("arbitrary",),
                                             vmem_limit_bytes=VMEM_LIMIT_BYTES),
        name="prompt_layer",
    )(sinks, h, p, rope, *weights, *weights_t)


def _sample_layer(layer, h, p, rope, ck, cv, s0, sinks, weights, dec_seq):
    n_seq = ck.shape[1]
    nb = SAMPLE_BATCH_TILE
    assert n_seq % nb == 0
    rows = nb * dec_seq
    row = lambda w: pl.BlockSpec((rows, w), lambda i: (i, 0))
    per_seq_in = lambda a, b: pl.BlockSpec((None, nb, a, b), lambda i: (layer, i, 0, 0))
    return pl.pallas_call(
        functools.partial(_sample_kernel, layer=layer, dec_seq=dec_seq),
        grid=(n_seq // nb,),
        in_specs=[pl.BlockSpec(memory_space=pltpu.SMEM), row(D_MODEL),
                  pl.BlockSpec((None, rows, D_PLE), lambda i: (layer, i, 0)),
                  pl.BlockSpec((LANES, rows), lambda i: (0, 0)),
                  per_seq_in(WINDOW, A_KV_WIDTH), per_seq_in(WINDOW, A_KV_WIDTH),
                  per_seq_in(B_QK_WIDTH, B_VAL_DIM)]
        + _weight_specs(layer),
        out_specs=[row(D_MODEL), row(A_KV_WIDTH), row(A_KV_WIDTH),
                   pl.BlockSpec((nb, B_QK_WIDTH, B_VAL_DIM), lambda i: (i, 0, 0))],
        out_shape=[jax.ShapeDtypeStruct((n_seq * dec_seq, D_MODEL), F32),
                   jax.ShapeDtypeStruct((n_seq * dec_seq, A_KV_WIDTH), F32),
                   jax.ShapeDtypeStruct((n_seq * dec_seq, A_KV_WIDTH), F32),
                   jax.ShapeDtypeStruct((n_seq, B_QK_WIDTH, B_VAL_DIM), F32)],
        compiler_params=pltpu.CompilerParams(dimension_semantics=("arbitrary",),
                                             vmem_limit_bytes=VMEM_LIMIT_BYTES),
        name="sample_layer",
    )(sinks, h, p, rope, ck, cv, s0, *weights)


def kernel(x_prompt, x_sample, cache_k, cache_v, state_gla, p_prompt, p_sample, norm_g, w_in, q_norm_g, k_norm_g,
           sinks, w_gate_up, b_gate, gla_norm_g, w_out, pe_norm_g, w_pe, w_pg):
    batch, seq, _ = x_prompt.shape
    n_seq, dec_seq, _ = x_sample.shape
    assert batch == 1 and cache_k.shape[2] == WINDOW

    rope_p = _rope_table(jnp.arange(seq, dtype=jnp.int32))
    rope_s = jnp.tile(_rope_table(PAST_LEN + jnp.arange(dec_seq, dtype=jnp.int32)), (1, SAMPLE_BATCH_TILE))

    w_ab = w_in[:, :, C_AB:IN_COLS]
    wab_b = jnp.pad(w_ab, ((0, 0), (0, 0), (0, LANES - GATE_RANK))).astype(BF16)
    wgu_b = jnp.pad(w_gate_up, ((0, 0), (0, LANES - GATE_RANK), (0, 0))).astype(BF16)
    gqk = jnp.concatenate([jnp.tile(q_norm_g, (1, A_HEADS)) * (A_HEAD_DIM ** -0.5),
                           jnp.tile(k_norm_g, (1, A_KV_HEADS))], axis=1)
    vec = lambda a: a[:, None, :]
    weights = (vec(norm_g), w_in.astype(BF16), wab_b, vec(gqk), wgu_b, vec(b_gate), vec(gla_norm_g), w_out.astype(BF16),
               vec(pe_norm_g), w_pe.astype(BF16), w_pg.astype(BF16))
    weights_t = (jnp.swapaxes(w_in[:, :, C_KB:C_VB], 1, 2).astype(BF16),
                 jnp.swapaxes(w_ab, 1, 2).astype(BF16),
                 jnp.swapaxes(w_gate_up, 1, 2).astype(BF16),
                 b_gate[:, :, None])

    hp = x_prompt.reshape(seq, D_MODEL)
    hs = x_sample.reshape(n_seq * dec_seq, D_MODEL)
    ck = cache_k.reshape(DEPTH, n_seq, WINDOW, A_KV_WIDTH)
    cv = cache_v.reshape(DEPTH, n_seq, WINDOW, A_KV_WIDTH)
    s0 = state_gla.reshape(DEPTH, n_seq, B_QK_WIDTH, B_VAL_DIM)
    pp = p_prompt.reshape(DEPTH, seq, D_PLE)
    ps = p_sample.reshape(DEPTH, n_seq * dec_seq, D_PLE)

    kp_l, vp_l, sp_l, ks_l, vs_l, ss_l = [], [], [], [], [], []
    for layer in range(DEPTH):
        hp, kn, vn, sn = _prompt_layer(layer, hp, pp, rope_p, sinks, weights, weights_t)
        kp_l.append(kn); vp_l.append(vn); sp_l.append(sn)
        hs, kn, vn, sn = _sample_layer(layer, hs, ps, rope_s, ck, cv, s0, sinks, weights, dec_seq)
        ks_l.append(kn); vs_l.append(vn); ss_l.append(sn)

    return (hp.reshape(batch, seq, D_MODEL),
            hs.reshape(n_seq, dec_seq, D_MODEL),
            jnp.stack(kp_l).reshape(DEPTH, batch, WINDOW, A_KV_HEADS, A_HEAD_DIM),
            jnp.stack(vp_l).reshape(DEPTH, batch, WINDOW, A_KV_HEADS, A_HEAD_DIM),
            jnp.stack(sp_l).reshape(DEPTH, batch, B_HEADS, B_KEY_DIM, B_VAL_DIM),
            jnp.stack(ks_l).reshape(DEPTH, n_seq, dec_seq, A_KV_HEADS, A_HEAD_DIM),
            jnp.stack(vs_l).reshape(DEPTH, n_seq, dec_seq, A_KV_HEADS, A_HEAD_DIM),
            jnp.stack(ss_l).reshape(DEPTH, n_seq, B_HEADS, B_KEY_DIM, B_VAL_DIM))
```

```python
import functools

import jax
import jax.numpy as jnp
from jax import lax
from jax.experimental import pallas as pl
from jax.experimental.pallas import tpu as pltpu

F32 = jnp.float32
BF16 = jnp.bfloat16

D_MODEL = 1024
DEPTH = 4
PAST_LEN = 1024
CHUNK = 64
D_PLE = 256
A_HEADS = 8
A_KV_HEADS = 2
A_HEAD_DIM = 64
A_GROUP = A_HEADS // A_KV_HEADS
A_WIDTH = A_HEADS * A_HEAD_DIM
A_KV_WIDTH = A_KV_HEADS * A_HEAD_DIM
WINDOW = 128
ROT_DIM = A_HEAD_DIM // 4
ROPE_THETA = 500000.0
B_HEADS = 4
B_KEY_DIM = 64
B_VAL_DIM = 128
B_QK_WIDTH = B_HEADS * B_KEY_DIM
B_WIDTH = B_HEADS * B_VAL_DIM
GATE_RANK = 16
GATE_TAU = 16.0
NEG_INF = -1e30
EPS = 1e-6

LANES = 128

C_QA = 0
C_KA = C_QA + A_WIDTH
C_VA = C_KA + A_KV_WIDTH
C_GA = C_VA + A_KV_WIDTH
C_QB = C_GA + A_WIDTH
C_KB = C_QB + B_QK_WIDTH
C_VB = C_KB + B_QK_WIDTH
C_GB = C_VB + B_WIDTH
C_AB = C_GB + B_WIDTH
IN_COLS = C_AB + GATE_RANK

PROMPT_TILE = 512
GLA_SPAN = 256
GLA_SUB = 16
GLA_FAST_MAX_DECAY = 60.0
FINISH_ROWS = 256
SAMPLE_BATCH_TILE = 16
CAST_ROWS = 256
VMEM_LIMIT_BYTES = 56 * 1024 * 1024


def _dot(a, b):
    return jnp.dot(a, b, preferred_element_type=F32)


def _dot_nt(a, b):
    return lax.dot_general(a, b, (((1,), (1,)), ((), ())), preferred_element_type=F32)


def _dot_tn(a, b):
    return lax.dot_general(a, b, (((0,), (0,)), ((), ())), preferred_element_type=F32)


def _split_bf16(x):
    hi = x.astype(BF16)
    lo = (x - hi.astype(F32)).astype(BF16)
    return hi, lo


def _rms(x, g):
    ms = jnp.mean(x * x, axis=-1, keepdims=True)
    return x * lax.rsqrt(ms + EPS) * g


def _sigmoid(x):
    return 0.5 + 0.5 * jnp.tanh(0.5 * x)


def _silu(x):
    return x * _sigmoid(x)


def _log_sigmoid(x):
    return jnp.minimum(x, 0.0) - jnp.log(1.0 + jnp.exp(-jnp.abs(x)))


def _block_mask(n, block, lower):
    r = lax.broadcasted_iota(jnp.int32, (n, n), 0)
    c = lax.broadcasted_iota(jnp.int32, (n, n), 1)
    m = (r // block) == (c // block)
    if lower:
        m = m & (r >= c)
    return jnp.where(m, 1.0, 0.0).astype(BF16)


def _project_a(xn, tab, win_ref, gqk_ref):
    return _norm_rope(_dot(xn, win_ref[:, C_QA:C_GA]), tab, gqk_ref)


def _norm_rope(qkv, tab, gqk_ref):
    qk, v = qkv[:, :C_VA], qkv[:, C_VA:]
    seg = _block_mask(256, A_HEAD_DIM, lower=False)
    hi, lo = _split_bf16(qk * qk)
    parts = []
    for c0 in range(0, C_VA, 256):
        c1 = min(c0 + 256, C_VA)
        sg = seg[: c1 - c0, : c1 - c0]
        parts.append(_dot(hi[:, c0:c1], sg) + _dot(lo[:, c0:c1], sg))
    ss = jnp.concatenate(parts, axis=1)
    qkn = qk * lax.rsqrt(ss * (1.0 / A_HEAD_DIM) + EPS) * gqk_ref[...]
    head_lane = lax.broadcasted_iota(jnp.int32, (1, LANES), 1) % A_HEAD_DIM
    cos = jnp.where(head_lane < ROT_DIM, tab, 1.0)
    sin = jnp.where(head_lane < ROT_DIM, pltpu.roll(tab, LANES - ROT_DIM, 1), 0.0)
    first_half = head_lane < ROT_DIM // 2
    slabs = []
    for s in range(C_VA // LANES):
        x = qkn[:, s * LANES:(s + 1) * LANES]
        up = pltpu.roll(x, LANES - ROT_DIM // 2, 1)
        dn = pltpu.roll(x, ROT_DIM // 2, 1)
        slabs.append(x * cos + jnp.where(first_half, up, dn) * sin)
    q = jnp.concatenate(slabs[:A_WIDTH // LANES], axis=1)
    k = slabs[A_WIDTH // LANES]
    return q, k, v


def _kv_operands(k, v, lo_half):
    ksw = pltpu.roll(k, A_HEAD_DIM, 1)
    vsw = pltpu.roll(v, A_HEAD_DIM, 1)
    one = jnp.ones_like(v)
    kd = (jnp.where(lo_half, k, ksw).astype(BF16), jnp.where(lo_half, ksw, k).astype(BF16))
    vx = (jnp.concatenate([jnp.where(lo_half, v, one), jnp.where(lo_half, one, vsw)], axis=1).astype(BF16),
          jnp.concatenate([jnp.where(lo_half, vsw, one), jnp.where(lo_half, one, v)], axis=1).astype(BF16))
    return kd, vx


def _attn_scores(q_blk, j, kd, lo_half):
    a = q_blk[:, (2 * j) * LANES:(2 * j + 1) * LANES]
    b = q_blk[:, (2 * j + 1) * LANES:(2 * j + 2) * LANES]
    z = jnp.zeros_like(a)
    stack = jnp.concatenate([jnp.where(lo_half, a, z), jnp.where(lo_half, z, a),
                             jnp.where(lo_half, b, z), jnp.where(lo_half, z, b)], axis=0).astype(BF16)
    return _dot_nt(stack, kd)


def _attn_probs(s, j, bias, sink_ref, layer):
    rows = s.shape[0] // A_GROUP
    if bias is not None:
        s = s + bias
    sk = jnp.concatenate([jnp.full((rows, 1), sink_ref[layer, A_GROUP * j + g], F32) for g in range(A_GROUP)],
                         axis=0)
    m = jnp.maximum(jnp.max(s, axis=-1, keepdims=True), sk)
    return jnp.exp(s - m).astype(BF16), jnp.exp(sk - m)


def _attn_values(p, de, vx, lo_half):
    rows = p.shape[0] // A_GROUP
    r = _dot(p, vx)
    r_lo, r_hi = r[:, :LANES], r[:, LANES:]

    def slab(g0):
        e = slice(g0 * rows, (g0 + 1) * rows)
        o = slice((g0 + 1) * rows, (g0 + 2) * rows)
        return jnp.where(lo_half, r_lo[e] / (r_hi[e] + de[e]), r_hi[o] / (r_lo[o] + de[o]))

    return slab(0), slab(2)


def _attn_block(q_blk, j, kd, vx, bias, sink_ref, layer, lo_half):
    p, de = _attn_probs(_attn_scores(q_blk, j, kd, lo_half), j, bias, sink_ref, layer)
    return _attn_values(p, de, vx, lo_half)


def _gla_inputs(xn, win_ref, wab_ref, wgu_ref, bg_ref, block):
    rows = xn.shape[0]
    qb = _dot(xn, win_ref[:, C_QB:C_KB])
    kb = _dot(xn, win_ref[:, C_KB:C_VB])
    vb = _dot(xn, win_ref[:, C_VB:C_GB])
    ab = _dot(xn, wab_ref[...])
    logg = _log_sigmoid(_dot(ab.astype(BF16), wgu_ref[...]) + bg_ref[...]) * (1.0 / GATE_TAU)
    g_hi, g_lo = _split_bf16(logg)
    tri = _block_mask(rows, block, lower=True)
    ones = _block_mask(rows, block, lower=False)
    b = _dot(tri, g_hi) + _dot(tri, g_lo)
    b_last = _dot(ones, g_hi) + _dot(ones, g_lo)
    qt = qb * (jnp.exp(b) * (B_KEY_DIM ** -0.5))
    kt = kb * jnp.exp(-b)
    ke = kb * jnp.exp(b_last - b)
    return qt, kt, ke, vb, logg


def _head_stack(x, head_mask):
    z = jnp.zeros_like(x)
    return jnp.concatenate([jnp.where(head_mask[h], x, z) for h in range(B_HEADS)], axis=0).astype(BF16)


def _finish_layer(h, oa, ob, xn, p, win_ref, glag_ref, wout_ref, peg_ref, wpe_ref, wpg_ref):
    ga = _dot(xn, win_ref[:, C_GA:C_QB])
    gb = _dot(xn, win_ref[:, C_GB:C_AB])
    obn = jnp.concatenate(
        [_rms(ob[:, hh * B_VAL_DIM:(hh + 1) * B_VAL_DIM], glag_ref[...]) for hh in range(B_HEADS)], axis=1)
    mix = jnp.concatenate([oa * _silu(ga), obn * _silu(gb)], axis=1).astype(BF16)
    h1 = h + _dot(mix, wout_ref[...])
    gate = _sigmoid(_dot(_rms(h1, peg_ref[...]).astype(BF16), wpg_ref[...]))
    return h1 + gate * _dot(p.astype(BF16), wpe_ref[...])


def _gla_decays(kbt, abt, wgut_ref, bgc_ref):
    tile = kbt.shape[1]
    loggt = _log_sigmoid(_dot(wgut_ref[...], abt) + bgc_ref[...]) * (1.0 / GATE_TAU)
    g_hi, g_lo = _split_bf16(loggt)
    r = lax.broadcasted_iota(jnp.int32, (GLA_SPAN, GLA_SPAN), 0)
    c = lax.broadcasted_iota(jnp.int32, (GLA_SPAN, GLA_SPAN), 1)
    upper = jnp.where(((r // CHUNK) == (c // CHUNK)) & (r <= c), 1.0, 0.0).astype(BF16)
    bt = jnp.concatenate(
        [_dot(g_hi[:, s0:s0 + GLA_SPAN], upper) + _dot(g_lo[:, s0:s0 + GLA_SPAN], upper)
         for s0 in range(0, tile, GLA_SPAN)], axis=1)
    ktt = (kbt * jnp.exp(-bt)).astype(BF16)
    pair = 2 * CHUNK
    cols = [bt[:, (c + 1) * CHUNK - 1:(c + 1) * CHUNK] for c in range(tile // CHUNK)]
    lane = lax.broadcasted_iota(jnp.int32, (1, pair), 1)
    ket = jnp.concatenate(
        [kbt[:, pc * pair:(pc + 1) * pair]
         * jnp.exp(jnp.where(lane < CHUNK, cols[2 * pc], cols[2 * pc + 1]) - bt[:, pc * pair:(pc + 1) * pair])
         for pc in range(tile // pair)], axis=1)
    return bt, ktt, ket, cols


def _gla_chunk_jobs(qb, qt, vb, kbt, bt, ktt, ket, cols, stk, out):
    tile = qt.shape[0]
    n_chunks = tile // CHUNK
    pair = 2 * CHUNK
    head_of_lane = lax.broadcasted_iota(jnp.int32, (1, B_QK_WIDTH), 1) // B_KEY_DIM
    head_mask = [head_of_lane == hh for hh in range(B_HEADS)]
    lane = lax.broadcasted_iota(jnp.int32, (1, pair), 1)
    t_row = lax.broadcasted_iota(jnp.int32, (CHUNK, pair), 0)
    s_col = lax.broadcasted_iota(jnp.int32, (CHUNK, pair), 1)
    own = [lane < CHUNK, lane >= CHUNK]
    causal = [s_col <= t_row, (s_col >= CHUNK) & (s_col - CHUNK <= t_row)]
    v_heads = [[vb[(c // 2) * pair:(c // 2 + 1) * pair, hh * B_VAL_DIM:(hh + 1) * B_VAL_DIM]
                for hh in range(B_HEADS)] for c in range(n_chunks)]
    updates, states, results, ob_rows = {}, {}, {}, {}

    def update(c):
        ps = slice((c // 2) * pair, (c // 2 + 1) * pair)
        ket_c = jnp.where(own[c % 2], ket[:, ps], 0.0).astype(BF16)
        updates[c] = jnp.concatenate(
            [_dot(ket_c[hh * B_KEY_DIM:(hh + 1) * B_KEY_DIM], v_heads[c][hh]) for hh in range(B_HEADS)], axis=0)

    def chain():
        s = stk
        for c in range(n_chunks):
            states[c] = s
            s = s * jnp.exp(cols[c]) + updates[c]
        out["state"] = s

    def result(c):
        ps = slice((c // 2) * pair, (c // 2 + 1) * pair)
        q_stack = _head_stack(qt[c * CHUNK:(c + 1) * CHUNK], head_mask)
        results[c] = _dot(q_stack, jnp.concatenate([states[c].astype(BF16), ktt[:, ps]], axis=1))

    def chunk_output(c, res):
        outs = []
        for hh in range(B_HEADS):
            hr = slice(hh * CHUNK, (hh + 1) * CHUNK)
            a_h = jnp.where(causal[c % 2], res[hr, B_VAL_DIM:], 0.0).astype(BF16)
            outs.append(_dot(a_h, v_heads[c][hh]) + res[hr, :B_VAL_DIM])
        return jnp.concatenate(outs, axis=1)

    def output(c):
        ob_rows[c] = chunk_output(c, results.pop(c))
        if c == n_chunks - 1:
            out["ob"] = jnp.concatenate([ob_rows[cc] for cc in range(n_chunks)], axis=0)

    def wide_range_ob():
        n_sub = CHUNK // GLA_SUB
        rows = []
        for c in range(n_chunks):
            ps = slice((c // 2) * pair, (c // 2 + 1) * pair)
            first = (c % 2) * CHUNK
            bt_c = bt[:, c * CHUNK:(c + 1) * CHUNK]
            res = []
            for i in range(n_sub):
                r0 = c * CHUNK + i * GLA_SUB
                ref = bt_c[:, i * GLA_SUB - 1:i * GLA_SUB] if i else jnp.zeros((B_QK_WIDTH, 1), F32)
                rel_q = (bt_c[:, i * GLA_SUB:(i + 1) * GLA_SUB] - ref).T
                q_sub = qb[r0:r0 + GLA_SUB] * (jnp.exp(rel_q) * (B_KEY_DIM ** -0.5))
                visible = (lane >= first) & (lane < first + (i + 1) * GLA_SUB)
                keys = kbt[:, ps] * jnp.exp(jnp.where(visible, ref - bt[:, ps], NEG_INF))
                state = (states[c] * jnp.exp(ref)).astype(BF16)
                res.append(_dot(_head_stack(q_sub, head_mask),
                                jnp.concatenate([state, keys.astype(BF16)], axis=1)))
            rows.append(chunk_output(c, jnp.concatenate(
                [res[i][hh * GLA_SUB:(hh + 1) * GLA_SUB] for hh in range(B_HEADS) for i in range(n_sub)], axis=0)))
        return jnp.concatenate(rows, axis=0)

    out["wide_range_ob"] = wide_range_ob

    bind = lambda f, c: (lambda: f(c))
    return ([bind(update, c) for c in range(n_chunks)] + [chain]
            + [bind(result, c) for c in range(n_chunks)] + [bind(output, c) for c in range(n_chunks)])


def _trace_interleaved(jobs_a, jobs_b):
    na, nb = len(jobs_a), len(jobs_b)
    ia = ib = 0
    while ia < na or ib < nb:
        if ib >= nb or (ia < na and ia * nb <= ib * na):
            jobs_a[ia]()
            ia += 1
        else:
            jobs_b[ib]()
            ib += 1


def _prompt_kernel(sink_ref, h_ref, p_ref, rope_ref, ng_ref, win_ref, wab_ref, gqk_ref, wgu_ref, bg_ref, glag_ref,
                   wout_ref, peg_ref, wpe_ref, wpg_ref, wkbt_ref, wabt_ref, wgut_ref, bgc_ref,
                   ho_ref, nk_ref, nv_ref, st_ref,
                   kd_scr, vx_scr, st_scr, *, layer):
    tile = PROMPT_TILE
    n_chunks = tile // CHUNK
    band = WINDOW + CHUNK
    i = pl.program_id(0)
    rd = (i + 1) % 2
    wr = i % 2

    @pl.when(i == 0)
    def _():
        kd_scr[rd] = jnp.zeros(kd_scr.shape[1:], BF16)
        vx_scr[rd] = jnp.zeros(vx_scr.shape[1:], BF16)
        st_scr[rd] = jnp.zeros(st_scr.shape[1:], F32)

    h = h_ref[...]
    xn = _rms(h, ng_ref[...]).astype(BF16)
    lo_half = lax.broadcasted_iota(jnp.int32, (1, LANES), 1) < A_HEAD_DIM
    key_blk = lax.broadcasted_iota(jnp.int32, (1, band), 1) // CHUNK
    blocks = [(c, j) for c in range(n_chunks) for j in range(A_KV_HEADS)]
    n_blocks = len(blocks)
    v, scores, probs, slabs, gla = {}, {}, {}, {}, {}
    bind = lambda f, n: (lambda: f(n))

    qkv = _dot(xn, win_ref[:, C_QA:C_GA])

    def rope_tab():
        tab = rope_ref[...].T
        head_lane = lax.broadcasted_iota(jnp.int32, (1, LANES), 1) % A_HEAD_DIM
        v["cos"] = jnp.where(head_lane < ROT_DIM, tab, 1.0)
        v["sin"] = jnp.where(head_lane < ROT_DIM, pltpu.roll(tab, LANES - ROT_DIM, 1), 0.0)
        v["first_half"] = head_lane < ROT_DIM // 2

    def norm(g):
        width = min(256, C_VA - 256 * g)
        x = qkv[:, 256 * g:256 * g + width]
        seg = _block_mask(width, A_HEAD_DIM, lower=False)
        hi, lo = _split_bf16(x * x)
        ss = _dot(hi, seg) + _dot(lo, seg)
        xs = x * lax.rsqrt(ss * (1.0 / A_HEAD_DIM) + EPS) * gqk_ref[:, 256 * g:256 * g + width]
        for s in range(width // LANES):
            x1 = xs[:, s * LANES:(s + 1) * LANES]
            up = pltpu.roll(x1, LANES - ROT_DIM // 2, 1)
            dn = pltpu.roll(x1, ROT_DIM // 2, 1)
            v[f"slab{2 * g + s}"] = x1 * v["cos"] + jnp.where(v["first_half"], up, dn) * v["sin"]

    def kv_ops():
        k, vv = v["slab4"], qkv[:, C_VA:]
        nk_ref[...] = k[tile - WINDOW:, :]
        nv_ref[...] = vv[tile - WINDOW:, :]
        kd, vx = _kv_operands(k, vv, lo_half)
        v["kd_ext"] = [jnp.concatenate([kd_scr[rd, j], kd[j]], axis=0) for j in range(A_KV_HEADS)]
        v["vx_ext"] = [jnp.concatenate([vx_scr[rd, j], vx[j]], axis=0) for j in range(A_KV_HEADS)]
        for j in range(A_KV_HEADS):
            kd_scr[wr, j] = kd[j][tile - WINDOW:, :]
            vx_scr[wr, j] = vx[j][tile - WINDOW:, :]

    def proj_k():
        v["kbt"] = _dot_nt(wkbt_ref[...], xn)
        v["abt"] = _dot_nt(wabt_ref[...], xn).astype(BF16)

    def proj_qv():
        v["qb"] = _dot(xn, win_ref[:, C_QB:C_KB])
        v["vb"] = _dot(xn, win_ref[:, C_VB:C_GB]).astype(BF16)

    _trace_interleaved([rope_tab, bind(norm, 2), kv_ops, bind(norm, 0), bind(norm, 1)], [proj_k, proj_qv])

    def score(n):
        c, j = blocks[n]
        cs = slice(c * CHUNK, (c + 1) * CHUNK)
        q_blk = jnp.concatenate([v[f"slab{2 * j}"][cs], v[f"slab{2 * j + 1}"][cs]], axis=1)
        scores[n] = _attn_scores(q_blk, 0, v["kd_ext"][j][c * CHUNK:c * CHUNK + band, :], lo_half)

    def decays():
        gla["bt"], gla["ktt"], gla["ket"], gla["cols"] = _gla_decays(v["kbt"], v["abt"], wgut_ref, bgc_ref)

    def scale_q():
        gla["qt"] = v["qb"] * (jnp.exp(gla["bt"].T) * (B_KEY_DIM ** -0.5))

    _trace_interleaved([bind(score, n) for n in range(n_blocks)], [decays, scale_q])

    def prob(n):
        c, j = blocks[n]
        first_key_chunk = i * n_chunks + c - WINDOW // CHUNK
        bias = jnp.where(key_blk + first_key_chunk >= 0, 0.0, NEG_INF)
        probs[n] = _attn_probs(scores.pop(n), j, bias, sink_ref, layer)

    def value(n):
        c, j = blocks[n]
        slabs[n] = _attn_values(*probs.pop(n), v["vx_ext"][j][c * CHUNK:c * CHUNK + band, :], lo_half)

    def proj_ga():
        v["ga"] = _dot(xn, win_ref[:, C_GA:C_QB])

    def proj_gb():
        v["gb"] = _dot(xn, win_ref[:, C_GB:C_AB])

    def embed():
        v["pe"] = _dot(p_ref[...].astype(BF16), wpe_ref[...])

    gla_jobs = _gla_chunk_jobs(v["qb"], gla["qt"], v["vb"], v["kbt"], gla["bt"], gla["ktt"], gla["ket"], gla["cols"],
                               st_scr[rd], gla)
    n_first = n_chunks + 1
    _trace_interleaved([bind(prob, n) for n in range(n_blocks)], [proj_ga, proj_gb, embed] + gla_jobs[:n_first])
    _trace_interleaved([bind(value, n) for n in range(n_blocks)], gla_jobs[n_first:])
    st_scr[wr] = gla["state"]
    st_ref[...] = gla["state"]
    ob, pe, ga, gb = gla["ob"], v["pe"], v["ga"], v["gb"]

    def finish(ob):
        for r0 in range(0, tile, FINISH_ROWS):
            rs = slice(r0, r0 + FINISH_ROWS)
            oa = jnp.concatenate(
                [jnp.concatenate([sl for j in range(A_KV_HEADS) for sl in slabs[c * A_KV_HEADS + j]], axis=1)
                 for c in range(r0 // CHUNK, (r0 + FINISH_ROWS) // CHUNK)], axis=0)
            h1 = h[rs] + _dot((oa * _silu(ga[rs])).astype(BF16), wout_ref[0:A_WIDTH, :])
            obn = jnp.concatenate(
                [_rms(ob[rs, hh * B_VAL_DIM:(hh + 1) * B_VAL_DIM], glag_ref[...]) for hh in range(B_HEADS)], axis=1)
            h2 = h1 + _dot((obn * _silu(gb[rs])).astype(BF16), wout_ref[A_WIDTH:, :])
            gate = _sigmoid(_dot(_rms(h2, peg_ref[...]).astype(BF16), wpg_ref[...]))
            ho_ref[rs, :] = h2 + gate * pe[rs]

    finish(ob)

    strongest = gla["cols"][0]
    for c in range(1, n_chunks):
        strongest = jnp.minimum(strongest, gla["cols"][c])

    @pl.when(jnp.min(strongest) < -GLA_FAST_MAX_DECAY)
    def _():
        finish(gla["wide_range_ob"]())


def _sample_kernel(sink_ref, h_ref, p_ref, rope_ref, ck_ref, cv_ref, s0_ref, ng_ref, win_ref, wab_ref, gqk_ref, wgu_ref,
                   bg_ref, glag_ref, wout_ref, peg_ref, wpe_ref, wpg_ref,
                   ho_ref, nk_ref, nv_ref, st_ref, *, layer, dec_seq):
    nb = SAMPLE_BATCH_TILE
    h = h_ref[...]
    xn = _rms(h, ng_ref[...]).astype(BF16)
    lo_half = lax.broadcasted_iota(jnp.int32, (1, LANES), 1) < A_HEAD_DIM

    q, k, v = _project_a(xn, rope_ref[...].T, win_ref, gqk_ref)
    nk_ref[...] = k
    nv_ref[...] = v
    kd_new, vx_new = _kv_operands(k, v, lo_half)
    rows_of = [slice(bi * dec_seq, (bi + 1) * dec_seq) for bi in range(nb)]
    blocks = [(bi, j) for bi in range(nb) for j in range(A_KV_HEADS)]
    old = [_kv_operands(ck_ref[bi], cv_ref[bi], lo_half) for bi in range(nb)]
    scores = [_attn_scores(q[rows_of[bi]], j,
                           jnp.concatenate([old[bi][0][j], kd_new[j][rows_of[bi]]], axis=0), lo_half)
              for bi, j in blocks]
    probs = [_attn_probs(s, j, None, sink_ref, layer) for (bi, j), s in zip(blocks, scores)]
    slabs = [_attn_values(p, de, jnp.concatenate([old[bi][1][j], vx_new[j][rows_of[bi]]], axis=0), lo_half)
             for (bi, j), (p, de) in zip(blocks, probs)]
    oa = jnp.concatenate(
        [jnp.concatenate([sl for j in range(A_KV_HEADS) for sl in slabs[bi * A_KV_HEADS + j]], axis=1)
         for bi in range(nb)], axis=0)

    qt, kt, ke, vb, g_all = _gla_inputs(xn, win_ref, wab_ref, wgu_ref, bg_ref, dec_seq)
    head_of_lane = lax.broadcasted_iota(jnp.int32, (1, B_QK_WIDTH), 1) // B_KEY_DIM
    head_mask = [head_of_lane == hh for hh in range(B_HEADS)]
    causal = (lax.broadcasted_iota(jnp.int32, (dec_seq, dec_seq), 0)
              >= lax.broadcasted_iota(jnp.int32, (dec_seq, dec_seq), 1))
    ones_rhs = jnp.ones((dec_seq, LANES), BF16)
    vb = vb.astype(BF16)
    g_hi, g_lo = _split_bf16(g_all)
    q_stacks = [_head_stack(qt[rs], head_mask) for rs in rows_of]
    atts = [_dot_nt(q_stacks[bi], kt[rows_of[bi]].astype(BF16)) for bi in range(nb)]
    inters = [_dot(q_stacks[bi], s0_ref[bi].astype(BF16)) for bi in range(nb)]
    totals = [_dot_tn(g_hi[rs], ones_rhs) + _dot_tn(g_lo[rs], ones_rhs) for rs in rows_of]
    upd_fulls = [_dot_tn(ke[rs].astype(BF16), vb[rs]) for rs in rows_of]
    atts = [jnp.concatenate([jnp.where(causal, a[hh * dec_seq:(hh + 1) * dec_seq], 0.0)
                             for hh in range(B_HEADS)], axis=0).astype(BF16) for a in atts]
    intras = [_dot(atts[bi], vb[rows_of[bi]]) for bi in range(nb)]
    ob = jnp.concatenate(
        [jnp.concatenate([intras[bi][hh * dec_seq:(hh + 1) * dec_seq, hh * B_VAL_DIM:(hh + 1) * B_VAL_DIM]
                          + inters[bi][hh * dec_seq:(hh + 1) * dec_seq] for hh in range(B_HEADS)], axis=1)
         for bi in range(nb)], axis=0)
    for bi in range(nb):
        upd = jnp.concatenate(
            [upd_fulls[bi][hh * B_KEY_DIM:(hh + 1) * B_KEY_DIM, hh * B_VAL_DIM:(hh + 1) * B_VAL_DIM]
             for hh in range(B_HEADS)], axis=0)
        st_ref[bi] = s0_ref[bi] * jnp.exp(totals[bi]) + upd

    ho_ref[...] = _finish_layer(h, oa, ob, xn, p_ref[...], win_ref, glag_ref, wout_ref, peg_ref,
                                wpe_ref, wpg_ref)


def _cast_kernel(x_ref, o_ref):
    o_ref[...] = x_ref[:, :o_ref.shape[1]].astype(o_ref.dtype)


def _to_bf16(w, cols=None):
    depth, rows, width = w.shape
    cols = width if cols is None else cols
    block_rows = min(rows, CAST_ROWS)
    assert rows % block_rows == 0 and cols % LANES == 0
    return pl.pallas_call(
        _cast_kernel,
        grid=(depth, rows // block_rows),
        in_specs=[pl.BlockSpec((None, block_rows, width), lambda l, r: (l, r, 0))],
        out_specs=pl.BlockSpec((None, block_rows, cols), lambda l, r: (l, r, 0)),
        out_shape=jax.ShapeDtypeStruct((depth, rows, cols), BF16),
        compiler_params=pltpu.CompilerParams(dimension_semantics=("arbitrary", "arbitrary")),
        name="cast_bf16",
    )(w)


def _rope_table(pos):
    half = ROT_DIM // 2
    inv = jnp.power(jnp.float32(ROPE_THETA), -jnp.arange(half, dtype=F32) * (2.0 / ROT_DIM))
    ang = inv[:, None] * pos.astype(F32)[None, :]
    cos, sin = jnp.cos(ang), jnp.sin(ang)
    head = jnp.concatenate([cos, cos, -sin, sin, jnp.zeros((A_HEAD_DIM - 2 * ROT_DIM, pos.shape[0]), F32)], axis=0)
    return jnp.concatenate([head, head], axis=0)


def _layer_spec(layer, shape):
    return pl.BlockSpec((None,) + shape, lambda i: (layer,) + (0,) * len(shape))


def _weight_specs(layer):
    return [
        _layer_spec(layer, (1, D_MODEL)),
        pl.BlockSpec((None, D_MODEL, C_AB), lambda i: (layer, 0, 0)),
        _layer_spec(layer, (D_MODEL, LANES)),
        _layer_spec(layer, (1, C_VA)),
        _layer_spec(layer, (LANES, B_QK_WIDTH)),
        _layer_spec(layer, (1, B_QK_WIDTH)),
        _layer_spec(layer, (1, B_VAL_DIM)),
        _layer_spec(layer, (A_WIDTH + B_WIDTH, D_MODEL)),
        _layer_spec(layer, (1, D_MODEL)),
        _layer_spec(layer, (D_PLE, D_MODEL)),
        _layer_spec(layer, (D_MODEL, D_MODEL)),
    ]


def _prompt_layer(layer, h, p, rope, sinks, weights, weights_t):
    seq = h.shape[0]
    tile = PROMPT_TILE
    assert seq % tile == 0 and tile % GLA_SPAN == 0 and GLA_SPAN % (2 * CHUNK) == 0 and tile >= WINDOW
    row = lambda w: pl.BlockSpec((tile, w), lambda i: (i, 0))
    const = lambda shape: pl.BlockSpec(shape, lambda i: (0,) * len(shape))
    return pl.pallas_call(
        functools.partial(_prompt_kernel, layer=layer),
        grid=(seq // tile,),
        in_specs=[pl.BlockSpec(memory_space=pltpu.SMEM), row(D_MODEL),
                  pl.BlockSpec((None, tile, D_PLE), lambda i: (layer, i, 0)),
                  pl.BlockSpec((LANES, tile), lambda i: (0, i))]
        + _weight_specs(layer)
        + [_layer_spec(layer, (B_QK_WIDTH, D_MODEL)),
           _layer_spec(layer, (GATE_RANK, D_MODEL)),
           _layer_spec(layer, (B_QK_WIDTH, GATE_RANK)),
           _layer_spec(layer, (B_QK_WIDTH, 1))],
        out_specs=[row(D_MODEL), const((WINDOW, A_KV_WIDTH)), const((WINDOW, A_KV_WIDTH)),
                   const((B_QK_WIDTH, B_VAL_DIM))],
        out_shape=[jax.ShapeDtypeStruct((seq, D_MODEL), F32),
                   jax.ShapeDtypeStruct((WINDOW, A_KV_WIDTH), F32),
                   jax.ShapeDtypeStruct((WINDOW, A_KV_WIDTH), F32),
                   jax.ShapeDtypeStruct((B_QK_WIDTH, B_VAL_DIM), F32)],
        scratch_shapes=[pltpu.VMEM((2, A_KV_HEADS, WINDOW, LANES), BF16),
                        pltpu.VMEM((2, A_KV_HEADS, WINDOW, 2 * LANES), BF16),
                        pltpu.VMEM((2, B_QK_WIDTH, B_VAL_DIM), F32)],
        compiler_params=pltpu.CompilerParams(dimension_semantics=("arbitrary",),
                                             vmem_limit_bytes=VMEM_LIMIT_BYTES),
        name="prompt_layer",
    )(sinks, h, p, rope, *weights, *weights_t)


def _sample_layer(layer, h, p, rope, ck, cv, s0, sinks, weights, dec_seq):
    n_seq = ck.shape[1]
    nb = SAMPLE_BATCH_TILE
    assert n_seq % nb == 0
    rows = nb * dec_seq
    row = lambda w: pl.BlockSpec((rows, w), lambda i: (i, 0))
    per_seq_in = lambda a, b: pl.BlockSpec((None, nb, a, b), lambda i: (layer, i, 0, 0))
    return pl.pallas_call(
        functools.partial(_sample_kernel, layer=layer, dec_seq=dec_seq),
        grid=(n_seq // nb,),
        in_specs=[pl.BlockSpec(memory_space=pltpu.SMEM), row(D_MODEL),
                  pl.BlockSpec((None, rows, D_PLE), lambda i: (layer, i, 0)),
                  pl.BlockSpec((LANES, rows), lambda i: (0, 0)),
                  per_seq_in(WINDOW, A_KV_WIDTH), per_seq_in(WINDOW, A_KV_WIDTH),
                  per_seq_in(B_QK_WIDTH, B_VAL_DIM)]
        + _weight_specs(layer),
        out_specs=[row(D_MODEL), row(A_KV_WIDTH), row(A_KV_WIDTH),
                   pl.BlockSpec((nb, B_QK_WIDTH, B_VAL_DIM), lambda i: (i, 0, 0))],
        out_shape=[jax.ShapeDtypeStruct((n_seq * dec_seq, D_MODEL), F32),
                   jax.ShapeDtypeStruct((n_seq * dec_seq, A_KV_WIDTH), F32),
                   jax.ShapeDtypeStruct((n_seq * dec_seq, A_KV_WIDTH), F32),
                   jax.ShapeDtypeStruct((n_seq, B_QK_WIDTH, B_VAL_DIM), F32)],
        compiler_params=pltpu.CompilerParams(dimension_semantics=("arbitrary",),
                                             vmem_limit_bytes=VMEM_LIMIT_BYTES),
        name="sample_layer",
    )(sinks, h, p, rope, ck, cv, s0, *weights)


def kernel(x_prompt, x_sample, cache_k, cache_v, state_gla, p_prompt, p_sample, norm_g, w_in, q_norm_g, k_norm_g,
           sinks, w_gate_up, b_gate, gla_norm_g, w_out, pe_norm_g, w_pe, w_pg):
    batch, seq, _ = x_prompt.shape
    n_seq, dec_seq, _ = x_sample.shape
    assert batch == 1 and cache_k.shape[2] == WINDOW

    rope_p = _rope_table(jnp.arange(seq, dtype=jnp.int32))
    rope_s = jnp.tile(_rope_table(PAST_LEN + jnp.arange(dec_seq, dtype=jnp.int32)), (1, SAMPLE_BATCH_TILE))

    w_ab = w_in[:, :, C_AB:IN_COLS]
    wab_b = jnp.pad(w_ab, ((0, 0), (0, 0), (0, LANES - GATE_RANK))).astype(BF16)
    wgu_b = jnp.pad(w_gate_up, ((0, 0), (0, LANES - GATE_RANK), (0, 0))).astype(BF16)
    gqk = jnp.concatenate([jnp.tile(q_norm_g, (1, A_HEADS)) * (A_HEAD_DIM ** -0.5),
                           jnp.tile(k_norm_g, (1, A_KV_HEADS))], axis=1)
    vec = lambda a: a[:, None, :]
    weights = (vec(norm_g), _to_bf16(w_in, C_AB), wab_b, vec(gqk), wgu_b, vec(b_gate), vec(gla_norm_g),
               _to_bf16(w_out), vec(pe_norm_g), _to_bf16(w_pe), _to_bf16(w_pg))
    weights_t = (jnp.swapaxes(w_in[:, :, C_KB:C_VB], 1, 2).astype(BF16),
                 jnp.swapaxes(w_ab, 1, 2).astype(BF16),
                 jnp.swapaxes(w_gate_up, 1, 2).astype(BF16),
                 b_gate[:, :, None])

    hp = x_prompt.reshape(seq, D_MODEL)
    hs = x_sample.reshape(n_seq * dec_seq, D_MODEL)
    ck = cache_k.reshape(DEPTH, n_seq, WINDOW, A_KV_WIDTH)
    cv = cache_v.reshape(DEPTH, n_seq, WINDOW, A_KV_WIDTH)
    s0 = state_gla.reshape(DEPTH, n_seq, B_QK_WIDTH, B_VAL_DIM)
    pp = p_prompt.reshape(DEPTH, seq, D_PLE)
    ps = p_sample.reshape(DEPTH, n_seq * dec_seq, D_PLE)

    kp_l, vp_l, sp_l, ks_l, vs_l, ss_l = [], [], [], [], [], []
    for layer in range(DEPTH):
        hp, kn, vn, sn = _prompt_layer(layer, hp, pp, rope_p, sinks, weights, weights_t)
        kp_l.append(kn); vp_l.append(vn); sp_l.append(sn)
        hs, kn, vn, sn = _sample_layer(layer, hs, ps, rope_s, ck, cv, s0, sinks, weights, dec_seq)
        ks_l.append(kn); vs_l.append(vn); ss_l.append(sn)

    return (hp.reshape(batch, seq, D_MODEL),
            hs.reshape(n_seq, dec_seq, D_MODEL),
            jnp.stack(kp_l).reshape(DEPTH, batch, WINDOW, A_KV_HEADS, A_HEAD_DIM),
            jnp.stack(vp_l).reshape(DEPTH, batch, WINDOW, A_KV_HEADS, A_HEAD_DIM),
            jnp.stack(sp_l).reshape(DEPTH, batch, B_HEADS, B_KEY_DIM, B_VAL_DIM),
            jnp.stack(ks_l).reshape(DEPTH, n_seq, dec_seq, A_KV_HEADS, A_HEAD_DIM),
            jnp.stack(vs_l).reshape(DEPTH, n_seq, dec_seq, A_KV_HEADS, A_HEAD_DIM),
            jnp.stack(ss_l).reshape(DEPTH, n_seq, B_HEADS, B_KEY_DIM, B_VAL_DIM))
```

```python
import functools

import jax
import jax.numpy as jnp
from jax import lax
from jax.experimental import pallas as pl
from jax.experimental.pallas import tpu as pltpu

F32 = jnp.float32
BF16 = jnp.bfloat16

D_MODEL = 1024
DEPTH = 4
PAST_LEN = 1024
CHUNK = 64
D_PLE = 256
A_HEADS = 8
A_KV_HEADS = 2
A_HEAD_DIM = 64
A_GROUP = A_HEADS // A_KV_HEADS
A_WIDTH = A_HEADS * A_HEAD_DIM
A_KV_WIDTH = A_KV_HEADS * A_HEAD_DIM
WINDOW = 128
ROT_DIM = A_HEAD_DIM // 4
ROPE_THETA = 500000.0
B_HEADS = 4
B_KEY_DIM = 64
B_VAL_DIM = 128
B_QK_WIDTH = B_HEADS * B_KEY_DIM
B_WIDTH = B_HEADS * B_VAL_DIM
GATE_RANK = 16
GATE_TAU = 16.0
NEG_INF = -1e30
EPS = 1e-6

LANES = 128

C_QA = 0
C_KA = C_QA + A_WIDTH
C_VA = C_KA + A_KV_WIDTH
C_GA = C_VA + A_KV_WIDTH
C_QB = C_GA + A_WIDTH
C_KB = C_QB + B_QK_WIDTH
C_VB = C_KB + B_QK_WIDTH
C_GB = C_VB + B_WIDTH
C_AB = C_GB + B_WIDTH
IN_COLS = C_AB + GATE_RANK

PROMPT_TILE = 512
GLA_SPAN = 256
GLA_SUB = 16
GLA_FAST_MAX_DECAY = 60.0
FINISH_ROWS = 256
SAMPLE_BATCH_TILE = 16
VMEM_LIMIT_BYTES = 56 * 1024 * 1024


def _dot(a, b):
    return jnp.dot(a, b, preferred_element_type=F32)


def _dot_nt(a, b):
    return lax.dot_general(a, b, (((1,), (1,)), ((), ())), preferred_element_type=F32)


def _dot_tn(a, b):
    return lax.dot_general(a, b, (((0,), (0,)), ((), ())), preferred_element_type=F32)


def _proj(xn, wint_ref, c0, c1):
    return _dot_nt(xn, wint_ref[c0:c1, :])


def _split_bf16(x):
    hi = x.astype(BF16)
    lo = (x - hi.astype(F32)).astype(BF16)
    return hi, lo


def _rms(x, g):
    ms = jnp.mean(x * x, axis=-1, keepdims=True)
    return x * lax.rsqrt(ms + EPS) * g


def _sigmoid(x):
    return 0.5 + 0.5 * jnp.tanh(0.5 * x)


def _silu(x):
    return x * _sigmoid(x)


def _log_sigmoid(x):
    return jnp.minimum(x, 0.0) - jnp.log(1.0 + jnp.exp(-jnp.abs(x)))


def _block_mask(n, block, lower):
    r = lax.broadcasted_iota(jnp.int32, (n, n), 0)
    c = lax.broadcasted_iota(jnp.int32, (n, n), 1)
    m = (r // block) == (c // block)
    if lower:
        m = m & (r >= c)
    return jnp.where(m, 1.0, 0.0).astype(BF16)


def _project_a(xn, tab, win_ref, gqk_ref):
    return _norm_rope(_proj(xn, win_ref, C_QA, C_GA), tab, gqk_ref)


def _norm_rope(qkv, tab, gqk_ref):
    qk, v = qkv[:, :C_VA], qkv[:, C_VA:]
    seg = _block_mask(256, A_HEAD_DIM, lower=False)
    hi, lo = _split_bf16(qk * qk)
    parts = []
    for c0 in range(0, C_VA, 256):
        c1 = min(c0 + 256, C_VA)
        sg = seg[: c1 - c0, : c1 - c0]
        parts.append(_dot(hi[:, c0:c1], sg) + _dot(lo[:, c0:c1], sg))
    ss = jnp.concatenate(parts, axis=1)
    qkn = qk * lax.rsqrt(ss * (1.0 / A_HEAD_DIM) + EPS) * gqk_ref[...]
    head_lane = lax.broadcasted_iota(jnp.int32, (1, LANES), 1) % A_HEAD_DIM
    cos = jnp.where(head_lane < ROT_DIM, tab, 1.0)
    sin = jnp.where(head_lane < ROT_DIM, pltpu.roll(tab, LANES - ROT_DIM, 1), 0.0)
    first_half = head_lane < ROT_DIM // 2
    slabs = []
    for s in range(C_VA // LANES):
        x = qkn[:, s * LANES:(s + 1) * LANES]
        up = pltpu.roll(x, LANES - ROT_DIM // 2, 1)
        dn = pltpu.roll(x, ROT_DIM // 2, 1)
        slabs.append(x * cos + jnp.where(first_half, up, dn) * sin)
    q = jnp.concatenate(slabs[:A_WIDTH // LANES], axis=1)
    k = slabs[A_WIDTH // LANES]
    return q, k, v


def _kv_operands(k, v, lo_half):
    ksw = pltpu.roll(k, A_HEAD_DIM, 1)
    vsw = pltpu.roll(v, A_HEAD_DIM, 1)
    one = jnp.ones_like(v)
    kd = (jnp.where(lo_half, k, ksw).astype(BF16), jnp.where(lo_half, ksw, k).astype(BF16))
    vx = (jnp.concatenate([jnp.where(lo_half, v, one), jnp.where(lo_half, one, vsw)], axis=1).astype(BF16),
          jnp.concatenate([jnp.where(lo_half, vsw, one), jnp.where(lo_half, one, v)], axis=1).astype(BF16))
    return kd, vx


def _attn_scores(q_blk, j, kd, lo_half):
    a = q_blk[:, (2 * j) * LANES:(2 * j + 1) * LANES]
    b = q_blk[:, (2 * j + 1) * LANES:(2 * j + 2) * LANES]
    z = jnp.zeros_like(a)
    stack = jnp.concatenate([jnp.where(lo_half, a, z), jnp.where(lo_half, z, a),
                             jnp.where(lo_half, b, z), jnp.where(lo_half, z, b)], axis=0).astype(BF16)
    return _dot_nt(stack, kd)


def _attn_probs(s, j, bias, sink_ref, layer):
    rows = s.shape[0] // A_GROUP
    if bias is not None:
        s = s + bias
    sk = jnp.concatenate([jnp.full((rows, 1), sink_ref[layer, A_GROUP * j + g], F32) for g in range(A_GROUP)],
                         axis=0)
    m = jnp.maximum(jnp.max(s, axis=-1, keepdims=True), sk)
    return jnp.exp(s - m).astype(BF16), jnp.exp(sk - m)


def _attn_values(p, de, vx, lo_half):
    rows = p.shape[0] // A_GROUP
    r = _dot(p, vx)
    r_lo, r_hi = r[:, :LANES], r[:, LANES:]

    def slab(g0):
        e = slice(g0 * rows, (g0 + 1) * rows)
        o = slice((g0 + 1) * rows, (g0 + 2) * rows)
        return jnp.where(lo_half, r_lo[e] / (r_hi[e] + de[e]), r_hi[o] / (r_lo[o] + de[o]))

    return slab(0), slab(2)


def _attn_block(q_blk, j, kd, vx, bias, sink_ref, layer, lo_half):
    p, de = _attn_probs(_attn_scores(q_blk, j, kd, lo_half), j, bias, sink_ref, layer)
    return _attn_values(p, de, vx, lo_half)


def _gla_inputs(xn, win_ref, wgu_ref, bg_ref, block):
    rows = xn.shape[0]
    qb = _proj(xn, win_ref, C_QB, C_KB)
    kb = _proj(xn, win_ref, C_KB, C_VB)
    vb = _proj(xn, win_ref, C_VB, C_GB)
    ab = _proj(xn, win_ref, C_AB, IN_COLS)
    logg = _log_sigmoid(_dot(ab.astype(BF16), wgu_ref[...]) + bg_ref[...]) * (1.0 / GATE_TAU)
    g_hi, g_lo = _split_bf16(logg)
    tri = _block_mask(rows, block, lower=True)
    ones = _block_mask(rows, block, lower=False)
    b = _dot(tri, g_hi) + _dot(tri, g_lo)
    b_last = _dot(ones, g_hi) + _dot(ones, g_lo)
    qt = qb * (jnp.exp(b) * (B_KEY_DIM ** -0.5))
    kt = kb * jnp.exp(-b)
    ke = kb * jnp.exp(b_last - b)
    return qt, kt, ke, vb, logg


def _head_stack(x, head_mask):
    z = jnp.zeros_like(x)
    return jnp.concatenate([jnp.where(head_mask[h], x, z) for h in range(B_HEADS)], axis=0).astype(BF16)


def _finish_layer(h, oa, ob, xn, p, win_ref, glag_ref, wout_ref, peg_ref, wpe_ref, wpg_ref):
    ga = _proj(xn, win_ref, C_GA, C_QB)
    gb = _proj(xn, win_ref, C_GB, C_AB)
    obn = jnp.concatenate(
        [_rms(ob[:, hh * B_VAL_DIM:(hh + 1) * B_VAL_DIM], glag_ref[...]) for hh in range(B_HEADS)], axis=1)
    mix = jnp.concatenate([oa * _silu(ga), obn * _silu(gb)], axis=1).astype(BF16)
    h1 = h + _dot(mix, wout_ref[...])
    gate = _sigmoid(_dot(_rms(h1, peg_ref[...]).astype(BF16), wpg_ref[...]))
    return h1 + gate * _dot(p.astype(BF16), wpe_ref[...])


def _gla_decays(kbt, abt, wgut_ref, bgc_ref):
    tile = kbt.shape[1]
    loggt = _log_sigmoid(_dot(wgut_ref[...], abt) + bgc_ref[...]) * (1.0 / GATE_TAU)
    g_hi, g_lo = _split_bf16(loggt)
    r = lax.broadcasted_iota(jnp.int32, (GLA_SPAN, GLA_SPAN), 0)
    c = lax.broadcasted_iota(jnp.int32, (GLA_SPAN, GLA_SPAN), 1)
    upper = jnp.where(((r // CHUNK) == (c // CHUNK)) & (r <= c), 1.0, 0.0).astype(BF16)
    bt = jnp.concatenate(
        [_dot(g_hi[:, s0:s0 + GLA_SPAN], upper) + _dot(g_lo[:, s0:s0 + GLA_SPAN], upper)
         for s0 in range(0, tile, GLA_SPAN)], axis=1)
    ktt = (kbt * jnp.exp(-bt)).astype(BF16)
    pair = 2 * CHUNK
    cols = [bt[:, (c + 1) * CHUNK - 1:(c + 1) * CHUNK] for c in range(tile // CHUNK)]
    lane = lax.broadcasted_iota(jnp.int32, (1, pair), 1)
    ket = jnp.concatenate(
        [kbt[:, pc * pair:(pc + 1) * pair]
         * jnp.exp(jnp.where(lane < CHUNK, cols[2 * pc], cols[2 * pc + 1]) - bt[:, pc * pair:(pc + 1) * pair])
         for pc in range(tile // pair)], axis=1)
    return bt, ktt, ket, cols


def _gla_chunk_jobs(qb, qt, vb, kbt, bt, ktt, ket, cols, stk, out):
    tile = qt.shape[0]
    n_chunks = tile // CHUNK
    pair = 2 * CHUNK
    head_of_lane = lax.broadcasted_iota(jnp.int32, (1, B_QK_WIDTH), 1) // B_KEY_DIM
    head_mask = [head_of_lane == hh for hh in range(B_HEADS)]
    lane = lax.broadcasted_iota(jnp.int32, (1, pair), 1)
    t_row = lax.broadcasted_iota(jnp.int32, (CHUNK, pair), 0)
    s_col = lax.broadcasted_iota(jnp.int32, (CHUNK, pair), 1)
    own = [lane < CHUNK, lane >= CHUNK]
    causal = [s_col <= t_row, (s_col >= CHUNK) & (s_col - CHUNK <= t_row)]
    v_heads = [[vb[(c // 2) * pair:(c // 2 + 1) * pair, hh * B_VAL_DIM:(hh + 1) * B_VAL_DIM]
                for hh in range(B_HEADS)] for c in range(n_chunks)]
    updates, states, results, ob_rows = {}, {}, {}, {}

    def update(c):
        ps = slice((c // 2) * pair, (c // 2 + 1) * pair)
        ket_c = jnp.where(own[c % 2], ket[:, ps], 0.0).astype(BF16)
        updates[c] = jnp.concatenate(
            [_dot(ket_c[hh * B_KEY_DIM:(hh + 1) * B_KEY_DIM], v_heads[c][hh]) for hh in range(B_HEADS)], axis=0)

    def chain():
        s = stk
        for c in range(n_chunks):
            states[c] = s
            s = s * jnp.exp(cols[c]) + updates[c]
        out["state"] = s

    def result(c):
        ps = slice((c // 2) * pair, (c // 2 + 1) * pair)
        q_stack = _head_stack(qt[c * CHUNK:(c + 1) * CHUNK], head_mask)
        results[c] = _dot(q_stack, jnp.concatenate([states[c].astype(BF16), ktt[:, ps]], axis=1))

    def chunk_output(c, res):
        outs = []
        for hh in range(B_HEADS):
            hr = slice(hh * CHUNK, (hh + 1) * CHUNK)
            a_h = jnp.where(causal[c % 2], res[hr, B_VAL_DIM:], 0.0).astype(BF16)
            outs.append(_dot(a_h, v_heads[c][hh]) + res[hr, :B_VAL_DIM])
        return jnp.concatenate(outs, axis=1)

    def output(c):
        ob_rows[c] = chunk_output(c, results.pop(c))
        if c == n_chunks - 1:
            out["ob"] = jnp.concatenate([ob_rows[cc] for cc in range(n_chunks)], axis=0)

    def wide_range_ob():
        n_sub = CHUNK // GLA_SUB
        rows = []
        for c in range(n_chunks):
            ps = slice((c // 2) * pair, (c // 2 + 1) * pair)
            first = (c % 2) * CHUNK
            bt_c = bt[:, c * CHUNK:(c + 1) * CHUNK]
            res = []
            for i in range(n_sub):
                r0 = c * CHUNK + i * GLA_SUB
                ref = bt_c[:, i * GLA_SUB - 1:i * GLA_SUB] if i else jnp.zeros((B_QK_WIDTH, 1), F32)
                rel_q = (bt_c[:, i * GLA_SUB:(i + 1) * GLA_SUB] - ref).T
                q_sub = qb[r0:r0 + GLA_SUB] * (jnp.exp(rel_q) * (B_KEY_DIM ** -0.5))
                visible = (lane >= first) & (lane < first + (i + 1) * GLA_SUB)
                keys = kbt[:, ps] * jnp.exp(jnp.where(visible, ref - bt[:, ps], NEG_INF))
                state = (states[c] * jnp.exp(ref)).astype(BF16)
                res.append(_dot(_head_stack(q_sub, head_mask),
                                jnp.concatenate([state, keys.astype(BF16)], axis=1)))
            rows.append(chunk_output(c, jnp.concatenate(
                [res[i][hh * GLA_SUB:(hh + 1) * GLA_SUB] for hh in range(B_HEADS) for i in range(n_sub)], axis=0)))
        return jnp.concatenate(rows, axis=0)

    out["wide_range_ob"] = wide_range_ob

    bind = lambda f, c: (lambda: f(c))
    return ([bind(update, c) for c in range(n_chunks)] + [chain]
            + [bind(result, c) for c in range(n_chunks)] + [bind(output, c) for c in range(n_chunks)])


def _trace_interleaved(jobs_a, jobs_b):
    na, nb = len(jobs_a), len(jobs_b)
    ia = ib = 0
    while ia < na or ib < nb:
        if ib >= nb or (ia < na and ia * nb <= ib * na):
            jobs_a[ia]()
            ia += 1
        else:
            jobs_b[ib]()
            ib += 1


def _prompt_kernel(sink_ref, h_ref, p_ref, rope_ref, ng_ref, win_ref, gqk_ref, wgu_ref, bg_ref, glag_ref,
                   wout_ref, peg_ref, wpe_ref, wpg_ref, wgut_ref, bgc_ref,
                   ho_ref, nk_ref, nv_ref, st_ref,
                   kd_scr, vx_scr, st_scr, *, layer):
    tile = PROMPT_TILE
    n_chunks = tile // CHUNK
    band = WINDOW + CHUNK
    i = pl.program_id(0)
    rd = (i + 1) % 2
    wr = i % 2

    @pl.when(i == 0)
    def _():
        kd_scr[rd] = jnp.zeros(kd_scr.shape[1:], BF16)
        vx_scr[rd] = jnp.zeros(vx_scr.shape[1:], BF16)
        st_scr[rd] = jnp.zeros(st_scr.shape[1:], F32)

    h = h_ref[...]
    xn = _rms(h, ng_ref[...]).astype(BF16)
    lo_half = lax.broadcasted_iota(jnp.int32, (1, LANES), 1) < A_HEAD_DIM
    key_blk = lax.broadcasted_iota(jnp.int32, (1, band), 1) // CHUNK
    blocks = [(c, j) for c in range(n_chunks) for j in range(A_KV_HEADS)]
    n_blocks = len(blocks)
    v, scores, probs, slabs, gla = {}, {}, {}, {}, {}
    bind = lambda f, n: (lambda: f(n))

    qkv = _proj(xn, win_ref, C_QA, C_GA)

    def rope_tab():
        tab = rope_ref[...].T
        head_lane = lax.broadcasted_iota(jnp.int32, (1, LANES), 1) % A_HEAD_DIM
        v["cos"] = jnp.where(head_lane < ROT_DIM, tab, 1.0)
        v["sin"] = jnp.where(head_lane < ROT_DIM, pltpu.roll(tab, LANES - ROT_DIM, 1), 0.0)
        v["first_half"] = head_lane < ROT_DIM // 2

    def norm(g):
        width = min(256, C_VA - 256 * g)
        x = qkv[:, 256 * g:256 * g + width]
        seg = _block_mask(width, A_HEAD_DIM, lower=False)
        hi, lo = _split_bf16(x * x)
        ss = _dot(hi, seg) + _dot(lo, seg)
        xs = x * lax.rsqrt(ss * (1.0 / A_HEAD_DIM) + EPS) * gqk_ref[:, 256 * g:256 * g + width]
        for s in range(width // LANES):
            x1 = xs[:, s * LANES:(s + 1) * LANES]
            up = pltpu.roll(x1, LANES - ROT_DIM // 2, 1)
            dn = pltpu.roll(x1, ROT_DIM // 2, 1)
            v[f"slab{2 * g + s}"] = x1 * v["cos"] + jnp.where(v["first_half"], up, dn) * v["sin"]

    def kv_ops():
        k, vv = v["slab4"], qkv[:, C_VA:]
        nk_ref[...] = k[tile - WINDOW:, :]
        nv_ref[...] = vv[tile - WINDOW:, :]
        kd, vx = _kv_operands(k, vv, lo_half)
        v["kd_ext"] = [jnp.concatenate([kd_scr[rd, j], kd[j]], axis=0) for j in range(A_KV_HEADS)]
        v["vx_ext"] = [jnp.concatenate([vx_scr[rd, j], vx[j]], axis=0) for j in range(A_KV_HEADS)]
        for j in range(A_KV_HEADS):
            kd_scr[wr, j] = kd[j][tile - WINDOW:, :]
            vx_scr[wr, j] = vx[j][tile - WINDOW:, :]

    def proj_k():
        v["kbt"] = _dot_nt(win_ref[C_KB:C_VB, :], xn)
        v["abt"] = _dot_nt(win_ref[C_AB:IN_COLS, :], xn).astype(BF16)

    def proj_qv():
        v["qb"] = _proj(xn, win_ref, C_QB, C_KB)
        v["vb"] = _proj(xn, win_ref, C_VB, C_GB).astype(BF16)

    _trace_interleaved([rope_tab, bind(norm, 2), kv_ops, bind(norm, 0), bind(norm, 1)], [proj_k, proj_qv])

    def score(n):
        c, j = blocks[n]
        cs = slice(c * CHUNK, (c + 1) * CHUNK)
        q_blk = jnp.concatenate([v[f"slab{2 * j}"][cs], v[f"slab{2 * j + 1}"][cs]], axis=1)
        scores[n] = _attn_scores(q_blk, 0, v["kd_ext"][j][c * CHUNK:c * CHUNK + band, :], lo_half)

    def decays():
        gla["bt"], gla["ktt"], gla["ket"], gla["cols"] = _gla_decays(v["kbt"], v["abt"], wgut_ref, bgc_ref)

    def scale_q():
        gla["qt"] = v["qb"] * (jnp.exp(gla["bt"].T) * (B_KEY_DIM ** -0.5))

    _trace_interleaved([bind(score, n) for n in range(n_blocks)], [decays, scale_q])

    def prob(n):
        c, j = blocks[n]
        first_key_chunk = i * n_chunks + c - WINDOW // CHUNK
        bias = jnp.where(key_blk + first_key_chunk >= 0, 0.0, NEG_INF)
        probs[n] = _attn_probs(scores.pop(n), j, bias, sink_ref, layer)

    def value(n):
        c, j = blocks[n]
        slabs[n] = _attn_values(*probs.pop(n), v["vx_ext"][j][c * CHUNK:c * CHUNK + band, :], lo_half)

    def proj_ga():
        v["ga"] = _proj(xn, win_ref, C_GA, C_QB)

    def proj_gb():
        v["gb"] = _proj(xn, win_ref, C_GB, C_AB)

    def embed():
        v["pe"] = _dot(p_ref[...].astype(BF16), wpe_ref[...])

    gla_jobs = _gla_chunk_jobs(v["qb"], gla["qt"], v["vb"], v["kbt"], gla["bt"], gla["ktt"], gla["ket"], gla["cols"],
                               st_scr[rd], gla)
    n_first = n_chunks + 1
    _trace_interleaved([bind(prob, n) for n in range(n_blocks)], [proj_ga, proj_gb, embed] + gla_jobs[:n_first])
    _trace_interleaved([bind(value, n) for n in range(n_blocks)], gla_jobs[n_first:])
    st_scr[wr] = gla["state"]
    st_ref[...] = gla["state"]
    ob, pe, ga, gb = gla["ob"], v["pe"], v["ga"], v["gb"]

    def finish(ob):
        for r0 in range(0, tile, FINISH_ROWS):
            rs = slice(r0, r0 + FINISH_ROWS)
            oa = jnp.concatenate(
                [jnp.concatenate([sl for j in range(A_KV_HEADS) for sl in slabs[c * A_KV_HEADS + j]], axis=1)
                 for c in range(r0 // CHUNK, (r0 + FINISH_ROWS) // CHUNK)], axis=0)
            h1 = h[rs] + _dot((oa * _silu(ga[rs])).astype(BF16), wout_ref[0:A_WIDTH, :])
            obn = jnp.concatenate(
                [_rms(ob[rs, hh * B_VAL_DIM:(hh + 1) * B_VAL_DIM], glag_ref[...]) for hh in range(B_HEADS)], axis=1)
            h2 = h1 + _dot((obn * _silu(gb[rs])).astype(BF16), wout_ref[A_WIDTH:, :])
            gate = _sigmoid(_dot(_rms(h2, peg_ref[...]).astype(BF16), wpg_ref[...]))
            ho_ref[rs, :] = h2 + gate * pe[rs]

    finish(ob)

    strongest = gla["cols"][0]
    for c in range(1, n_chunks):
        strongest = jnp.minimum(strongest, gla["cols"][c])

    @pl.when(jnp.min(strongest) < -GLA_FAST_MAX_DECAY)
    def _():
        finish(gla["wide_range_ob"]())


def _sample_kernel(sink_ref, h_ref, p_ref, rope_ref, ck_ref, cv_ref, s0_ref, ng_ref, win_ref, gqk_ref, wgu_ref,
                   bg_ref, glag_ref, wout_ref, peg_ref, wpe_ref, wpg_ref,
                   ho_ref, nk_ref, nv_ref, st_ref, *, layer, dec_seq):
    nb = SAMPLE_BATCH_TILE
    h = h_ref[...]
    xn = _rms(h, ng_ref[...]).astype(BF16)
    lo_half = lax.broadcasted_iota(jnp.int32, (1, LANES), 1) < A_HEAD_DIM

    q, k, v = _project_a(xn, rope_ref[...].T, win_ref, gqk_ref)
    nk_ref[...] = k
    nv_ref[...] = v
    kd_new, vx_new = _kv_operands(k, v, lo_half)
    rows_of = [slice(bi * dec_seq, (bi + 1) * dec_seq) for bi in range(nb)]
    blocks = [(bi, j) for bi in range(nb) for j in range(A_KV_HEADS)]
    old = [_kv_operands(ck_ref[bi], cv_ref[bi], lo_half) for bi in range(nb)]
    scores = [_attn_scores(q[rows_of[bi]], j,
                           jnp.concatenate([old[bi][0][j], kd_new[j][rows_of[bi]]], axis=0), lo_half)
              for bi, j in blocks]
    probs = [_attn_probs(s, j, None, sink_ref, layer) for (bi, j), s in zip(blocks, scores)]
    slabs = [_attn_values(p, de, jnp.concatenate([old[bi][1][j], vx_new[j][rows_of[bi]]], axis=0), lo_half)
             for (bi, j), (p, de) in zip(blocks, probs)]
    oa = jnp.concatenate(
        [jnp.concatenate([sl for j in range(A_KV_HEADS) for sl in slabs[bi * A_KV_HEADS + j]], axis=1)
         for bi in range(nb)], axis=0)

    qt, kt, ke, vb, g_all = _gla_inputs(xn, win_ref, wgu_ref, bg_ref, dec_seq)
    head_of_lane = lax.broadcasted_iota(jnp.int32, (1, B_QK_WIDTH), 1) // B_KEY_DIM
    head_mask = [head_of_lane == hh for hh in range(B_HEADS)]
    causal = (lax.broadcasted_iota(jnp.int32, (dec_seq, dec_seq), 0)
              >= lax.broadcasted_iota(jnp.int32, (dec_seq, dec_seq), 1))
    ones_rhs = jnp.ones((dec_seq, LANES), BF16)
    vb = vb.astype(BF16)
    g_hi, g_lo = _split_bf16(g_all)
    q_stacks = [_head_stack(qt[rs], head_mask) for rs in rows_of]
    atts = [_dot_nt(q_stacks[bi], kt[rows_of[bi]].astype(BF16)) for bi in range(nb)]
    inters = [_dot(q_stacks[bi], s0_ref[bi].astype(BF16)) for bi in range(nb)]
    totals = [_dot_tn(g_hi[rs], ones_rhs) + _dot_tn(g_lo[rs], ones_rhs) for rs in rows_of]
    upd_fulls = [_dot_tn(ke[rs].astype(BF16), vb[rs]) for rs in rows_of]
    atts = [jnp.concatenate([jnp.where(causal, a[hh * dec_seq:(hh + 1) * dec_seq], 0.0)
                             for hh in range(B_HEADS)], axis=0).astype(BF16) for a in atts]
    intras = [_dot(atts[bi], vb[rows_of[bi]]) for bi in range(nb)]
    ob = jnp.concatenate(
        [jnp.concatenate([intras[bi][hh * dec_seq:(hh + 1) * dec_seq, hh * B_VAL_DIM:(hh + 1) * B_VAL_DIM]
                          + inters[bi][hh * dec_seq:(hh + 1) * dec_seq] for hh in range(B_HEADS)], axis=1)
         for bi in range(nb)], axis=0)
    for bi in range(nb):
        upd = jnp.concatenate(
            [upd_fulls[bi][hh * B_KEY_DIM:(hh + 1) * B_KEY_DIM, hh * B_VAL_DIM:(hh + 1) * B_VAL_DIM]
             for hh in range(B_HEADS)], axis=0)
        st_ref[bi] = s0_ref[bi] * jnp.exp(totals[bi]) + upd

    ho_ref[...] = _finish_layer(h, oa, ob, xn, p_ref[...], win_ref, glag_ref, wout_ref, peg_ref,
                                wpe_ref, wpg_ref)


def _rope_table(pos):
    half = ROT_DIM // 2
    inv = jnp.power(jnp.float32(ROPE_THETA), -jnp.arange(half, dtype=F32) * (2.0 / ROT_DIM))
    ang = inv[:, None] * pos.astype(F32)[None, :]
    cos, sin = jnp.cos(ang), jnp.sin(ang)
    head = jnp.concatenate([cos, cos, -sin, sin, jnp.zeros((A_HEAD_DIM - 2 * ROT_DIM, pos.shape[0]), F32)], axis=0)
    return jnp.concatenate([head, head], axis=0)


def _layer_spec(layer, shape):
    return pl.BlockSpec((None,) + shape, lambda i: (layer,) + (0,) * len(shape))


def _weight_specs(layer):
    return [
        _layer_spec(layer, (1, D_MODEL)),
        _layer_spec(layer, (IN_COLS, D_MODEL)),
        _layer_spec(layer, (1, C_VA)),
        _layer_spec(layer, (GATE_RANK, B_QK_WIDTH)),
        _layer_spec(layer, (1, B_QK_WIDTH)),
        _layer_spec(layer, (1, B_VAL_DIM)),
        _layer_spec(layer, (A_WIDTH + B_WIDTH, D_MODEL)),
        _layer_spec(layer, (1, D_MODEL)),
        _layer_spec(layer, (D_PLE, D_MODEL)),
        _layer_spec(layer, (D_MODEL, D_MODEL)),
    ]


def _prompt_layer(layer, h, p, rope, sinks, weights, weights_t):
    seq = h.shape[0]
    tile = PROMPT_TILE
    assert seq % tile == 0 and tile % GLA_SPAN == 0 and GLA_SPAN % (2 * CHUNK) == 0 and tile >= WINDOW
    row = lambda w: pl.BlockSpec((tile, w), lambda i: (i, 0))
    const = lambda shape: pl.BlockSpec(shape, lambda i: (0,) * len(shape))
    return pl.pallas_call(
        functools.partial(_prompt_kernel, layer=layer),
        grid=(seq // tile,),
        in_specs=[pl.BlockSpec(memory_space=pltpu.SMEM), row(D_MODEL),
                  pl.BlockSpec((None, tile, D_PLE), lambda i: (layer, i, 0)),
                  pl.BlockSpec((LANES, tile), lambda i: (0, i))]
        + _weight_specs(layer)
        + [_layer_spec(layer, (B_QK_WIDTH, GATE_RANK)),
           _layer_spec(layer, (B_QK_WIDTH, 1))],
        out_specs=[row(D_MODEL), const((WINDOW, A_KV_WIDTH)), const((WINDOW, A_KV_WIDTH)),
                   const((B_QK_WIDTH, B_VAL_DIM))],
        out_shape=[jax.ShapeDtypeStruct((seq, D_MODEL), F32),
                   jax.ShapeDtypeStruct((WINDOW, A_KV_WIDTH), F32),
                   jax.ShapeDtypeStruct((WINDOW, A_KV_WIDTH), F32),
                   jax.ShapeDtypeStruct((B_QK_WIDTH, B_VAL_DIM), F32)],
        scratch_shapes=[pltpu.VMEM((2, A_KV_HEADS, WINDOW, LANES), BF16),
                        pltpu.VMEM((2, A_KV_HEADS, WINDOW, 2 * LANES), BF16),
                        pltpu.VMEM((2, B_QK_WIDTH, B_VAL_DIM), F32)],
        compiler_params=pltpu.CompilerParams(dimension_semantics=("arbitrary",),
                                             vmem_limit_bytes=VMEM_LIMIT_BYTES),
        name="prompt_layer",
    )(sinks, h, p, rope, *weights, *weights_t)


def _sample_layer(layer, h, p, rope, ck, cv, s0, sinks, weights, dec_seq):
    n_seq = ck.shape[1]
    nb = SAMPLE_BATCH_TILE
    assert n_seq % nb == 0
    rows = nb * dec_seq
    row = lambda w: pl.BlockSpec((rows, w), lambda i: (i, 0))
    per_seq_in = lambda a, b: pl.BlockSpec((None, nb, a, b), lambda i: (layer, i, 0, 0))
    return pl.pallas_call(
        functools.partial(_sample_kernel, layer=layer, dec_seq=dec_seq),
        grid=(n_seq // nb,),
        in_specs=[pl.BlockSpec(memory_space=pltpu.SMEM), row(D_MODEL),
                  pl.BlockSpec((None, rows, D_PLE), lambda i: (layer, i, 0)),
                  pl.BlockSpec((LANES, rows), lambda i: (0, 0)),
                  per_seq_in(WINDOW, A_KV_WIDTH), per_seq_in(WINDOW, A_KV_WIDTH),
                  per_seq_in(B_QK_WIDTH, B_VAL_DIM)]
        + _weight_specs(layer),
        out_specs=[row(D_MODEL), row(A_KV_WIDTH), row(A_KV_WIDTH),
                   pl.BlockSpec((nb, B_QK_WIDTH, B_VAL_DIM), lambda i: (i, 0, 0))],
        out_shape=[jax.ShapeDtypeStruct((n_seq * dec_seq, D_MODEL), F32),
                   jax.ShapeDtypeStruct((n_seq * dec_seq, A_KV_WIDTH), F32),
                   jax.ShapeDtypeStruct((n_seq * dec_seq, A_KV_WIDTH), F32),
                   jax.ShapeDtypeStruct((n_seq, B_QK_WIDTH, B_VAL_DIM), F32)],
        compiler_params=pltpu.CompilerParams(dimension_semantics=("arbitrary",),
                                             vmem_limit_bytes=VMEM_LIMIT_BYTES),
        name="sample_layer",
    )(sinks, h, p, rope, ck, cv, s0, *weights)


def kernel(x_prompt, x_sample, cache_k, cache_v, state_gla, p_prompt, p_sample, norm_g, w_in, q_norm_g, k_norm_g,
           sinks, w_gate_up, b_gate, gla_norm_g, w_out, pe_norm_g, w_pe, w_pg):
    batch, seq, _ = x_prompt.shape
    n_seq, dec_seq, _ = x_sample.shape
    assert batch == 1 and cache_k.shape[2] == WINDOW

    rope_p = _rope_table(jnp.arange(seq, dtype=jnp.int32))
    rope_s = jnp.tile(_rope_table(PAST_LEN + jnp.arange(dec_seq, dtype=jnp.int32)), (1, SAMPLE_BATCH_TILE))

    wgu_b = w_gate_up.astype(BF16)
    gqk = jnp.concatenate([jnp.tile(q_norm_g, (1, A_HEADS)) * (A_HEAD_DIM ** -0.5),
                           jnp.tile(k_norm_g, (1, A_KV_HEADS))], axis=1)
    vec = lambda a: a[:, None, :]
    weights = (vec(norm_g), jnp.swapaxes(w_in, 1, 2).astype(BF16), vec(gqk), wgu_b, vec(b_gate), vec(gla_norm_g), w_out.astype(BF16),
               vec(pe_norm_g), w_pe.astype(BF16), w_pg.astype(BF16))
    weights_t = (jnp.swapaxes(w_gate_up, 1, 2).astype(BF16),
                 b_gate[:, :, None])

    hp = x_prompt.reshape(seq, D_MODEL)
    hs = x_sample.reshape(n_seq * dec_seq, D_MODEL)
    ck = cache_k.reshape(DEPTH, n_seq, WINDOW, A_KV_WIDTH)
    cv = cache_v.reshape(DEPTH, n_seq, WINDOW, A_KV_WIDTH)
    s0 = state_gla.reshape(DEPTH, n_seq, B_QK_WIDTH, B_VAL_DIM)
    pp = p_prompt.reshape(DEPTH, seq, D_PLE)
    ps = p_sample.reshape(DEPTH, n_seq * dec_seq, D_PLE)

    kp_l, vp_l, sp_l, ks_l, vs_l, ss_l = [], [], [], [], [], []
    for layer in range(DEPTH):
        hp, kn, vn, sn = _prompt_layer(layer, hp, pp, rope_p, sinks, weights, weights_t)
        kp_l.append(kn); vp_l.append(vn); sp_l.append(sn)
        hs, kn, vn, sn = _sample_layer(layer, hs, ps, rope_s, ck, cv, s0, sinks, weights, dec_seq)
        ks_l.append(kn); vs_l.append(vn); ss_l.append(sn)

    return (hp.reshape(batch, seq, D_MODEL),
            hs.reshape(n_seq, dec_seq, D_MODEL),
            jnp.stack(kp_l).reshape(DEPTH, batch, WINDOW, A_KV_HEADS, A_HEAD_DIM),
            jnp.stack(vp_l).reshape(DEPTH, batch, WINDOW, A_KV_HEADS, A_HEAD_DIM),
            jnp.stack(sp_l).reshape(DEPTH, batch, B_HEADS, B_KEY_DIM, B_VAL_DIM),
            jnp.stack(ks_l).reshape(DEPTH, n_seq, dec_seq, A_KV_HEADS, A_HEAD_DIM),
            jnp.stack(vs_l).reshape(DEPTH, n_seq, dec_seq, A_KV_HEADS, A_HEAD_DIM),
            jnp.stack(ss_l).reshape(DEPTH, n_seq, B_HEADS, B_KEY_DIM, B_VAL_DIM))
```

```python
import functools

import jax
import jax.numpy as jnp
from jax import lax
from jax.experimental import pallas as pl
from jax.experimental.pallas import tpu as pltpu

F32 = jnp.float32
BF16 = jnp.bfloat16

D_MODEL = 1024
DEPTH = 4
PAST_LEN = 1024
CHUNK = 64
D_PLE = 256
A_HEADS = 8
A_KV_HEADS = 2
A_HEAD_DIM = 64
A_GROUP = A_HEADS // A_KV_HEADS
A_WIDTH = A_HEADS * A_HEAD_DIM
A_KV_WIDTH = A_KV_HEADS * A_HEAD_DIM
WINDOW = 128
ROT_DIM = A_HEAD_DIM // 4
ROPE_THETA = 500000.0
B_HEADS = 4
B_KEY_DIM = 64
B_VAL_DIM = 128
B_QK_WIDTH = B_HEADS * B_KEY_DIM
B_WIDTH = B_HEADS * B_VAL_DIM
GATE_RANK = 16
GATE_TAU = 16.0
NEG_INF = -1e30
EPS = 1e-6

LANES = 128
MXU_TILE = 256

C_QA = 0
C_KA = C_QA + A_WIDTH
C_VA = C_KA + A_KV_WIDTH
C_GA = C_VA + A_KV_WIDTH
C_QB = C_GA + A_WIDTH
C_KB = C_QB + B_QK_WIDTH
C_VB = C_KB + B_QK_WIDTH
C_GB = C_VB + B_WIDTH
C_AB = C_GB + B_WIDTH
IN_COLS = C_AB + GATE_RANK

PROMPT_TILE = 512
GLA_SPAN = MXU_TILE
GLA_SUB = 16
GLA_FAST_MAX_DECAY = 60.0
FINISH_ROWS = 256
SAMPLE_BATCH_TILE = 16
VMEM_LIMIT_BYTES = 56 * 1024 * 1024


def _dot(a, b):
    return jnp.dot(a, b, preferred_element_type=F32)


def _dot_nt(a, b):
    return lax.dot_general(a, b, (((1,), (1,)), ((), ())), preferred_element_type=F32)


def _dot_tn(a, b):
    return lax.dot_general(a, b, (((0,), (0,)), ((), ())), preferred_element_type=F32)


def _proj(xn, wint_ref, c0, c1):
    return _dot_nt(xn, wint_ref[c0:c1, :])


def _split_bf16(x):
    hi = x.astype(BF16)
    lo = (x - hi.astype(F32)).astype(BF16)
    return hi, lo


def _rms(x, g):
    ms = jnp.mean(x * x, axis=-1, keepdims=True)
    return x * lax.rsqrt(ms + EPS) * g


def _sigmoid(x):
    return 0.5 + 0.5 * jnp.tanh(0.5 * x)


def _silu(x):
    return x * _sigmoid(x)


def _log_sigmoid(x):
    return jnp.minimum(x, 0.0) - jnp.log(1.0 + jnp.exp(-jnp.abs(x)))


def _block_mask(n, block, lower):
    r = lax.broadcasted_iota(jnp.int32, (n, n), 0)
    c = lax.broadcasted_iota(jnp.int32, (n, n), 1)
    m = (r // block) == (c // block)
    if lower:
        m = m & (r >= c)
    return jnp.where(m, 1.0, 0.0).astype(BF16)


def _project_a(xn, tab, win_ref, gqk_ref):
    return _norm_rope(_proj(xn, win_ref, C_QA, C_GA), tab, gqk_ref)


def _norm_rope(qkv, tab, gqk_ref):
    qk, v = qkv[:, :C_VA], qkv[:, C_VA:]
    seg = _block_mask(MXU_TILE, A_HEAD_DIM, lower=False)
    hi, lo = _split_bf16(qk * qk)
    parts = []
    for c0 in range(0, C_VA, MXU_TILE):
        c1 = min(c0 + MXU_TILE, C_VA)
        sg = seg[: c1 - c0, : c1 - c0]
        parts.append(_dot(hi[:, c0:c1], sg) + _dot(lo[:, c0:c1], sg))
    ss = jnp.concatenate(parts, axis=1)
    qkn = qk * lax.rsqrt(ss * (1.0 / A_HEAD_DIM) + EPS) * gqk_ref[...]
    head_lane = lax.broadcasted_iota(jnp.int32, (1, LANES), 1) % A_HEAD_DIM
    cos = jnp.where(head_lane < ROT_DIM, tab, 1.0)
    sin = jnp.where(head_lane < ROT_DIM, pltpu.roll(tab, LANES - ROT_DIM, 1), 0.0)
    first_half = head_lane < ROT_DIM // 2
    slabs = []
    for s in range(C_VA // LANES):
        x = qkn[:, s * LANES:(s + 1) * LANES]
        up = pltpu.roll(x, LANES - ROT_DIM // 2, 1)
        dn = pltpu.roll(x, ROT_DIM // 2, 1)
        slabs.append(x * cos + jnp.where(first_half, up, dn) * sin)
    q = jnp.concatenate(slabs[:A_WIDTH // LANES], axis=1)
    k = slabs[A_WIDTH // LANES]
    return q, k, v


def _kv_operands(k, v, lo_half):
    ksw = pltpu.roll(k, A_HEAD_DIM, 1)
    vsw = pltpu.roll(v, A_HEAD_DIM, 1)
    one = jnp.ones_like(v)
    kd = (jnp.where(lo_half, k, ksw).astype(BF16), jnp.where(lo_half, ksw, k).astype(BF16))
    vx = (jnp.concatenate([jnp.where(lo_half, v, one), jnp.where(lo_half, one, vsw)], axis=1).astype(BF16),
          jnp.concatenate([jnp.where(lo_half, vsw, one), jnp.where(lo_half, one, v)], axis=1).astype(BF16))
    return kd, vx


def _query_stack(q_blk, j, lo_half):
    a = q_blk[:, (2 * j) * LANES:(2 * j + 1) * LANES]
    b = q_blk[:, (2 * j + 1) * LANES:(2 * j + 2) * LANES]
    z = jnp.zeros_like(a)
    return jnp.concatenate([jnp.where(lo_half, a, z), jnp.where(lo_half, z, a),
                            jnp.where(lo_half, b, z), jnp.where(lo_half, z, b)], axis=0).astype(BF16)


def _attn_scores(q_blk, j, kd, lo_half):
    return _dot_nt(_query_stack(q_blk, j, lo_half), kd)


def _attn_probs(s, j, bias, sink_ref, layer):
    rows = s.shape[0] // A_GROUP
    if bias is not None:
        s = s + bias
    sk = jnp.concatenate([jnp.full((rows, 1), sink_ref[layer, A_GROUP * j + g], F32) for g in range(A_GROUP)],
                         axis=0)
    m = jnp.maximum(jnp.max(s, axis=-1, keepdims=True), sk)
    return jnp.exp(s - m).astype(BF16), jnp.exp(sk - m)


def _attn_values(p, de, vx, lo_half):
    return _attn_slabs(_dot(p, vx), de, lo_half)


def _attn_slabs(r, de, lo_half):
    rows = r.shape[0] // A_GROUP
    r_lo, r_hi = r[:, :LANES], r[:, LANES:]

    def slab(g0):
        e = slice(g0 * rows, (g0 + 1) * rows)
        o = slice((g0 + 1) * rows, (g0 + 2) * rows)
        return jnp.where(lo_half, r_lo[e] / (r_hi[e] + de[e]), r_hi[o] / (r_lo[o] + de[o]))

    return slab(0), slab(2)


def _gla_inputs(xn, win_ref, wgu_ref, bg_ref, block):
    rows = xn.shape[0]
    qb = _proj(xn, win_ref, C_QB, C_KB)
    kb = _proj(xn, win_ref, C_KB, C_VB)
    vb = _proj(xn, win_ref, C_VB, C_GB)
    ab = _proj(xn, win_ref, C_AB, IN_COLS)
    logg = _log_sigmoid(_dot(ab.astype(BF16), wgu_ref[...]) + bg_ref[...]) * (1.0 / GATE_TAU)
    g_hi, g_lo = _split_bf16(logg)
    tri = _block_mask(rows, block, lower=True)
    ones = _block_mask(rows, block, lower=False)
    b = _dot(tri, g_hi) + _dot(tri, g_lo)
    b_last = _dot(ones, g_hi) + _dot(ones, g_lo)
    qt = qb * (jnp.exp(b) * (B_KEY_DIM ** -0.5))
    kt = kb * jnp.exp(-b)
    ke = kb * jnp.exp(b_last - b)
    return qt, kt, ke, vb, logg


def _head_stack(x, head_mask):
    z = jnp.zeros_like(x)
    return jnp.concatenate([jnp.where(head_mask[h], x, z) for h in range(B_HEADS)], axis=0).astype(BF16)


def _finish_layer(h, oa, ob, xn, p, win_ref, glag_ref, wout_ref, peg_ref, wpe_ref, wpg_ref):
    ga = _proj(xn, win_ref, C_GA, C_QB)
    gb = _proj(xn, win_ref, C_GB, C_AB)
    obn = jnp.concatenate(
        [_rms(ob[:, hh * B_VAL_DIM:(hh + 1) * B_VAL_DIM], glag_ref[...]) for hh in range(B_HEADS)], axis=1)
    mix = jnp.concatenate([oa * _silu(ga), obn * _silu(gb)], axis=1).astype(BF16)
    h1 = h + _dot(mix, wout_ref[...])
    gate = _sigmoid(_dot(_rms(h1, peg_ref[...]).astype(BF16), wpg_ref[...]))
    return h1 + gate * _dot(p.astype(BF16), wpe_ref[...])


def _gla_decays(kbt, abt, wgut_ref, bgc_ref):
    tile = kbt.shape[1]
    loggt = _log_sigmoid(_dot(wgut_ref[...], abt) + bgc_ref[...]) * (1.0 / GATE_TAU)
    g_hi, g_lo = _split_bf16(loggt)
    r = lax.broadcasted_iota(jnp.int32, (GLA_SPAN, GLA_SPAN), 0)
    c = lax.broadcasted_iota(jnp.int32, (GLA_SPAN, GLA_SPAN), 1)
    upper = jnp.where(((r // CHUNK) == (c // CHUNK)) & (r <= c), 1.0, 0.0).astype(BF16)
    bt = jnp.concatenate(
        [_dot(g_hi[:, s0:s0 + GLA_SPAN], upper) + _dot(g_lo[:, s0:s0 + GLA_SPAN], upper)
         for s0 in range(0, tile, GLA_SPAN)], axis=1)
    ktt = (kbt * jnp.exp(-bt)).astype(BF16)
    pair = 2 * CHUNK
    cols = [bt[:, (c + 1) * CHUNK - 1:(c + 1) * CHUNK] for c in range(tile // CHUNK)]
    lane = lax.broadcasted_iota(jnp.int32, (1, pair), 1)
    ket = jnp.concatenate(
        [kbt[:, pc * pair:(pc + 1) * pair]
         * jnp.exp(jnp.where(lane < CHUNK, cols[2 * pc], cols[2 * pc + 1]) - bt[:, pc * pair:(pc + 1) * pair])
         for pc in range(tile // pair)], axis=1)
    return bt, ktt, ket, cols


def _gla_chunk_jobs(qb, qt, vb, kbt, bt, ktt, ket, cols, stk, out):
    tile = qt.shape[0]
    n_chunks = tile // CHUNK
    pair = 2 * CHUNK
    head_of_lane = lax.broadcasted_iota(jnp.int32, (1, B_QK_WIDTH), 1) // B_KEY_DIM
    head_mask = [head_of_lane == hh for hh in range(B_HEADS)]
    lane = lax.broadcasted_iota(jnp.int32, (1, pair), 1)
    t_row = lax.broadcasted_iota(jnp.int32, (CHUNK, pair), 0)
    s_col = lax.broadcasted_iota(jnp.int32, (CHUNK, pair), 1)
    own = [lane < CHUNK, lane >= CHUNK]
    causal = [s_col <= t_row, (s_col >= CHUNK) & (s_col - CHUNK <= t_row)]
    v_heads = [[vb[(c // 2) * pair:(c // 2 + 1) * pair, hh * B_VAL_DIM:(hh + 1) * B_VAL_DIM]
                for hh in range(B_HEADS)] for c in range(n_chunks)]
    updates, states, results, ob_rows = {}, {}, {}, {}

    def update(c):
        ps = slice((c // 2) * pair, (c // 2 + 1) * pair)
        ket_c = jnp.where(own[c % 2], ket[:, ps], 0.0).astype(BF16)
        updates[c] = jnp.concatenate(
            [_dot(ket_c[hh * B_KEY_DIM:(hh + 1) * B_KEY_DIM], v_heads[c][hh]) for hh in range(B_HEADS)], axis=0)

    def chain():
        s = stk
        for c in range(n_chunks):
            states[c] = s
            s = s * jnp.exp(cols[c]) + updates[c]
        out["state"] = s

    def result(c):
        ps = slice((c // 2) * pair, (c // 2 + 1) * pair)
        q_stack = _head_stack(qt[c * CHUNK:(c + 1) * CHUNK], head_mask)
        results[c] = _dot(q_stack, jnp.concatenate([states[c].astype(BF16), ktt[:, ps]], axis=1))

    def chunk_output(c, res):
        outs = []
        for hh in range(B_HEADS):
            hr = slice(hh * CHUNK, (hh + 1) * CHUNK)
            a_h = jnp.where(causal[c % 2], res[hr, B_VAL_DIM:], 0.0).astype(BF16)
            outs.append(_dot(a_h, v_heads[c][hh]) + res[hr, :B_VAL_DIM])
        return jnp.concatenate(outs, axis=1)

    def output(c):
        ob_rows[c] = chunk_output(c, results.pop(c))
        if c == n_chunks - 1:
            out["ob"] = jnp.concatenate([ob_rows[cc] for cc in range(n_chunks)], axis=0)

    def wide_range_ob():
        n_sub = CHUNK // GLA_SUB
        rows = []
        for c in range(n_chunks):
            ps = slice((c // 2) * pair, (c // 2 + 1) * pair)
            first = (c % 2) * CHUNK
            bt_c = bt[:, c * CHUNK:(c + 1) * CHUNK]
            res = []
            for i in range(n_sub):
                r0 = c * CHUNK + i * GLA_SUB
                ref = bt_c[:, i * GLA_SUB - 1:i * GLA_SUB] if i else jnp.zeros((B_QK_WIDTH, 1), F32)
                rel_q = (bt_c[:, i * GLA_SUB:(i + 1) * GLA_SUB] - ref).T
                q_sub = qb[r0:r0 + GLA_SUB] * (jnp.exp(rel_q) * (B_KEY_DIM ** -0.5))
                visible = (lane >= first) & (lane < first + (i + 1) * GLA_SUB)
                keys = kbt[:, ps] * jnp.exp(jnp.where(visible, ref - bt[:, ps], NEG_INF))
                state = (states[c] * jnp.exp(ref)).astype(BF16)
                res.append(_dot(_head_stack(q_sub, head_mask),
                                jnp.concatenate([state, keys.astype(BF16)], axis=1)))
            rows.append(chunk_output(c, jnp.concatenate(
                [res[i][hh * GLA_SUB:(hh + 1) * GLA_SUB] for hh in range(B_HEADS) for i in range(n_sub)], axis=0)))
        return jnp.concatenate(rows, axis=0)

    out["wide_range_ob"] = wide_range_ob

    bind = lambda f, c: (lambda: f(c))
    return ([bind(update, c) for c in range(n_chunks)] + [chain]
            + [bind(result, c) for c in range(n_chunks)] + [bind(output, c) for c in range(n_chunks)])


def _trace_interleaved(jobs_a, jobs_b):
    na, nb = len(jobs_a), len(jobs_b)
    ia = ib = 0
    while ia < na or ib < nb:
        if ib >= nb or (ia < na and ia * nb <= ib * na):
            jobs_a[ia]()
            ia += 1
        else:
            jobs_b[ib]()
            ib += 1


def _prompt_kernel(sink_ref, h_ref, p_ref, rope_ref, ng_ref, win_ref, gqk_ref, wgu_ref, bg_ref, glag_ref,
                   wout_ref, peg_ref, wpe_ref, wpg_ref, wgut_ref, bgc_ref,
                   ho_ref, nk_ref, nv_ref, st_ref,
                   kd_scr, vx_scr, st_scr, *, layer):
    tile = PROMPT_TILE
    n_chunks = tile // CHUNK
    band = WINDOW + CHUNK
    i = pl.program_id(0)
    rd = (i + 1) % 2
    wr = i % 2

    @pl.when(i == 0)
    def _():
        kd_scr[rd] = jnp.zeros(kd_scr.shape[1:], BF16)
        vx_scr[rd] = jnp.zeros(vx_scr.shape[1:], BF16)
        st_scr[rd] = jnp.zeros(st_scr.shape[1:], F32)

    h = h_ref[...]
    xn = _rms(h, ng_ref[...]).astype(BF16)
    lo_half = lax.broadcasted_iota(jnp.int32, (1, LANES), 1) < A_HEAD_DIM
    key_blk = lax.broadcasted_iota(jnp.int32, (1, band), 1) // CHUNK
    blocks = [(c, j) for c in range(n_chunks) for j in range(A_KV_HEADS)]
    n_blocks = len(blocks)
    v, scores, probs, slabs, gla = {}, {}, {}, {}, {}
    bind = lambda f, n: (lambda: f(n))

    qkv = _proj(xn, win_ref, C_QA, C_GA)

    def rope_tab():
        tab = rope_ref[...].T
        head_lane = lax.broadcasted_iota(jnp.int32, (1, LANES), 1) % A_HEAD_DIM
        v["cos"] = jnp.where(head_lane < ROT_DIM, tab, 1.0)
        v["sin"] = jnp.where(head_lane < ROT_DIM, pltpu.roll(tab, LANES - ROT_DIM, 1), 0.0)
        v["first_half"] = head_lane < ROT_DIM // 2

    def norm(g):
        width = min(MXU_TILE, C_VA - MXU_TILE * g)
        x = qkv[:, MXU_TILE * g:MXU_TILE * g + width]
        seg = _block_mask(width, A_HEAD_DIM, lower=False)
        hi, lo = _split_bf16(x * x)
        ss = _dot(hi, seg) + _dot(lo, seg)
        xs = x * lax.rsqrt(ss * (1.0 / A_HEAD_DIM) + EPS) * gqk_ref[:, MXU_TILE * g:MXU_TILE * g + width]
        for s in range(width // LANES):
            x1 = xs[:, s * LANES:(s + 1) * LANES]
            up = pltpu.roll(x1, LANES - ROT_DIM // 2, 1)
            dn = pltpu.roll(x1, ROT_DIM // 2, 1)
            v[f"slab{2 * g + s}"] = x1 * v["cos"] + jnp.where(v["first_half"], up, dn) * v["sin"]

    def kv_ops():
        k, vv = v["slab4"], qkv[:, C_VA:]
        nk_ref[...] = k[tile - WINDOW:, :]
        nv_ref[...] = vv[tile - WINDOW:, :]
        kd, vx = _kv_operands(k, vv, lo_half)
        v["kd_ext"] = [jnp.concatenate([kd_scr[rd, j], kd[j]], axis=0) for j in range(A_KV_HEADS)]
        v["vx_ext"] = [jnp.concatenate([vx_scr[rd, j], vx[j]], axis=0) for j in range(A_KV_HEADS)]
        for j in range(A_KV_HEADS):
            kd_scr[wr, j] = kd[j][tile - WINDOW:, :]
            vx_scr[wr, j] = vx[j][tile - WINDOW:, :]

    def proj_k():
        v["kbt"] = _dot_nt(win_ref[C_KB:C_VB, :], xn)
        v["abt"] = _dot_nt(win_ref[C_AB:IN_COLS, :], xn).astype(BF16)

    def proj_qv():
        v["qb"] = _proj(xn, win_ref, C_QB, C_KB)
        v["vb"] = _proj(xn, win_ref, C_VB, C_GB).astype(BF16)

    _trace_interleaved([rope_tab, bind(norm, 2), kv_ops, bind(norm, 0), bind(norm, 1)], [proj_k, proj_qv])

    def score(n):
        c, j = blocks[n]
        cs = slice(c * CHUNK, (c + 1) * CHUNK)
        q_blk = jnp.concatenate([v[f"slab{2 * j}"][cs], v[f"slab{2 * j + 1}"][cs]], axis=1)
        scores[n] = _attn_scores(q_blk, 0, v["kd_ext"][j][c * CHUNK:c * CHUNK + band, :], lo_half)

    def decays():
        gla["bt"], gla["ktt"], gla["ket"], gla["cols"] = _gla_decays(v["kbt"], v["abt"], wgut_ref, bgc_ref)

    def scale_q():
        gla["qt"] = v["qb"] * (jnp.exp(gla["bt"].T) * (B_KEY_DIM ** -0.5))

    _trace_interleaved([bind(score, n) for n in range(n_blocks)], [decays, scale_q])

    def prob(n):
        c, j = blocks[n]
        first_key_chunk = i * n_chunks + c - WINDOW // CHUNK
        bias = jnp.where(key_blk + first_key_chunk >= 0, 0.0, NEG_INF)
        probs[n] = _attn_probs(scores.pop(n), j, bias, sink_ref, layer)

    def value(n):
        c, j = blocks[n]
        slabs[n] = _attn_values(*probs.pop(n), v["vx_ext"][j][c * CHUNK:c * CHUNK + band, :], lo_half)

    def proj_ga():
        v["ga"] = _proj(xn, win_ref, C_GA, C_QB)

    def proj_gb():
        v["gb"] = _proj(xn, win_ref, C_GB, C_AB)

    def embed():
        v["pe"] = _dot(p_ref[...].astype(BF16), wpe_ref[...])

    gla_jobs = _gla_chunk_jobs(v["qb"], gla["qt"], v["vb"], v["kbt"], gla["bt"], gla["ktt"], gla["ket"], gla["cols"],
                               st_scr[rd], gla)
    n_first = n_chunks + 1
    _trace_interleaved([bind(prob, n) for n in range(n_blocks)], [proj_ga, proj_gb, embed] + gla_jobs[:n_first])
    _trace_interleaved([bind(value, n) for n in range(n_blocks)], gla_jobs[n_first:])
    st_scr[wr] = gla["state"]
    st_ref[...] = gla["state"]
    ob, pe, ga, gb = gla["ob"], v["pe"], v["ga"], v["gb"]

    def finish(ob):
        for r0 in range(0, tile, FINISH_ROWS):
            rs = slice(r0, r0 + FINISH_ROWS)
            oa = jnp.concatenate(
                [jnp.concatenate([sl for j in range(A_KV_HEADS) for sl in slabs[c * A_KV_HEADS + j]], axis=1)
                 for c in range(r0 // CHUNK, (r0 + FINISH_ROWS) // CHUNK)], axis=0)
            h1 = h[rs] + _dot((oa * _silu(ga[rs])).astype(BF16), wout_ref[0:A_WIDTH, :])
            obn = jnp.concatenate(
                [_rms(ob[rs, hh * B_VAL_DIM:(hh + 1) * B_VAL_DIM], glag_ref[...]) for hh in range(B_HEADS)], axis=1)
            h2 = h1 + _dot((obn * _silu(gb[rs])).astype(BF16), wout_ref[A_WIDTH:, :])
            gate = _sigmoid(_dot(_rms(h2, peg_ref[...]).astype(BF16), wpg_ref[...]))
            ho_ref[rs, :] = h2 + gate * pe[rs]

    finish(ob)

    strongest = gla["cols"][0]
    for c in range(1, n_chunks):
        strongest = jnp.minimum(strongest, gla["cols"][c])

    @pl.when(jnp.min(strongest) < -GLA_FAST_MAX_DECAY)
    def _():
        finish(gla["wide_range_ob"]())


def _sample_kernel(sink_ref, h_ref, p_ref, rope_ref, ck_ref, cv_ref, s0_ref, ng_ref, win_ref, gqk_ref, wgu_ref,
                   bg_ref, glag_ref, wout_ref, peg_ref, wpe_ref, wpg_ref,
                   ho_ref, nk_ref, nv_ref, st_ref, *, layer, dec_seq):
    nb = SAMPLE_BATCH_TILE
    h = h_ref[...]
    xn = _rms(h, ng_ref[...]).astype(BF16)
    lo_half = lax.broadcasted_iota(jnp.int32, (1, LANES), 1) < A_HEAD_DIM

    q, k, v = _project_a(xn, rope_ref[...].T, win_ref, gqk_ref)
    nk_ref[...] = k
    nv_ref[...] = v
    kd_new, vx_new = _kv_operands(k, v, lo_half)
    rows_of = [slice(bi * dec_seq, (bi + 1) * dec_seq) for bi in range(nb)]
    blocks = [(bi, j) for bi in range(nb) for j in range(A_KV_HEADS)]
    stacks = [_query_stack(q[rows_of[bi]], j, lo_half) for bi, j in blocks]
    scores = []
    for (bi, j), st in zip(blocks, stacks):
        kt_old = ck_ref[bi, j].astype(BF16)
        scores.append(jnp.concatenate([_dot(st, jnp.concatenate([kt_old, kt_old], axis=0)),
                                       _dot_nt(st, kd_new[j][rows_of[bi]])], axis=1))
    probs = [_attn_probs(s, j, None, sink_ref, layer) for (bi, j), s in zip(blocks, scores)]
    ones = jnp.ones((A_HEAD_DIM, WINDOW), BF16)
    slabs = []
    for (bi, j), (p, de) in zip(blocks, probs):
        vt_old = cv_ref[bi, j].astype(BF16)
        r = (_dot_nt(p[:, :WINDOW], jnp.concatenate([vt_old, ones, ones, vt_old], axis=0))
             + _dot(p[:, WINDOW:], vx_new[j][rows_of[bi]]))
        slabs.append(_attn_slabs(r, de, lo_half))
    oa = jnp.concatenate(
        [jnp.concatenate([sl for j in range(A_KV_HEADS) for sl in slabs[bi * A_KV_HEADS + j]], axis=1)
         for bi in range(nb)], axis=0)

    qt, kt, ke, vb, g_all = _gla_inputs(xn, win_ref, wgu_ref, bg_ref, dec_seq)
    head_of_lane = lax.broadcasted_iota(jnp.int32, (1, B_QK_WIDTH), 1) // B_KEY_DIM
    head_mask = [head_of_lane == hh for hh in range(B_HEADS)]
    causal = (lax.broadcasted_iota(jnp.int32, (dec_seq, dec_seq), 0)
              >= lax.broadcasted_iota(jnp.int32, (dec_seq, dec_seq), 1))
    ones_rhs = jnp.ones((dec_seq, LANES), BF16)
    vb = vb.astype(BF16)
    g_hi, g_lo = _split_bf16(g_all)
    q_stacks = [_head_stack(qt[rs], head_mask) for rs in rows_of]
    atts = [_dot_nt(q_stacks[bi], kt[rows_of[bi]].astype(BF16)) for bi in range(nb)]
    inters = [_dot(q_stacks[bi], s0_ref[bi].astype(BF16)) for bi in range(nb)]
    totals = [_dot_tn(g_hi[rs], ones_rhs) + _dot_tn(g_lo[rs], ones_rhs) for rs in rows_of]
    upd_fulls = [_dot_tn(ke[rs].astype(BF16), vb[rs]) for rs in rows_of]
    atts = [jnp.concatenate([jnp.where(causal, a[hh * dec_seq:(hh + 1) * dec_seq], 0.0)
                             for hh in range(B_HEADS)], axis=0).astype(BF16) for a in atts]
    intras = [_dot(atts[bi], vb[rows_of[bi]]) for bi in range(nb)]
    ob = jnp.concatenate(
        [jnp.concatenate([intras[bi][hh * dec_seq:(hh + 1) * dec_seq, hh * B_VAL_DIM:(hh + 1) * B_VAL_DIM]
                          + inters[bi][hh * dec_seq:(hh + 1) * dec_seq] for hh in range(B_HEADS)], axis=1)
         for bi in range(nb)], axis=0)
    for bi in range(nb):
        upd = jnp.concatenate(
            [upd_fulls[bi][hh * B_KEY_DIM:(hh + 1) * B_KEY_DIM, hh * B_VAL_DIM:(hh + 1) * B_VAL_DIM]
             for hh in range(B_HEADS)], axis=0)
        st_ref[bi] = s0_ref[bi] * jnp.exp(totals[bi]) + upd

    ho_ref[...] = _finish_layer(h, oa, ob, xn, p_ref[...], win_ref, glag_ref, wout_ref, peg_ref,
                                wpe_ref, wpg_ref)


def _rope_table(pos):
    half = ROT_DIM // 2
    inv = jnp.power(jnp.float32(ROPE_THETA), -jnp.arange(half, dtype=F32) * (2.0 / ROT_DIM))
    ang = inv[:, None] * pos.astype(F32)[None, :]
    cos, sin = jnp.cos(ang), jnp.sin(ang)
    head = jnp.concatenate([cos, cos, -sin, sin, jnp.zeros((A_HEAD_DIM - 2 * ROT_DIM, pos.shape[0]), F32)], axis=0)
    return jnp.concatenate([head, head], axis=0)


def _layer_spec(layer, shape):
    return pl.BlockSpec((None,) + shape, lambda i: (layer,) + (0,) * len(shape))


def _weight_specs(layer):
    return [
        _layer_spec(layer, (1, D_MODEL)),
        _layer_spec(layer, (IN_COLS, D_MODEL)),
        _layer_spec(layer, (1, C_VA)),
        _layer_spec(layer, (GATE_RANK, B_QK_WIDTH)),
        _layer_spec(layer, (1, B_QK_WIDTH)),
        _layer_spec(layer, (1, B_VAL_DIM)),
        _layer_spec(layer, (A_WIDTH + B_WIDTH, D_MODEL)),
        _layer_spec(layer, (1, D_MODEL)),
        _layer_spec(layer, (D_PLE, D_MODEL)),
        _layer_spec(layer, (D_MODEL, D_MODEL)),
    ]


def _prompt_layer(layer, h, p, rope, sinks, weights, weights_t):
    seq = h.shape[0]
    tile = PROMPT_TILE
    assert seq % tile == 0 and tile % GLA_SPAN == 0 and GLA_SPAN % (2 * CHUNK) == 0 and tile >= WINDOW
    row = lambda w: pl.BlockSpec((tile, w), lambda i: (i, 0))
    const = lambda shape: pl.BlockSpec(shape, lambda i: (0,) * len(shape))
    return pl.pallas_call(
        functools.partial(_prompt_kernel, layer=layer),
        grid=(seq // tile,),
        in_specs=[pl.BlockSpec(memory_space=pltpu.SMEM), row(D_MODEL),
                  pl.BlockSpec((None, tile, D_PLE), lambda i: (layer, i, 0)),
                  pl.BlockSpec((LANES, tile), lambda i: (0, i))]
        + _weight_specs(layer)
        + [_layer_spec(layer, (B_QK_WIDTH, GATE_RANK)),
           _layer_spec(layer, (B_QK_WIDTH, 1))],
        out_specs=[row(D_MODEL), const((WINDOW, A_KV_WIDTH)), const((WINDOW, A_KV_WIDTH)),
                   const((B_QK_WIDTH, B_VAL_DIM))],
        out_shape=[jax.ShapeDtypeStruct((seq, D_MODEL), F32),
                   jax.ShapeDtypeStruct((WINDOW, A_KV_WIDTH), F32),
                   jax.ShapeDtypeStruct((WINDOW, A_KV_WIDTH), F32),
                   jax.ShapeDtypeStruct((B_QK_WIDTH, B_VAL_DIM), F32)],
        scratch_shapes=[pltpu.VMEM((2, A_KV_HEADS, WINDOW, LANES), BF16),
                        pltpu.VMEM((2, A_KV_HEADS, WINDOW, 2 * LANES), BF16),
                        pltpu.VMEM((2, B_QK_WIDTH, B_VAL_DIM), F32)],
        compiler_params=pltpu.CompilerParams(dimension_semantics=("arbitrary",),
                                             vmem_limit_bytes=VMEM_LIMIT_BYTES),
        name="prompt_layer",
    )(sinks, h, p, rope, *weights, *weights_t)


def _sample_layer(layer, h, p, rope, ck, cv, s0, sinks, weights, dec_seq):
    n_seq = ck.shape[1]
    nb = SAMPLE_BATCH_TILE
    assert n_seq % nb == 0
    rows = nb * dec_seq
    row = lambda w: pl.BlockSpec((rows, w), lambda i: (i, 0))
    per_seq_in = lambda a, b: pl.BlockSpec((None, nb, a, b), lambda i: (layer, i, 0, 0))
    cache_spec = pl.BlockSpec((None, nb, A_KV_HEADS, A_HEAD_DIM, WINDOW), lambda i: (layer, i, 0, 0, 0))
    return pl.pallas_call(
        functools.partial(_sample_kernel, layer=layer, dec_seq=dec_seq),
        grid=(n_seq // nb,),
        in_specs=[pl.BlockSpec(memory_space=pltpu.SMEM), row(D_MODEL),
                  pl.BlockSpec((None, rows, D_PLE), lambda i: (layer, i, 0)),
                  pl.BlockSpec((LANES, rows), lambda i: (0, 0)),
                  cache_spec, cache_spec, per_seq_in(B_QK_WIDTH, B_VAL_DIM)]
        + _weight_specs(layer),
        out_specs=[row(D_MODEL), row(A_KV_WIDTH), row(A_KV_WIDTH),
                   pl.BlockSpec((nb, B_QK_WIDTH, B_VAL_DIM), lambda i: (i, 0, 0))],
        out_shape=[jax.ShapeDtypeStruct((n_seq * dec_seq, D_MODEL), F32),
                   jax.ShapeDtypeStruct((n_seq * dec_seq, A_KV_WIDTH), F32),
                   jax.ShapeDtypeStruct((n_seq * dec_seq, A_KV_WIDTH), F32),
                   jax.ShapeDtypeStruct((n_seq, B_QK_WIDTH, B_VAL_DIM), F32)],
        compiler_params=pltpu.CompilerParams(dimension_semantics=("arbitrary",),
                                             vmem_limit_bytes=VMEM_LIMIT_BYTES),
        name="sample_layer",
    )(sinks, h, p, rope, ck, cv, s0, *weights)


def kernel(x_prompt, x_sample, cache_k, cache_v, state_gla, p_prompt, p_sample, norm_g, w_in, q_norm_g, k_norm_g,
           sinks, w_gate_up, b_gate, gla_norm_g, w_out, pe_norm_g, w_pe, w_pg):
    batch, seq, _ = x_prompt.shape
    n_seq, dec_seq, _ = x_sample.shape
    assert batch == 1 and cache_k.shape[2] == WINDOW

    rope_p = _rope_table(jnp.arange(seq, dtype=jnp.int32))
    rope_s = jnp.tile(_rope_table(PAST_LEN + jnp.arange(dec_seq, dtype=jnp.int32)), (1, SAMPLE_BATCH_TILE))

    wgu_b = w_gate_up.astype(BF16)
    gqk = jnp.concatenate([jnp.tile(q_norm_g, (1, A_HEADS)) * (A_HEAD_DIM ** -0.5),
                           jnp.tile(k_norm_g, (1, A_KV_HEADS))], axis=1)
    vec = lambda a: a[:, None, :]
    weights = (vec(norm_g), jnp.swapaxes(w_in, 1, 2).astype(BF16), vec(gqk), wgu_b, vec(b_gate), vec(gla_norm_g), w_out.astype(BF16),
               vec(pe_norm_g), w_pe.astype(BF16), w_pg.astype(BF16))
    weights_t = (jnp.swapaxes(w_gate_up, 1, 2).astype(BF16),
                 b_gate[:, :, None])

    hp = x_prompt.reshape(seq, D_MODEL)
    hs = x_sample.reshape(n_seq * dec_seq, D_MODEL)
    ck = jnp.transpose(cache_k, (0, 1, 3, 4, 2))
    cv = jnp.transpose(cache_v, (0, 1, 3, 4, 2))
    s0 = state_gla.reshape(DEPTH, n_seq, B_QK_WIDTH, B_VAL_DIM)
    pp = p_prompt.reshape(DEPTH, seq, D_PLE)
    ps = p_sample.reshape(DEPTH, n_seq * dec_seq, D_PLE)

    kp_l, vp_l, sp_l, ks_l, vs_l, ss_l = [], [], [], [], [], []
    for layer in range(DEPTH):
        hp, kn, vn, sn = _prompt_layer(layer, hp, pp, rope_p, sinks, weights, weights_t)
        kp_l.append(kn); vp_l.append(vn); sp_l.append(sn)
        hs, kn, vn, sn = _sample_layer(layer, hs, ps, rope_s, ck, cv, s0, sinks, weights, dec_seq)
        ks_l.append(kn); vs_l.append(vn); ss_l.append(sn)

    return (hp.reshape(batch, seq, D_MODEL),
            hs.reshape(n_seq, dec_seq, D_MODEL),
            jnp.stack(kp_l).reshape(DEPTH, batch, WINDOW, A_KV_HEADS, A_HEAD_DIM),
            jnp.stack(vp_l).reshape(DEPTH, batch, WINDOW, A_KV_HEADS, A_HEAD_DIM),
            jnp.stack(sp_l).reshape(DEPTH, batch, B_HEADS, B_KEY_DIM, B_VAL_DIM),
            jnp.stack(ks_l).reshape(DEPTH, n_seq, dec_seq, A_KV_HEADS, A_HEAD_DIM),
            jnp.stack(vs_l).reshape(DEPTH, n_seq, dec_seq, A_KV_HEADS, A_HEAD_DIM),
            jnp.stack(ss_l).reshape(DEPTH, n_seq, B_HEADS, B_KEY_DIM, B_VAL_DIM))
```

```python
import functools

import jax
import jax.numpy as jnp
from jax import lax
from jax.experimental import pallas as pl
from jax.experimental.pallas import tpu as pltpu

F32 = jnp.float32
BF16 = jnp.bfloat16

D_MODEL = 1024
DEPTH = 4
PAST_LEN = 1024
CHUNK = 64
D_PLE = 256
A_HEADS = 8
A_KV_HEADS = 2
A_HEAD_DIM = 64
A_GROUP = A_HEADS // A_KV_HEADS
A_WIDTH = A_HEADS * A_HEAD_DIM
A_KV_WIDTH = A_KV_HEADS * A_HEAD_DIM
WINDOW = 128
ROT_DIM = A_HEAD_DIM // 4
ROPE_THETA = 500000.0
B_HEADS = 4
B_KEY_DIM = 64
B_VAL_DIM = 128
B_QK_WIDTH = B_HEADS * B_KEY_DIM
B_WIDTH = B_HEADS * B_VAL_DIM
GATE_RANK = 16
GATE_TAU = 16.0
NEG_INF = -1e30
EPS = 1e-6

LANES = 128
MXU_TILE = 256

C_QA = 0
C_KA = C_QA + A_WIDTH
C_VA = C_KA + A_KV_WIDTH
C_GA = C_VA + A_KV_WIDTH
C_QB = C_GA + A_WIDTH
C_KB = C_QB + B_QK_WIDTH
C_VB = C_KB + B_QK_WIDTH
C_GB = C_VB + B_WIDTH
C_AB = C_GB + B_WIDTH
IN_COLS = C_AB + GATE_RANK

PROMPT_TILE = 512
GLA_SPAN = MXU_TILE
GLA_SUB = 16
GLA_FAST_MAX_DECAY = 60.0
FINISH_ROWS = 256
SAMPLE_BATCH_TILE = 16
VMEM_LIMIT_BYTES = 56 * 1024 * 1024


def _dot(a, b):
    return jnp.dot(a, b, preferred_element_type=F32)


def _dot_nt(a, b):
    return lax.dot_general(a, b, (((1,), (1,)), ((), ())), preferred_element_type=F32)


def _dot_tn(a, b):
    return lax.dot_general(a, b, (((0,), (0,)), ((), ())), preferred_element_type=F32)


def _proj(xn, wint_ref, c0, c1):
    return _dot_nt(xn, wint_ref[c0:c1, :])


def _split_bf16(x):
    hi = x.astype(BF16)
    lo = (x - hi.astype(F32)).astype(BF16)
    return hi, lo


def _rms(x, g):
    ms = jnp.mean(x * x, axis=-1, keepdims=True)
    return x * lax.rsqrt(ms + EPS) * g


def _sigmoid(x):
    return 0.5 + 0.5 * jnp.tanh(0.5 * x)


def _silu(x):
    return x * _sigmoid(x)


def _log_sigmoid(x):
    return jnp.minimum(x, 0.0) - jnp.log(1.0 + jnp.exp(-jnp.abs(x)))


def _block_mask(n, block, lower):
    r = lax.broadcasted_iota(jnp.int32, (n, n), 0)
    c = lax.broadcasted_iota(jnp.int32, (n, n), 1)
    m = (r // block) == (c // block)
    if lower:
        m = m & (r >= c)
    return jnp.where(m, 1.0, 0.0).astype(BF16)


def _project_a(xn, tab, win_ref, gqk_ref):
    return _norm_rope(_proj(xn, win_ref, C_QA, C_GA), tab, gqk_ref)


def _norm_rope(qkv, tab, gqk_ref):
    qk, v = qkv[:, :C_VA], qkv[:, C_VA:]
    seg = _block_mask(MXU_TILE, A_HEAD_DIM, lower=False)
    hi, lo = _split_bf16(qk * qk)
    parts = []
    for c0 in range(0, C_VA, MXU_TILE):
        c1 = min(c0 + MXU_TILE, C_VA)
        sg = seg[: c1 - c0, : c1 - c0]
        parts.append(_dot(hi[:, c0:c1], sg) + _dot(lo[:, c0:c1], sg))
    ss = jnp.concatenate(parts, axis=1)
    qkn = qk * lax.rsqrt(ss * (1.0 / A_HEAD_DIM) + EPS) * gqk_ref[...]
    head_lane = lax.broadcasted_iota(jnp.int32, (1, LANES), 1) % A_HEAD_DIM
    cos = jnp.where(head_lane < ROT_DIM, tab, 1.0)
    sin = jnp.where(head_lane < ROT_DIM, pltpu.roll(tab, LANES - ROT_DIM, 1), 0.0)
    first_half = head_lane < ROT_DIM // 2
    slabs = []
    for s in range(C_VA // LANES):
        x = qkn[:, s * LANES:(s + 1) * LANES]
        up = pltpu.roll(x, LANES - ROT_DIM // 2, 1)
        dn = pltpu.roll(x, ROT_DIM // 2, 1)
        slabs.append(x * cos + jnp.where(first_half, up, dn) * sin)
    q = jnp.concatenate(slabs[:A_WIDTH // LANES], axis=1)
    k = slabs[A_WIDTH // LANES]
    return q, k, v


def _kv_operands(k, v, lo_half):
    ksw = pltpu.roll(k, A_HEAD_DIM, 1)
    vsw = pltpu.roll(v, A_HEAD_DIM, 1)
    one = jnp.ones_like(v)
    kd = (jnp.where(lo_half, k, ksw).astype(BF16), jnp.where(lo_half, ksw, k).astype(BF16))
    vx = (jnp.concatenate([jnp.where(lo_half, v, one), jnp.where(lo_half, one, vsw)], axis=1).astype(BF16),
          jnp.concatenate([jnp.where(lo_half, vsw, one), jnp.where(lo_half, one, v)], axis=1).astype(BF16))
    return kd, vx


def _query_stack(q_blk, j, lo_half):
    a = q_blk[:, (2 * j) * LANES:(2 * j + 1) * LANES]
    b = q_blk[:, (2 * j + 1) * LANES:(2 * j + 2) * LANES]
    z = jnp.zeros_like(a)
    return jnp.concatenate([jnp.where(lo_half, a, z), jnp.where(lo_half, z, a),
                            jnp.where(lo_half, b, z), jnp.where(lo_half, z, b)], axis=0).astype(BF16)


def _attn_scores(q_blk, j, kd, lo_half):
    return _dot_nt(_query_stack(q_blk, j, lo_half), kd)


def _attn_probs(s, j, bias, sink_ref, layer):
    rows = s.shape[0] // A_GROUP
    if bias is not None:
        s = s + bias
    sk = jnp.concatenate([jnp.full((rows, 1), sink_ref[layer, A_GROUP * j + g], F32) for g in range(A_GROUP)],
                         axis=0)
    m = jnp.maximum(jnp.max(s, axis=-1, keepdims=True), sk)
    return jnp.exp(s - m).astype(BF16), jnp.exp(sk - m)


def _attn_values(p, de, vx, lo_half):
    return _attn_slabs(_dot(p, vx), de, lo_half)


def _attn_slabs(r, de, lo_half):
    rows = r.shape[0] // A_GROUP
    r_lo, r_hi = r[:, :LANES], r[:, LANES:]

    def slab(g0):
        e = slice(g0 * rows, (g0 + 1) * rows)
        o = slice((g0 + 1) * rows, (g0 + 2) * rows)
        return jnp.where(lo_half, r_lo[e] / (r_hi[e] + de[e]), r_hi[o] / (r_lo[o] + de[o]))

    return slab(0), slab(2)


def _gla_inputs(xn, win_ref, wgu_ref, bg_ref, block):
    rows = xn.shape[0]
    qb = _proj(xn, win_ref, C_QB, C_KB)
    kb = _proj(xn, win_ref, C_KB, C_VB)
    vb = _proj(xn, win_ref, C_VB, C_GB)
    ab = _proj(xn, win_ref, C_AB, IN_COLS)
    logg = _log_sigmoid(_dot(ab.astype(BF16), wgu_ref[...]) + bg_ref[...]) * (1.0 / GATE_TAU)
    g_hi, g_lo = _split_bf16(logg)
    tri = _block_mask(rows, block, lower=True)
    ones = _block_mask(rows, block, lower=False)
    b = _dot(tri, g_hi) + _dot(tri, g_lo)
    b_last = _dot(ones, g_hi) + _dot(ones, g_lo)
    qt = qb * (jnp.exp(b) * (B_KEY_DIM ** -0.5))
    kt = kb * jnp.exp(-b)
    ke = kb * jnp.exp(b_last - b)
    return qt, kt, ke, vb, logg


def _head_stack(x, head_mask):
    z = jnp.zeros_like(x)
    return jnp.concatenate([jnp.where(head_mask[h], x, z) for h in range(B_HEADS)], axis=0).astype(BF16)


def _finish_layer(h, oa, ob, xn, p, win_ref, glag_ref, wout_ref, peg_ref, wpe_ref, wpg_ref):
    ga = _proj(xn, win_ref, C_GA, C_QB)
    gb = _proj(xn, win_ref, C_GB, C_AB)
    obn = jnp.concatenate(
        [_rms(ob[:, hh * B_VAL_DIM:(hh + 1) * B_VAL_DIM], glag_ref[...]) for hh in range(B_HEADS)], axis=1)
    mix = jnp.concatenate([oa * _silu(ga), obn * _silu(gb)], axis=1).astype(BF16)
    h1 = h + _dot(mix, wout_ref[...])
    gate = _sigmoid(_dot(_rms(h1, peg_ref[...]).astype(BF16), wpg_ref[...]))
    return h1 + gate * _dot(p.astype(BF16), wpe_ref[...])


def _gla_decays(kbt, abt, wgut_ref, bgc_ref):
    tile = kbt.shape[1]
    loggt = _log_sigmoid(_dot(wgut_ref[...], abt) + bgc_ref[...]) * (1.0 / GATE_TAU)
    g_hi, g_lo = _split_bf16(loggt)
    r = lax.broadcasted_iota(jnp.int32, (GLA_SPAN, GLA_SPAN), 0)
    c = lax.broadcasted_iota(jnp.int32, (GLA_SPAN, GLA_SPAN), 1)
    upper = jnp.where(((r // CHUNK) == (c // CHUNK)) & (r <= c), 1.0, 0.0).astype(BF16)
    bt = jnp.concatenate(
        [_dot(g_hi[:, s0:s0 + GLA_SPAN], upper) + _dot(g_lo[:, s0:s0 + GLA_SPAN], upper)
         for s0 in range(0, tile, GLA_SPAN)], axis=1)
    ktt = (kbt * jnp.exp(-bt)).astype(BF16)
    pair = 2 * CHUNK
    cols = [bt[:, (c + 1) * CHUNK - 1:(c + 1) * CHUNK] for c in range(tile // CHUNK)]
    lane = lax.broadcasted_iota(jnp.int32, (1, pair), 1)
    ket = jnp.concatenate(
        [kbt[:, pc * pair:(pc + 1) * pair]
         * jnp.exp(jnp.where(lane < CHUNK, cols[2 * pc], cols[2 * pc + 1]) - bt[:, pc * pair:(pc + 1) * pair])
         for pc in range(tile // pair)], axis=1)
    return bt, ktt, ket, cols


def _gla_chunk_jobs(qb, qt, vb, kbt, bt, ktt, ket, cols, stk, out):
    tile = qt.shape[0]
    n_chunks = tile // CHUNK
    pair = 2 * CHUNK
    head_of_lane = lax.broadcasted_iota(jnp.int32, (1, B_QK_WIDTH), 1) // B_KEY_DIM
    head_mask = [head_of_lane == hh for hh in range(B_HEADS)]
    lane = lax.broadcasted_iota(jnp.int32, (1, pair), 1)
    t_row = lax.broadcasted_iota(jnp.int32, (CHUNK, pair), 0)
    s_col = lax.broadcasted_iota(jnp.int32, (CHUNK, pair), 1)
    own = [lane < CHUNK, lane >= CHUNK]
    causal = [s_col <= t_row, (s_col >= CHUNK) & (s_col - CHUNK <= t_row)]
    v_heads = [[vb[(c // 2) * pair:(c // 2 + 1) * pair, hh * B_VAL_DIM:(hh + 1) * B_VAL_DIM]
                for hh in range(B_HEADS)] for c in range(n_chunks)]
    updates, states, results, ob_rows = {}, {}, {}, {}

    def update(c):
        ps = slice((c // 2) * pair, (c // 2 + 1) * pair)
        ket_c = jnp.where(own[c % 2], ket[:, ps], 0.0).astype(BF16)
        updates[c] = jnp.concatenate(
            [_dot(ket_c[hh * B_KEY_DIM:(hh + 1) * B_KEY_DIM], v_heads[c][hh]) for hh in range(B_HEADS)], axis=0)

    def chain():
        s = stk
        for c in range(n_chunks):
            states[c] = s
            s = s * jnp.exp(cols[c]) + updates[c]
        out["state"] = s

    def result(c):
        ps = slice((c // 2) * pair, (c // 2 + 1) * pair)
        q_stack = _head_stack(qt[c * CHUNK:(c + 1) * CHUNK], head_mask)
        results[c] = _dot(q_stack, jnp.concatenate([states[c].astype(BF16), ktt[:, ps]], axis=1))

    def chunk_output(c, res):
        outs = []
        for hh in range(B_HEADS):
            hr = slice(hh * CHUNK, (hh + 1) * CHUNK)
            a_h = jnp.where(causal[c % 2], res[hr, B_VAL_DIM:], 0.0).astype(BF16)
            outs.append(_dot(a_h, v_heads[c][hh]) + res[hr, :B_VAL_DIM])
        return jnp.concatenate(outs, axis=1)

    def output(c):
        ob_rows[c] = chunk_output(c, results.pop(c))
        if c == n_chunks - 1:
            out["ob"] = jnp.concatenate([ob_rows[cc] for cc in range(n_chunks)], axis=0)

    def wide_range_ob():
        n_sub = CHUNK // GLA_SUB
        rows = []
        for c in range(n_chunks):
            ps = slice((c // 2) * pair, (c // 2 + 1) * pair)
            first = (c % 2) * CHUNK
            bt_c = bt[:, c * CHUNK:(c + 1) * CHUNK]
            res = []
            for i in range(n_sub):
                r0 = c * CHUNK + i * GLA_SUB
                ref = bt_c[:, i * GLA_SUB - 1:i * GLA_SUB] if i else jnp.zeros((B_QK_WIDTH, 1), F32)
                rel_q = (bt_c[:, i * GLA_SUB:(i + 1) * GLA_SUB] - ref).T
                q_sub = qb[r0:r0 + GLA_SUB] * (jnp.exp(rel_q) * (B_KEY_DIM ** -0.5))
                visible = (lane >= first) & (lane < first + (i + 1) * GLA_SUB)
                keys = kbt[:, ps] * jnp.exp(jnp.where(visible, ref - bt[:, ps], NEG_INF))
                state = (states[c] * jnp.exp(ref)).astype(BF16)
                res.append(_dot(_head_stack(q_sub, head_mask),
                                jnp.concatenate([state, keys.astype(BF16)], axis=1)))
            rows.append(chunk_output(c, jnp.concatenate(
                [res[i][hh * GLA_SUB:(hh + 1) * GLA_SUB] for hh in range(B_HEADS) for i in range(n_sub)], axis=0)))
        return jnp.concatenate(rows, axis=0)

    out["wide_range_ob"] = wide_range_ob

    bind = lambda f, c: (lambda: f(c))
    return ([bind(update, c) for c in range(n_chunks)] + [chain]
            + [bind(result, c) for c in range(n_chunks)] + [bind(output, c) for c in range(n_chunks)])


def _trace_interleaved(jobs_a, jobs_b):
    na, nb = len(jobs_a), len(jobs_b)
    ia = ib = 0
    while ia < na or ib < nb:
        if ib >= nb or (ia < na and ia * nb <= ib * na):
            jobs_a[ia]()
            ia += 1
        else:
            jobs_b[ib]()
            ib += 1


def _prompt_kernel(sink_ref, h_ref, p_ref, rope_ref, ng_ref, win_ref, gqk_ref, wgu_ref, bg_ref, glag_ref,
                   wout_ref, peg_ref, wpe_ref, wpg_ref, wgut_ref, bgc_ref,
                   ho_ref, nk_ref, nv_ref, st_ref,
                   kd_scr, vx_scr, st_scr, *, layer):
    tile = PROMPT_TILE
    n_chunks = tile // CHUNK
    band = WINDOW + CHUNK
    i = pl.program_id(0)
    rd = (i + 1) % 2
    wr = i % 2

    @pl.when(i == 0)
    def _():
        kd_scr[rd] = jnp.zeros(kd_scr.shape[1:], BF16)
        vx_scr[rd] = jnp.zeros(vx_scr.shape[1:], BF16)
        st_scr[rd] = jnp.zeros(st_scr.shape[1:], F32)

    h = h_ref[...]
    xn = _rms(h, ng_ref[...]).astype(BF16)
    lo_half = lax.broadcasted_iota(jnp.int32, (1, LANES), 1) < A_HEAD_DIM
    key_blk = lax.broadcasted_iota(jnp.int32, (1, band), 1) // CHUNK
    blocks = [(c, j) for c in range(n_chunks) for j in range(A_KV_HEADS)]
    n_blocks = len(blocks)
    v, scores, probs, slabs, gla = {}, {}, {}, {}, {}
    bind = lambda f, n: (lambda: f(n))

    qkv = _proj(xn, win_ref, C_QA, C_GA)

    def rope_tab():
        tab = rope_ref[...].T
        head_lane = lax.broadcasted_iota(jnp.int32, (1, LANES), 1) % A_HEAD_DIM
        v["cos"] = jnp.where(head_lane < ROT_DIM, tab, 1.0)
        v["sin"] = jnp.where(head_lane < ROT_DIM, pltpu.roll(tab, LANES - ROT_DIM, 1), 0.0)
        v["first_half"] = head_lane < ROT_DIM // 2

    def norm(g):
        width = min(MXU_TILE, C_VA - MXU_TILE * g)
        x = qkv[:, MXU_TILE * g:MXU_TILE * g + width]
        seg = _block_mask(width, A_HEAD_DIM, lower=False)
        hi, lo = _split_bf16(x * x)
        ss = _dot(hi, seg) + _dot(lo, seg)
        xs = x * lax.rsqrt(ss * (1.0 / A_HEAD_DIM) + EPS) * gqk_ref[:, MXU_TILE * g:MXU_TILE * g + width]
        for s in range(width // LANES):
            x1 = xs[:, s * LANES:(s + 1) * LANES]
            up = pltpu.roll(x1, LANES - ROT_DIM // 2, 1)
            dn = pltpu.roll(x1, ROT_DIM // 2, 1)
            v[f"slab{2 * g + s}"] = x1 * v["cos"] + jnp.where(v["first_half"], up, dn) * v["sin"]

    def kv_ops():
        k, vv = v["slab4"], qkv[:, C_VA:]
        nk_ref[...] = k[tile - WINDOW:, :]
        nv_ref[...] = vv[tile - WINDOW:, :]
        kd, vx = _kv_operands(k, vv, lo_half)
        v["kd_ext"] = [jnp.concatenate([kd_scr[rd, j], kd[j]], axis=0) for j in range(A_KV_HEADS)]
        v["vx_ext"] = [jnp.concatenate([vx_scr[rd, j], vx[j]], axis=0) for j in range(A_KV_HEADS)]
        for j in range(A_KV_HEADS):
            kd_scr[wr, j] = kd[j][tile - WINDOW:, :]
            vx_scr[wr, j] = vx[j][tile - WINDOW:, :]

    def proj_k():
        v["kbt"] = _dot_nt(win_ref[C_KB:C_VB, :], xn)
        v["abt"] = _dot_nt(win_ref[C_AB:IN_COLS, :], xn).astype(BF16)

    def proj_qv():
        v["qb"] = _proj(xn, win_ref, C_QB, C_KB)
        v["vb"] = _proj(xn, win_ref, C_VB, C_GB).astype(BF16)

    _trace_interleaved([rope_tab, bind(norm, 2), kv_ops, bind(norm, 0), bind(norm, 1)], [proj_k, proj_qv])

    def score(n):
        c, j = blocks[n]
        cs = slice(c * CHUNK, (c + 1) * CHUNK)
        q_blk = jnp.concatenate([v[f"slab{2 * j}"][cs], v[f"slab{2 * j + 1}"][cs]], axis=1)
        scores[n] = _attn_scores(q_blk, 0, v["kd_ext"][j][c * CHUNK:c * CHUNK + band, :], lo_half)

    def decays():
        gla["bt"], gla["ktt"], gla["ket"], gla["cols"] = _gla_decays(v["kbt"], v["abt"], wgut_ref, bgc_ref)

    def scale_q():
        gla["qt"] = v["qb"] * (jnp.exp(gla["bt"].T) * (B_KEY_DIM ** -0.5))

    _trace_interleaved([bind(score, n) for n in range(n_blocks)], [decays, scale_q])

    def prob(n):
        c, j = blocks[n]
        first_key_chunk = i * n_chunks + c - WINDOW // CHUNK
        bias = jnp.where(key_blk + first_key_chunk >= 0, 0.0, NEG_INF)
        probs[n] = _attn_probs(scores.pop(n), j, bias, sink_ref, layer)

    def value(n):
        c, j = blocks[n]
        slabs[n] = _attn_values(*probs.pop(n), v["vx_ext"][j][c * CHUNK:c * CHUNK + band, :], lo_half)

    def proj_ga():
        v["ga"] = _proj(xn, win_ref, C_GA, C_QB)

    def proj_gb():
        v["gb"] = _proj(xn, win_ref, C_GB, C_AB)

    def embed():
        v["pe"] = _dot(p_ref[...].astype(BF16), wpe_ref[...])

    gla_jobs = _gla_chunk_jobs(v["qb"], gla["qt"], v["vb"], v["kbt"], gla["bt"], gla["ktt"], gla["ket"], gla["cols"],
                               st_scr[rd], gla)
    n_first = n_chunks + 1
    _trace_interleaved([bind(prob, n) for n in range(n_blocks)], [proj_ga, proj_gb, embed] + gla_jobs[:n_first])
    _trace_interleaved([bind(value, n) for n in range(n_blocks)], gla_jobs[n_first:])
    st_scr[wr] = gla["state"]
    st_ref[...] = gla["state"]
    ob, pe, ga, gb = gla["ob"], v["pe"], v["ga"], v["gb"]

    def finish(ob):
        for r0 in range(0, tile, FINISH_ROWS):
            rs = slice(r0, r0 + FINISH_ROWS)
            oa = jnp.concatenate(
                [jnp.concatenate([sl for j in range(A_KV_HEADS) for sl in slabs[c * A_KV_HEADS + j]], axis=1)
                 for c in range(r0 // CHUNK, (r0 + FINISH_ROWS) // CHUNK)], axis=0)
            h1 = h[rs] + _dot((oa * _silu(ga[rs])).astype(BF16), wout_ref[0:A_WIDTH, :])
            obn = jnp.concatenate(
                [_rms(ob[rs, hh * B_VAL_DIM:(hh + 1) * B_VAL_DIM], glag_ref[...]) for hh in range(B_HEADS)], axis=1)
            h2 = h1 + _dot((obn * _silu(gb[rs])).astype(BF16), wout_ref[A_WIDTH:, :])
            gate = _sigmoid(_dot(_rms(h2, peg_ref[...]).astype(BF16), wpg_ref[...]))
            ho_ref[rs, :] = h2 + gate * pe[rs]

    finish(ob)

    strongest = gla["cols"][0]
    for c in range(1, n_chunks):
        strongest = jnp.minimum(strongest, gla["cols"][c])

    @pl.when(jnp.min(strongest) < -GLA_FAST_MAX_DECAY)
    def _():
        finish(gla["wide_range_ob"]())


def _sample_kernel(sink_ref, h_ref, p_ref, rope_ref, ck_ref, cv_ref, s0_ref, ng_ref, win_ref, gqk_ref, wgu_ref,
                   bg_ref, glag_ref, wout_ref, peg_ref, wpe_ref, wpg_ref,
                   ho_ref, nk_ref, nv_ref, st_ref, *, layer, dec_seq):
    nb = SAMPLE_BATCH_TILE
    h = h_ref[...]
    xn = _rms(h, ng_ref[...]).astype(BF16)
    lo_half = lax.broadcasted_iota(jnp.int32, (1, LANES), 1) < A_HEAD_DIM

    q, k, v = _project_a(xn, rope_ref[...].T, win_ref, gqk_ref)
    nk_ref[...] = k
    nv_ref[...] = v
    kd_new, vx_new = _kv_operands(k, v, lo_half)
    rows_of = [slice(bi * dec_seq, (bi + 1) * dec_seq) for bi in range(nb)]
    blocks = [(bi, j) for bi in range(nb) for j in range(A_KV_HEADS)]
    stacks = [_query_stack(q[rows_of[bi]], j, lo_half) for bi, j in blocks]
    scores = []
    for (bi, j), st in zip(blocks, stacks):
        kt_old = ck_ref[bi, j].astype(BF16)
        scores.append(jnp.concatenate([_dot(st, jnp.concatenate([kt_old, kt_old], axis=0)),
                                       _dot_nt(st, kd_new[j][rows_of[bi]])], axis=1))
    probs = [_attn_probs(s, j, None, sink_ref, layer) for (bi, j), s in zip(blocks, scores)]
    ones = jnp.ones((A_HEAD_DIM, WINDOW), BF16)
    slabs = []
    for (bi, j), (p, de) in zip(blocks, probs):
        vt_old = cv_ref[bi, j].astype(BF16)
        r = (_dot_nt(p[:, :WINDOW], jnp.concatenate([vt_old, ones, ones, vt_old], axis=0))
             + _dot(p[:, WINDOW:], vx_new[j][rows_of[bi]]))
        slabs.append(_attn_slabs(r, de, lo_half))
    oa = jnp.concatenate(
        [jnp.concatenate([sl for j in range(A_KV_HEADS) for sl in slabs[bi * A_KV_HEADS + j]], axis=1)
         for bi in range(nb)], axis=0)

    qt, kt, ke, vb, g_all = _gla_inputs(xn, win_ref, wgu_ref, bg_ref, dec_seq)
    head_of_lane = lax.broadcasted_iota(jnp.int32, (1, B_QK_WIDTH), 1) // B_KEY_DIM
    head_mask = [head_of_lane == hh for hh in range(B_HEADS)]
    causal = (lax.broadcasted_iota(jnp.int32, (dec_seq, dec_seq), 0)
              >= lax.broadcasted_iota(jnp.int32, (dec_seq, dec_seq), 1))
    ones_rhs = jnp.ones((dec_seq, LANES), BF16)
    vb = vb.astype(BF16)
    g_hi, g_lo = _split_bf16(g_all)
    q_stacks = [_head_stack(qt[rs], head_mask) for rs in rows_of]
    atts = [_dot_nt(q_stacks[bi], kt[rows_of[bi]].astype(BF16)) for bi in range(nb)]
    inters = [_dot(q_stacks[bi], s0_ref[bi].astype(BF16)) for bi in range(nb)]
    totals = [_dot_tn(g_hi[rs], ones_rhs) + _dot_tn(g_lo[rs], ones_rhs) for rs in rows_of]
    upd_fulls = [_dot_tn(ke[rs].astype(BF16), vb[rs]) for rs in rows_of]
    atts = [jnp.concatenate([jnp.where(causal, a[hh * dec_seq:(hh + 1) * dec_seq], 0.0)
                             for hh in range(B_HEADS)], axis=0).astype(BF16) for a in atts]
    intras = [_dot(atts[bi], vb[rows_of[bi]]) for bi in range(nb)]
    ob = jnp.concatenate(
        [jnp.concatenate([intras[bi][hh * dec_seq:(hh + 1) * dec_seq, hh * B_VAL_DIM:(hh + 1) * B_VAL_DIM]
                          + inters[bi][hh * dec_seq:(hh + 1) * dec_seq] for hh in range(B_HEADS)], axis=1)
         for bi in range(nb)], axis=0)
    for bi in range(nb):
        upd = jnp.concatenate(
            [upd_fulls[bi][hh * B_KEY_DIM:(hh + 1) * B_KEY_DIM, hh * B_VAL_DIM:(hh + 1) * B_VAL_DIM]
             for hh in range(B_HEADS)], axis=0)
        st_ref[bi] = s0_ref[bi] * jnp.exp(totals[bi]) + upd

    ho_ref[...] = _finish_layer(h, oa, ob, xn, p_ref[...], win_ref, glag_ref, wout_ref, peg_ref,
                                wpe_ref, wpg_ref)


def _rope_table(pos):
    half = ROT_DIM // 2
    inv = jnp.power(jnp.float32(ROPE_THETA), -jnp.arange(half, dtype=F32) * (2.0 / ROT_DIM))
    ang = inv[:, None] * pos.astype(F32)[None, :]
    reps = A_HEAD_DIM // half
    cos, sin = jnp.tile(jnp.cos(ang), (reps, 1)), jnp.tile(jnp.sin(ang), (reps, 1))
    row = lax.broadcasted_iota(jnp.int32, (A_HEAD_DIM, 1), 0)
    head = jnp.where(row < ROT_DIM, cos,
                     jnp.where(row < ROT_DIM + half, -sin, jnp.where(row < 2 * ROT_DIM, sin, 0.0)))
    return jnp.tile(head, (LANES // A_HEAD_DIM, 1))


def _layer_spec(layer, shape):
    return pl.BlockSpec((None,) + shape, lambda i: (layer,) + (0,) * len(shape))


def _weight_specs(layer):
    return [
        _layer_spec(layer, (1, D_MODEL)),
        _layer_spec(layer, (IN_COLS, D_MODEL)),
        _layer_spec(layer, (1, C_VA)),
        _layer_spec(layer, (GATE_RANK, B_QK_WIDTH)),
        _layer_spec(layer, (1, B_QK_WIDTH)),
        _layer_spec(layer, (1, B_VAL_DIM)),
        _layer_spec(layer, (A_WIDTH + B_WIDTH, D_MODEL)),
        _layer_spec(layer, (1, D_MODEL)),
        _layer_spec(layer, (D_PLE, D_MODEL)),
        _layer_spec(layer, (D_MODEL, D_MODEL)),
    ]


def _prompt_layer(layer, h, p, rope, sinks, weights, weights_t):
    seq = h.shape[0]
    tile = PROMPT_TILE
    assert seq % tile == 0 and tile % GLA_SPAN == 0 and GLA_SPAN % (2 * CHUNK) == 0 and tile >= WINDOW
    row = lambda w: pl.BlockSpec((tile, w), lambda i: (i, 0))
    const = lambda shape: pl.BlockSpec(shape, lambda i: (0,) * len(shape))
    return pl.pallas_call(
        functools.partial(_prompt_kernel, layer=layer),
        grid=(seq // tile,),
        in_specs=[pl.BlockSpec(memory_space=pltpu.SMEM), row(D_MODEL),
                  pl.BlockSpec((None, tile, D_PLE), lambda i: (layer, i, 0)),
                  pl.BlockSpec((LANES, tile), lambda i: (0, i))]
        + _weight_specs(layer)
        + [_layer_spec(layer, (B_QK_WIDTH, GATE_RANK)),
           _layer_spec(layer, (B_QK_WIDTH, 1))],
        out_specs=[row(D_MODEL), const((WINDOW, A_KV_WIDTH)), const((WINDOW, A_KV_WIDTH)),
                   const((B_QK_WIDTH, B_VAL_DIM))],
        out_shape=[jax.ShapeDtypeStruct((seq, D_MODEL), F32),
                   jax.ShapeDtypeStruct((WINDOW, A_KV_WIDTH), F32),
                   jax.ShapeDtypeStruct((WINDOW, A_KV_WIDTH), F32),
                   jax.ShapeDtypeStruct((B_QK_WIDTH, B_VAL_DIM), F32)],
        scratch_shapes=[pltpu.VMEM((2, A_KV_HEADS, WINDOW, LANES), BF16),
                        pltpu.VMEM((2, A_KV_HEADS, WINDOW, 2 * LANES), BF16),
                        pltpu.VMEM((2, B_QK_WIDTH, B_VAL_DIM), F32)],
        compiler_params=pltpu.CompilerParams(dimension_semantics=("arbitrary",),
                                             vmem_limit_bytes=VMEM_LIMIT_BYTES),
        name="prompt_layer",
    )(sinks, h, p, rope, *weights, *weights_t)


def _sample_layer(layer, h, p, rope, ck, cv, s0, sinks, weights, dec_seq):
    n_seq = ck.shape[1]
    nb = SAMPLE_BATCH_TILE
    assert n_seq % nb == 0
    rows = nb * dec_seq
    row = lambda w: pl.BlockSpec((rows, w), lambda i: (i, 0))
    per_seq_in = lambda a, b: pl.BlockSpec((None, nb, a, b), lambda i: (layer, i, 0, 0))
    cache_spec = pl.BlockSpec((None, nb, A_KV_HEADS, A_HEAD_DIM, WINDOW), lambda i: (layer, i, 0, 0, 0))
    return pl.pallas_call(
        functools.partial(_sample_kernel, layer=layer, dec_seq=dec_seq),
        grid=(n_seq // nb,),
        in_specs=[pl.BlockSpec(memory_space=pltpu.SMEM), row(D_MODEL),
                  pl.BlockSpec((None, rows, D_PLE), lambda i: (layer, i, 0)),
                  pl.BlockSpec((LANES, rows), lambda i: (0, 0)),
                  cache_spec, cache_spec, per_seq_in(B_QK_WIDTH, B_VAL_DIM)]
        + _weight_specs(layer),
        out_specs=[row(D_MODEL), row(A_KV_WIDTH), row(A_KV_WIDTH),
                   pl.BlockSpec((nb, B_QK_WIDTH, B_VAL_DIM), lambda i: (i, 0, 0))],
        out_shape=[jax.ShapeDtypeStruct((n_seq * dec_seq, D_MODEL), F32),
                   jax.ShapeDtypeStruct((n_seq * dec_seq, A_KV_WIDTH), F32),
                   jax.ShapeDtypeStruct((n_seq * dec_seq, A_KV_WIDTH), F32),
                   jax.ShapeDtypeStruct((n_seq, B_QK_WIDTH, B_VAL_DIM), F32)],
        compiler_params=pltpu.CompilerParams(dimension_semantics=("arbitrary",),
                                             vmem_limit_bytes=VMEM_LIMIT_BYTES),
        name="sample_layer",
    )(sinks, h, p, rope, ck, cv, s0, *weights)


def kernel(x_prompt, x_sample, cache_k, cache_v, state_gla, p_prompt, p_sample, norm_g, w_in, q_norm_g, k_norm_g,
           sinks, w_gate_up, b_gate, gla_norm_g, w_out, pe_norm_g, w_pe, w_pg):
    batch, seq, _ = x_prompt.shape
    n_seq, dec_seq, _ = x_sample.shape
    assert batch == 1 and cache_k.shape[2] == WINDOW

    rope_p = _rope_table(jnp.arange(seq, dtype=jnp.int32))
    rope_s = jnp.tile(_rope_table(PAST_LEN + jnp.arange(dec_seq, dtype=jnp.int32)), (1, SAMPLE_BATCH_TILE))

    wgu_b = w_gate_up.astype(BF16)
    gqk = jnp.concatenate([jnp.tile(q_norm_g, (1, A_HEADS)) * (A_HEAD_DIM ** -0.5),
                           jnp.tile(k_norm_g, (1, A_KV_HEADS))], axis=1)
    vec = lambda a: a[:, None, :]
    weights = (vec(norm_g), jnp.swapaxes(w_in, 1, 2).astype(BF16), vec(gqk), wgu_b, vec(b_gate), vec(gla_norm_g), w_out.astype(BF16),
               vec(pe_norm_g), w_pe.astype(BF16), w_pg.astype(BF16))
    weights_t = (jnp.swapaxes(w_gate_up, 1, 2).astype(BF16),
                 b_gate[:, :, None])

    hp = x_prompt.reshape(seq, D_MODEL)
    hs = x_sample.reshape(n_seq * dec_seq, D_MODEL)
    ck = jnp.transpose(cache_k, (0, 1, 3, 4, 2))
    cv = jnp.transpose(cache_v, (0, 1, 3, 4, 2))
    s0 = state_gla.reshape(DEPTH, n_seq, B_QK_WIDTH, B_VAL_DIM)
    pp = p_prompt.reshape(DEPTH, seq, D_PLE)
    ps = p_sample.reshape(DEPTH, n_seq * dec_seq, D_PLE)

    kp_l, vp_l, sp_l, ks_l, vs_l, ss_l = [], [], [], [], [], []
    for layer in range(DEPTH):
        hp, kn, vn, sn = _prompt_layer(layer, hp, pp, rope_p, sinks, weights, weights_t)
        kp_l.append(kn); vp_l.append(vn); sp_l.append(sn)
        hs, kn, vn, sn = _sample_layer(layer, hs, ps, rope_s, ck, cv, s0, sinks, weights, dec_seq)
        ks_l.append(kn); vs_l.append(vn); ss_l.append(sn)

    return (hp.reshape(batch, seq, D_MODEL),
            hs.reshape(n_seq, dec_seq, D_MODEL),
            jnp.stack(kp_l).reshape(DEPTH, batch, WINDOW, A_KV_HEADS, A_HEAD_DIM),
            jnp.stack(vp_l).reshape(DEPTH, batch, WINDOW, A_KV_HEADS, A_HEAD_DIM),
            jnp.stack(sp_l).reshape(DEPTH, batch, B_HEADS, B_KEY_DIM, B_VAL_DIM),
            jnp.stack(ks_l).reshape(DEPTH, n_seq, dec_seq, A_KV_HEADS, A_HEAD_DIM),
            jnp.stack(vs_l).reshape(DEPTH, n_seq, dec_seq, A_KV_HEADS, A_HEAD_DIM),
            jnp.stack(ss_l).reshape(DEPTH, n_seq, B_HEADS, B_KEY_DIM, B_VAL_DIM))
```

```python
import functools

import jax
import jax.numpy as jnp
from jax import lax
from jax.experimental import pallas as pl
from jax.experimental.pallas import tpu as pltpu

F32 = jnp.float32
BF16 = jnp.bfloat16

D_MODEL = 1024
DEPTH = 4
PAST_LEN = 1024
CHUNK = 64
D_PLE = 256
A_HEADS = 8
A_KV_HEADS = 2
A_HEAD_DIM = 64
A_GROUP = A_HEADS // A_KV_HEADS
A_WIDTH = A_HEADS * A_HEAD_DIM
A_KV_WIDTH = A_KV_HEADS * A_HEAD_DIM
WINDOW = 128
ROT_DIM = A_HEAD_DIM // 4
ROPE_THETA = 500000.0
B_HEADS = 4
B_KEY_DIM = 64
B_VAL_DIM = 128
B_QK_WIDTH = B_HEADS * B_KEY_DIM
B_WIDTH = B_HEADS * B_VAL_DIM
GATE_RANK = 16
GATE_TAU = 16.0
NEG_INF = -1e30
EPS = 1e-6

LANES = 128
MXU_TILE = 256

C_QA = 0
C_KA = C_QA + A_WIDTH
C_VA = C_KA + A_KV_WIDTH
C_GA = C_VA + A_KV_WIDTH
C_QB = C_GA + A_WIDTH
C_KB = C_QB + B_QK_WIDTH
C_VB = C_KB + B_QK_WIDTH
C_GB = C_VB + B_WIDTH
C_AB = C_GB + B_WIDTH
IN_COLS = C_AB + GATE_RANK

PROMPT_TILE = 512
GLA_SPAN = MXU_TILE
GLA_SUB = 16
GLA_FAST_MAX_DECAY = 60.0
FINISH_ROWS = 512
SAMPLE_BATCH_TILE = 16
VMEM_LIMIT_BYTES = 56 * 1024 * 1024


def _dot(a, b):
    return jnp.dot(a, b, preferred_element_type=F32)


def _dot_nt(a, b):
    return lax.dot_general(a, b, (((1,), (1,)), ((), ())), preferred_element_type=F32)


def _dot_tn(a, b):
    return lax.dot_general(a, b, (((0,), (0,)), ((), ())), preferred_element_type=F32)


def _proj(xn, wint_ref, c0, c1):
    return _dot_nt(xn, wint_ref[c0:c1, :])


def _split_bf16(x):
    hi = x.astype(BF16)
    lo = (x - hi.astype(F32)).astype(BF16)
    return hi, lo


def _rms(x, g):
    ms = jnp.mean(x * x, axis=-1, keepdims=True)
    return x * lax.rsqrt(ms + EPS) * g


def _sigmoid(x):
    return 0.5 + 0.5 * jnp.tanh(0.5 * x)


def _silu(x):
    return x * _sigmoid(x)


def _log_sigmoid(x):
    return jnp.minimum(x, 0.0) - jnp.log(1.0 + jnp.exp(-jnp.abs(x)))


def _block_mask(n, block, lower):
    r = lax.broadcasted_iota(jnp.int32, (n, n), 0)
    c = lax.broadcasted_iota(jnp.int32, (n, n), 1)
    m = (r // block) == (c // block)
    if lower:
        m = m & (r >= c)
    return jnp.where(m, 1.0, 0.0).astype(BF16)


def _project_a(xn, tab, win_ref, gqk_ref):
    return _norm_rope(_proj(xn, win_ref, C_QA, C_GA), tab, gqk_ref)


def _norm_rope(qkv, tab, gqk_ref):
    qk, v = qkv[:, :C_VA], qkv[:, C_VA:]
    seg = _block_mask(MXU_TILE, A_HEAD_DIM, lower=False)
    hi, lo = _split_bf16(qk * qk)
    parts = []
    for c0 in range(0, C_VA, MXU_TILE):
        c1 = min(c0 + MXU_TILE, C_VA)
        sg = seg[: c1 - c0, : c1 - c0]
        parts.append(_dot(hi[:, c0:c1], sg) + _dot(lo[:, c0:c1], sg))
    ss = jnp.concatenate(parts, axis=1)
    qkn = qk * lax.rsqrt(ss * (1.0 / A_HEAD_DIM) + EPS) * gqk_ref[...]
    head_lane = lax.broadcasted_iota(jnp.int32, (1, LANES), 1) % A_HEAD_DIM
    cos = jnp.where(head_lane < ROT_DIM, tab, 1.0)
    sin = jnp.where(head_lane < ROT_DIM, pltpu.roll(tab, LANES - ROT_DIM, 1), 0.0)
    first_half = head_lane < ROT_DIM // 2
    slabs = []
    for s in range(C_VA // LANES):
        x = qkn[:, s * LANES:(s + 1) * LANES]
        up = pltpu.roll(x, LANES - ROT_DIM // 2, 1)
        dn = pltpu.roll(x, ROT_DIM // 2, 1)
        slabs.append(x * cos + jnp.where(first_half, up, dn) * sin)
    q = jnp.concatenate(slabs[:A_WIDTH // LANES], axis=1)
    k = slabs[A_WIDTH // LANES]
    return q, k, v


def _kv_operands(k, v, lo_half):
    ksw = pltpu.roll(k, A_HEAD_DIM, 1)
    vsw = pltpu.roll(v, A_HEAD_DIM, 1)
    one = jnp.ones_like(v)
    kd = (jnp.where(lo_half, k, ksw).astype(BF16), jnp.where(lo_half, ksw, k).astype(BF16))
    vx = (jnp.concatenate([jnp.where(lo_half, v, one), jnp.where(lo_half, one, vsw)], axis=1).astype(BF16),
          jnp.concatenate([jnp.where(lo_half, vsw, one), jnp.where(lo_half, one, v)], axis=1).astype(BF16))
    return kd, vx


def _query_stack(q_blk, j, lo_half):
    a = q_blk[:, (2 * j) * LANES:(2 * j + 1) * LANES]
    b = q_blk[:, (2 * j + 1) * LANES:(2 * j + 2) * LANES]
    z = jnp.zeros_like(a)
    return jnp.concatenate([jnp.where(lo_half, a, z), jnp.where(lo_half, z, a),
                            jnp.where(lo_half, b, z), jnp.where(lo_half, z, b)], axis=0).astype(BF16)


def _attn_scores(q_blk, j, kd, lo_half):
    return _dot_nt(_query_stack(q_blk, j, lo_half), kd)


def _attn_probs(s, j, bias, sink_ref, layer):
    rows = s.shape[0] // A_GROUP
    if bias is not None:
        s = s + bias
    sk = jnp.concatenate([jnp.full((rows, 1), sink_ref[layer, A_GROUP * j + g], F32) for g in range(A_GROUP)],
                         axis=0)
    m = jnp.maximum(jnp.max(s, axis=-1, keepdims=True), sk)
    return jnp.exp(s - m).astype(BF16), jnp.exp(sk - m)


def _attn_values(p, de, vx, lo_half):
    return _attn_slabs(_dot(p, vx), de, lo_half)


def _attn_slabs(r, de, lo_half):
    rows = r.shape[0] // A_GROUP
    r_lo, r_hi = r[:, :LANES], r[:, LANES:]

    def slab(g0):
        e = slice(g0 * rows, (g0 + 1) * rows)
        o = slice((g0 + 1) * rows, (g0 + 2) * rows)
        return jnp.where(lo_half, r_lo[e] / (r_hi[e] + de[e]), r_hi[o] / (r_lo[o] + de[o]))

    return slab(0), slab(2)


def _gla_inputs(xn, win_ref, wgu_ref, bg_ref, block):
    rows = xn.shape[0]
    qb = _proj(xn, win_ref, C_QB, C_KB)
    kb = _proj(xn, win_ref, C_KB, C_VB)
    vb = _proj(xn, win_ref, C_VB, C_GB)
    ab = _proj(xn, win_ref, C_AB, IN_COLS)
    logg = _log_sigmoid(_dot(ab.astype(BF16), wgu_ref[...]) + bg_ref[...]) * (1.0 / GATE_TAU)
    g_hi, g_lo = _split_bf16(logg)
    tri = _block_mask(rows, block, lower=True)
    ones = _block_mask(rows, block, lower=False)
    b = _dot(tri, g_hi) + _dot(tri, g_lo)
    b_last = _dot(ones, g_hi) + _dot(ones, g_lo)
    qt = qb * (jnp.exp(b) * (B_KEY_DIM ** -0.5))
    kt = kb * jnp.exp(-b)
    ke = kb * jnp.exp(b_last - b)
    return qt, kt, ke, vb, logg


def _head_stack(x, head_mask):
    z = jnp.zeros_like(x)
    return jnp.concatenate([jnp.where(head_mask[h], x, z) for h in range(B_HEADS)], axis=0).astype(BF16)


def _finish_layer(h, oa, ob, xn, p, win_ref, glag_ref, wout_ref, peg_ref, wpe_ref, wpg_ref):
    ga = _proj(xn, win_ref, C_GA, C_QB)
    gb = _proj(xn, win_ref, C_GB, C_AB)
    obn = jnp.concatenate(
        [_rms(ob[:, hh * B_VAL_DIM:(hh + 1) * B_VAL_DIM], glag_ref[...]) for hh in range(B_HEADS)], axis=1)
    mix = jnp.concatenate([oa * _silu(ga), obn * _silu(gb)], axis=1).astype(BF16)
    h1 = h + _dot(mix, wout_ref[...])
    gate = _sigmoid(_dot(_rms(h1, peg_ref[...]).astype(BF16), wpg_ref[...]))
    return h1 + gate * _dot(p.astype(BF16), wpe_ref[...])


def _gla_decays(kbt, abt, wgut_ref, bgc_ref):
    tile = kbt.shape[1]
    loggt = _log_sigmoid(_dot(wgut_ref[...], abt) + bgc_ref[...]) * (1.0 / GATE_TAU)
    g_hi, g_lo = _split_bf16(loggt)
    r = lax.broadcasted_iota(jnp.int32, (GLA_SPAN, GLA_SPAN), 0)
    c = lax.broadcasted_iota(jnp.int32, (GLA_SPAN, GLA_SPAN), 1)
    upper = jnp.where(((r // CHUNK) == (c // CHUNK)) & (r <= c), 1.0, 0.0).astype(BF16)
    bt = jnp.concatenate(
        [_dot(g_hi[:, s0:s0 + GLA_SPAN], upper) + _dot(g_lo[:, s0:s0 + GLA_SPAN], upper)
         for s0 in range(0, tile, GLA_SPAN)], axis=1)
    ktt = (kbt * jnp.exp(-bt)).astype(BF16)
    pair = 2 * CHUNK
    cols = [bt[:, (c + 1) * CHUNK - 1:(c + 1) * CHUNK] for c in range(tile // CHUNK)]
    lane = lax.broadcasted_iota(jnp.int32, (1, pair), 1)
    ket = jnp.concatenate(
        [kbt[:, pc * pair:(pc + 1) * pair]
         * jnp.exp(jnp.where(lane < CHUNK, cols[2 * pc], cols[2 * pc + 1]) - bt[:, pc * pair:(pc + 1) * pair])
         for pc in range(tile // pair)], axis=1)
    return bt, ktt, ket, cols


def _gla_chunk_jobs(qb, qt, vb, kbt, bt, ktt, ket, cols, stk, out):
    tile = qt.shape[0]
    n_chunks = tile // CHUNK
    pair = 2 * CHUNK
    head_of_lane = lax.broadcasted_iota(jnp.int32, (1, B_QK_WIDTH), 1) // B_KEY_DIM
    head_mask = [head_of_lane == hh for hh in range(B_HEADS)]
    lane = lax.broadcasted_iota(jnp.int32, (1, pair), 1)
    t_row = lax.broadcasted_iota(jnp.int32, (CHUNK, pair), 0)
    s_col = lax.broadcasted_iota(jnp.int32, (CHUNK, pair), 1)
    own = [lane < CHUNK, lane >= CHUNK]
    causal = [s_col <= t_row, (s_col >= CHUNK) & (s_col - CHUNK <= t_row)]
    v_heads = [[vb[(c // 2) * pair:(c // 2 + 1) * pair, hh * B_VAL_DIM:(hh + 1) * B_VAL_DIM]
                for hh in range(B_HEADS)] for c in range(n_chunks)]
    updates, states, results, ob_rows = {}, {}, {}, {}

    def update(c):
        ps = slice((c // 2) * pair, (c // 2 + 1) * pair)
        ket_c = jnp.where(own[c % 2], ket[:, ps], 0.0).astype(BF16)
        updates[c] = jnp.concatenate(
            [_dot(ket_c[hh * B_KEY_DIM:(hh + 1) * B_KEY_DIM], v_heads[c][hh]) for hh in range(B_HEADS)], axis=0)

    def chain():
        s = stk
        for c in range(n_chunks):
            states[c] = s
            s = s * jnp.exp(cols[c]) + updates[c]
        out["state"] = s

    def result(c):
        ps = slice((c // 2) * pair, (c // 2 + 1) * pair)
        q_stack = _head_stack(qt[c * CHUNK:(c + 1) * CHUNK], head_mask)
        results[c] = _dot(q_stack, jnp.concatenate([states[c].astype(BF16), ktt[:, ps]], axis=1))

    def chunk_output(c, res):
        outs = []
        for hh in range(B_HEADS):
            hr = slice(hh * CHUNK, (hh + 1) * CHUNK)
            a_h = jnp.where(causal[c % 2], res[hr, B_VAL_DIM:], 0.0).astype(BF16)
            outs.append(_dot(a_h, v_heads[c][hh]) + res[hr, :B_VAL_DIM])
        return jnp.concatenate(outs, axis=1)

    def output(c):
        ob_rows[c] = chunk_output(c, results.pop(c))
        if c == n_chunks - 1:
            out["ob"] = jnp.concatenate([ob_rows[cc] for cc in range(n_chunks)], axis=0)

    def wide_range_ob():
        n_sub = CHUNK // GLA_SUB
        rows = []
        for c in range(n_chunks):
            ps = slice((c // 2) * pair, (c // 2 + 1) * pair)
            first = (c % 2) * CHUNK
            bt_c = bt[:, c * CHUNK:(c + 1) * CHUNK]
            res = []
            for i in range(n_sub):
                r0 = c * CHUNK + i * GLA_SUB
                ref = bt_c[:, i * GLA_SUB - 1:i * GLA_SUB] if i else jnp.zeros((B_QK_WIDTH, 1), F32)
                rel_q = (bt_c[:, i * GLA_SUB:(i + 1) * GLA_SUB] - ref).T
                q_sub = qb[r0:r0 + GLA_SUB] * (jnp.exp(rel_q) * (B_KEY_DIM ** -0.5))
                visible = (lane >= first) & (lane < first + (i + 1) * GLA_SUB)
                keys = kbt[:, ps] * jnp.exp(jnp.where(visible, ref - bt[:, ps], NEG_INF))
                state = (states[c] * jnp.exp(ref)).astype(BF16)
                res.append(_dot(_head_stack(q_sub, head_mask),
                                jnp.concatenate([state, keys.astype(BF16)], axis=1)))
            rows.append(chunk_output(c, jnp.concatenate(
                [res[i][hh * GLA_SUB:(hh + 1) * GLA_SUB] for hh in range(B_HEADS) for i in range(n_sub)], axis=0)))
        return jnp.concatenate(rows, axis=0)

    out["wide_range_ob"] = wide_range_ob

    bind = lambda f, c: (lambda: f(c))
    return ([bind(update, c) for c in range(n_chunks)] + [chain]
            + [bind(result, c) for c in range(n_chunks)] + [bind(output, c) for c in range(n_chunks)])


def _trace_interleaved(jobs_a, jobs_b):
    na, nb = len(jobs_a), len(jobs_b)
    ia = ib = 0
    while ia < na or ib < nb:
        if ib >= nb or (ia < na and ia * nb <= ib * na):
            jobs_a[ia]()
            ia += 1
        else:
            jobs_b[ib]()
            ib += 1


def _prompt_kernel(sink_ref, h_ref, p_ref, rope_ref, ng_ref, win_ref, gqk_ref, wgu_ref, bg_ref, glag_ref,
                   wout_ref, peg_ref, wpe_ref, wpg_ref, wgut_ref, bgc_ref,
                   ho_ref, nk_ref, nv_ref, st_ref,
                   kd_scr, vx_scr, st_scr, *, layer):
    tile = PROMPT_TILE
    n_chunks = tile // CHUNK
    band = WINDOW + CHUNK
    i = pl.program_id(0)
    rd = (i + 1) % 2
    wr = i % 2

    @pl.when(i == 0)
    def _():
        kd_scr[rd] = jnp.zeros(kd_scr.shape[1:], BF16)
        vx_scr[rd] = jnp.zeros(vx_scr.shape[1:], BF16)
        st_scr[rd] = jnp.zeros(st_scr.shape[1:], F32)

    h = h_ref[...]
    xn = _rms(h, ng_ref[...]).astype(BF16)
    lo_half = lax.broadcasted_iota(jnp.int32, (1, LANES), 1) < A_HEAD_DIM
    key_blk = lax.broadcasted_iota(jnp.int32, (1, band), 1) // CHUNK
    blocks = [(c, j) for c in range(n_chunks) for j in range(A_KV_HEADS)]
    n_blocks = len(blocks)
    v, scores, probs, slabs, gla = {}, {}, {}, {}, {}
    bind = lambda f, n: (lambda: f(n))

    qkv = _proj(xn, win_ref, C_QA, C_GA)

    def rope_tab():
        tab = rope_ref[...].T
        head_lane = lax.broadcasted_iota(jnp.int32, (1, LANES), 1) % A_HEAD_DIM
        v["cos"] = jnp.where(head_lane < ROT_DIM, tab, 1.0)
        v["sin"] = jnp.where(head_lane < ROT_DIM, pltpu.roll(tab, LANES - ROT_DIM, 1), 0.0)
        v["first_half"] = head_lane < ROT_DIM // 2

    def norm(g):
        width = min(MXU_TILE, C_VA - MXU_TILE * g)
        x = qkv[:, MXU_TILE * g:MXU_TILE * g + width]
        seg = _block_mask(width, A_HEAD_DIM, lower=False)
        hi, lo = _split_bf16(x * x)
        ss = _dot(hi, seg) + _dot(lo, seg)
        xs = x * lax.rsqrt(ss * (1.0 / A_HEAD_DIM) + EPS) * gqk_ref[:, MXU_TILE * g:MXU_TILE * g + width]
        for s in range(width // LANES):
            x1 = xs[:, s * LANES:(s + 1) * LANES]
            up = pltpu.roll(x1, LANES - ROT_DIM // 2, 1)
            dn = pltpu.roll(x1, ROT_DIM // 2, 1)
            v[f"slab{2 * g + s}"] = x1 * v["cos"] + jnp.where(v["first_half"], up, dn) * v["sin"]

    def kv_ops():
        k, vv = v["slab4"], qkv[:, C_VA:]
        nk_ref[...] = k[tile - WINDOW:, :]
        nv_ref[...] = vv[tile - WINDOW:, :]
        kd, vx = _kv_operands(k, vv, lo_half)
        v["kd_ext"] = [jnp.concatenate([kd_scr[rd, j], kd[j]], axis=0) for j in range(A_KV_HEADS)]
        v["vx_ext"] = [jnp.concatenate([vx_scr[rd, j], vx[j]], axis=0) for j in range(A_KV_HEADS)]
        for j in range(A_KV_HEADS):
            kd_scr[wr, j] = kd[j][tile - WINDOW:, :]
            vx_scr[wr, j] = vx[j][tile - WINDOW:, :]

    def proj_k():
        v["kbt"] = _dot_nt(win_ref[C_KB:C_VB, :], xn)
        v["abt"] = _dot_nt(win_ref[C_AB:IN_COLS, :], xn).astype(BF16)

    def proj_qv():
        v["qb"] = _proj(xn, win_ref, C_QB, C_KB)
        v["vb"] = _proj(xn, win_ref, C_VB, C_GB).astype(BF16)

    _trace_interleaved([rope_tab, bind(norm, 2), kv_ops, bind(norm, 0), bind(norm, 1)], [proj_k, proj_qv])

    def score(n):
        c, j = blocks[n]
        cs = slice(c * CHUNK, (c + 1) * CHUNK)
        q_blk = jnp.concatenate([v[f"slab{2 * j}"][cs], v[f"slab{2 * j + 1}"][cs]], axis=1)
        scores[n] = _attn_scores(q_blk, 0, v["kd_ext"][j][c * CHUNK:c * CHUNK + band, :], lo_half)

    def decays():
        gla["bt"], gla["ktt"], gla["ket"], gla["cols"] = _gla_decays(v["kbt"], v["abt"], wgut_ref, bgc_ref)

    def scale_q():
        gla["qt"] = v["qb"] * (jnp.exp(gla["bt"].T) * (B_KEY_DIM ** -0.5))

    _trace_interleaved([bind(score, n) for n in range(n_blocks)], [decays, scale_q])

    def prob(n):
        c, j = blocks[n]
        first_key_chunk = i * n_chunks + c - WINDOW // CHUNK
        bias = jnp.where(key_blk + first_key_chunk >= 0, 0.0, NEG_INF)
        probs[n] = _attn_probs(scores.pop(n), j, bias, sink_ref, layer)

    def value(n):
        c, j = blocks[n]
        slabs[n] = _attn_values(*probs.pop(n), v["vx_ext"][j][c * CHUNK:c * CHUNK + band, :], lo_half)

    def proj_ga():
        v["ga"] = _proj(xn, win_ref, C_GA, C_QB)

    def proj_gb():
        v["gb"] = _proj(xn, win_ref, C_GB, C_AB)

    def embed():
        v["pe"] = _dot(p_ref[...].astype(BF16), wpe_ref[...])

    gla_jobs = _gla_chunk_jobs(v["qb"], gla["qt"], v["vb"], v["kbt"], gla["bt"], gla["ktt"], gla["ket"], gla["cols"],
                               st_scr[rd], gla)
    n_first = n_chunks + 1
    _trace_interleaved([bind(prob, n) for n in range(n_blocks)], [proj_ga, proj_gb] + gla_jobs[:n_first])
    _trace_interleaved([bind(value, n) for n in range(n_blocks)], [embed] + gla_jobs[n_first:])
    st_scr[wr] = gla["state"]
    st_ref[...] = gla["state"]
    ob, pe, ga, gb = gla["ob"], v["pe"], v["ga"], v["gb"]

    def finish(ob):
        for r0 in range(0, tile, FINISH_ROWS):
            rs = slice(r0, r0 + FINISH_ROWS)
            oa = jnp.concatenate(
                [jnp.concatenate([sl for j in range(A_KV_HEADS) for sl in slabs[c * A_KV_HEADS + j]], axis=1)
                 for c in range(r0 // CHUNK, (r0 + FINISH_ROWS) // CHUNK)], axis=0)
            h1 = h[rs] + _dot((oa * _silu(ga[rs])).astype(BF16), wout_ref[0:A_WIDTH, :])
            obn = jnp.concatenate(
                [_rms(ob[rs, hh * B_VAL_DIM:(hh + 1) * B_VAL_DIM], glag_ref[...]) for hh in range(B_HEADS)], axis=1)
            h2 = h1 + _dot((obn * _silu(gb[rs])).astype(BF16), wout_ref[A_WIDTH:, :])
            gate = _sigmoid(_dot(_rms(h2, peg_ref[...]).astype(BF16), wpg_ref[...]))
            ho_ref[rs, :] = h2 + gate * pe[rs]

    finish(ob)

    strongest = gla["cols"][0]
    for c in range(1, n_chunks):
        strongest = jnp.minimum(strongest, gla["cols"][c])

    @pl.when(jnp.min(strongest) < -GLA_FAST_MAX_DECAY)
    def _():
        finish(gla["wide_range_ob"]())


def _sample_kernel(sink_ref, h_ref, p_ref, rope_ref, ck_ref, cv_ref, s0_ref, ng_ref, win_ref, gqk_ref, wgu_ref,
                   bg_ref, glag_ref, wout_ref, peg_ref, wpe_ref, wpg_ref,
                   ho_ref, nk_ref, nv_ref, st_ref, *, layer, dec_seq):
    nb = SAMPLE_BATCH_TILE
    h = h_ref[...]
    xn = _rms(h, ng_ref[...]).astype(BF16)
    lo_half = lax.broadcasted_iota(jnp.int32, (1, LANES), 1) < A_HEAD_DIM

    q, k, v = _project_a(xn, rope_ref[...].T, win_ref, gqk_ref)
    nk_ref[...] = k
    nv_ref[...] = v
    kd_new, vx_new = _kv_operands(k, v, lo_half)
    rows_of = [slice(bi * dec_seq, (bi + 1) * dec_seq) for bi in range(nb)]
    blocks = [(bi, j) for bi in range(nb) for j in range(A_KV_HEADS)]
    stacks = [_query_stack(q[rows_of[bi]], j, lo_half) for bi, j in blocks]
    scores = []
    for (bi, j), st in zip(blocks, stacks):
        kt_old = ck_ref[bi, j].astype(BF16)
        scores.append(jnp.concatenate([_dot(st, jnp.concatenate([kt_old, kt_old], axis=0)),
                                       _dot_nt(st, kd_new[j][rows_of[bi]])], axis=1))
    probs = [_attn_probs(s, j, None, sink_ref, layer) for (bi, j), s in zip(blocks, scores)]
    ones = jnp.ones((A_HEAD_DIM, WINDOW), BF16)
    slabs = []
    for (bi, j), (p, de) in zip(blocks, probs):
        vt_old = cv_ref[bi, j].astype(BF16)
        r = (_dot_nt(p[:, :WINDOW], jnp.concatenate([vt_old, ones, ones, vt_old], axis=0))
             + _dot(p[:, WINDOW:], vx_new[j][rows_of[bi]]))
        slabs.append(_attn_slabs(r, de, lo_half))
    oa = jnp.concatenate(
        [jnp.concatenate([sl for j in range(A_KV_HEADS) for sl in slabs[bi * A_KV_HEADS + j]], axis=1)
         for bi in range(nb)], axis=0)

    qt, kt, ke, vb, g_all = _gla_inputs(xn, win_ref, wgu_ref, bg_ref, dec_seq)
    head_of_lane = lax.broadcasted_iota(jnp.int32, (1, B_QK_WIDTH), 1) // B_KEY_DIM
    head_mask = [head_of_lane == hh for hh in range(B_HEADS)]
    causal = (lax.broadcasted_iota(jnp.int32, (dec_seq, dec_seq), 0)
              >= lax.broadcasted_iota(jnp.int32, (dec_seq, dec_seq), 1))
    ones_rhs = jnp.ones((dec_seq, LANES), BF16)
    vb = vb.astype(BF16)
    g_hi, g_lo = _split_bf16(g_all)
    q_stacks = [_head_stack(qt[rs], head_mask) for rs in rows_of]
    atts = [_dot_nt(q_stacks[bi], kt[rows_of[bi]].astype(BF16)) for bi in range(nb)]
    inters = [_dot(q_stacks[bi], s0_ref[bi].astype(BF16)) for bi in range(nb)]
    totals = [_dot_tn(g_hi[rs], ones_rhs) + _dot_tn(g_lo[rs], ones_rhs) for rs in rows_of]
    upd_fulls = [_dot_tn(ke[rs].astype(BF16), vb[rs]) for rs in rows_of]
    atts = [jnp.concatenate([jnp.where(causal, a[hh * dec_seq:(hh + 1) * dec_seq], 0.0)
                             for hh in range(B_HEADS)], axis=0).astype(BF16) for a in atts]
    intras = [_dot(atts[bi], vb[rows_of[bi]]) for bi in range(nb)]
    ob = jnp.concatenate(
        [jnp.concatenate([intras[bi][hh * dec_seq:(hh + 1) * dec_seq, hh * B_VAL_DIM:(hh + 1) * B_VAL_DIM]
                          + inters[bi][hh * dec_seq:(hh + 1) * dec_seq] for hh in range(B_HEADS)], axis=1)
         for bi in range(nb)], axis=0)
    for bi in range(nb):
        upd = jnp.concatenate(
            [upd_fulls[bi][hh * B_KEY_DIM:(hh + 1) * B_KEY_DIM, hh * B_VAL_DIM:(hh + 1) * B_VAL_DIM]
             for hh in range(B_HEADS)], axis=0)
        st_ref[bi] = s0_ref[bi] * jnp.exp(totals[bi]) + upd

    ho_ref[...] = _finish_layer(h, oa, ob, xn, p_ref[...], win_ref, glag_ref, wout_ref, peg_ref,
                                wpe_ref, wpg_ref)


def _rope_table(pos):
    half = ROT_DIM // 2
    inv = jnp.power(jnp.float32(ROPE_THETA), -jnp.arange(half, dtype=F32) * (2.0 / ROT_DIM))
    ang = inv[:, None] * pos.astype(F32)[None, :]
    reps = A_HEAD_DIM // half
    cos, sin = jnp.tile(jnp.cos(ang), (reps, 1)), jnp.tile(jnp.sin(ang), (reps, 1))
    row = lax.broadcasted_iota(jnp.int32, (A_HEAD_DIM, 1), 0)
    head = jnp.where(row < ROT_DIM, cos,
                     jnp.where(row < ROT_DIM + half, -sin, jnp.where(row < 2 * ROT_DIM, sin, 0.0)))
    return jnp.tile(head, (LANES // A_HEAD_DIM, 1))


def _layer_spec(layer, shape):
    return pl.BlockSpec((None,) + shape, lambda i: (layer,) + (0,) * len(shape))


def _weight_specs(layer):
    return [
        _layer_spec(layer, (1, D_MODEL)),
        _layer_spec(layer, (IN_COLS, D_MODEL)),
        _layer_spec(layer, (1, C_VA)),
        _layer_spec(layer, (GATE_RANK, B_QK_WIDTH)),
        _layer_spec(layer, (1, B_QK_WIDTH)),
        _layer_spec(layer, (1, B_VAL_DIM)),
        _layer_spec(layer, (A_WIDTH + B_WIDTH, D_MODEL)),
        _layer_spec(layer, (1, D_MODEL)),
        _layer_spec(layer, (D_PLE, D_MODEL)),
        _layer_spec(layer, (D_MODEL, D_MODEL)),
    ]


def _prompt_layer(layer, h, p, rope, sinks, weights, weights_t):
    seq = h.shape[0]
    tile = PROMPT_TILE
    assert seq % tile == 0 and tile % GLA_SPAN == 0 and GLA_SPAN % (2 * CHUNK) == 0 and tile >= WINDOW
    row = lambda w: pl.BlockSpec((tile, w), lambda i: (i, 0))
    const = lambda shape: pl.BlockSpec(shape, lambda i: (0,) * len(shape))
    return pl.pallas_call(
        functools.partial(_prompt_kernel, layer=layer),
        grid=(seq // tile,),
        in_specs=[pl.BlockSpec(memory_space=pltpu.SMEM), row(D_MODEL),
                  pl.BlockSpec((None, tile, D_PLE), lambda i: (layer, i, 0)),
                  pl.BlockSpec((LANES, tile), lambda i: (0, i))]
        + _weight_specs(layer)
        + [_layer_spec(layer, (B_QK_WIDTH, GATE_RANK)),
           _layer_spec(layer, (B_QK_WIDTH, 1))],
        out_specs=[row(D_MODEL), const((WINDOW, A_KV_WIDTH)), const((WINDOW, A_KV_WIDTH)),
                   const((B_QK_WIDTH, B_VAL_DIM))],
        out_shape=[jax.ShapeDtypeStruct((seq, D_MODEL), F32),
                   jax.ShapeDtypeStruct((WINDOW, A_KV_WIDTH), F32),
                   jax.ShapeDtypeStruct((WINDOW, A_KV_WIDTH), F32),
                   jax.ShapeDtypeStruct((B_QK_WIDTH, B_VAL_DIM), F32)],
        scratch_shapes=[pltpu.VMEM((2, A_KV_HEADS, WINDOW, LANES), BF16),
                        pltpu.VMEM((2, A_KV_HEADS, WINDOW, 2 * LANES), BF16),
                        pltpu.VMEM((2, B_QK_WIDTH, B_VAL_DIM), F32)],
        compiler_params=pltpu.CompilerParams(dimension_semantics=("arbitrary",),
                                             vmem_limit_bytes=VMEM_LIMIT_BYTES),
        name="prompt_layer",
    )(sinks, h, p, rope, *weights, *weights_t)


def _sample_layer(layer, h, p, rope, ck, cv, s0, sinks, weights, dec_seq):
    n_seq = ck.shape[1]
    nb = SAMPLE_BATCH_TILE
    assert n_seq % nb == 0
    rows = nb * dec_seq
    row = lambda w: pl.BlockSpec((rows, w), lambda i: (i, 0))
    per_seq_in = lambda a, b: pl.BlockSpec((None, nb, a, b), lambda i: (layer, i, 0, 0))
    cache_spec = pl.BlockSpec((None, nb, A_KV_HEADS, A_HEAD_DIM, WINDOW), lambda i: (layer, i, 0, 0, 0))
    return pl.pallas_call(
        functools.partial(_sample_kernel, layer=layer, dec_seq=dec_seq),
        grid=(n_seq // nb,),
        in_specs=[pl.BlockSpec(memory_space=pltpu.SMEM), row(D_MODEL),
                  pl.BlockSpec((None, rows, D_PLE), lambda i: (layer, i, 0)),
                  pl.BlockSpec((LANES, rows), lambda i: (0, 0)),
                  cache_spec, cache_spec, per_seq_in(B_QK_WIDTH, B_VAL_DIM)]
        + _weight_specs(layer),
        out_specs=[row(D_MODEL), row(A_KV_WIDTH), row(A_KV_WIDTH),
                   pl.BlockSpec((nb, B_QK_WIDTH, B_VAL_DIM), lambda i: (i, 0, 0))],
        out_shape=[jax.ShapeDtypeStruct((n_seq * dec_seq, D_MODEL), F32),
                   jax.ShapeDtypeStruct((n_seq * dec_seq, A_KV_WIDTH), F32),
                   jax.ShapeDtypeStruct((n_seq * dec_seq, A_KV_WIDTH), F32),
                   jax.ShapeDtypeStruct((n_seq, B_QK_WIDTH, B_VAL_DIM), F32)],
        compiler_params=pltpu.CompilerParams(dimension_semantics=("arbitrary",),
                                             vmem_limit_bytes=VMEM_LIMIT_BYTES),
        name="sample_layer",
    )(sinks, h, p, rope, ck, cv, s0, *weights)


def kernel(x_prompt, x_sample, cache_k, cache_v, state_gla, p_prompt, p_sample, norm_g, w_in, q_norm_g, k_norm_g,
           sinks, w_gate_up, b_gate, gla_norm_g, w_out, pe_norm_g, w_pe, w_pg):
    batch, seq, _ = x_prompt.shape
    n_seq, dec_seq, _ = x_sample.shape
    assert batch == 1 and cache_k.shape[2] == WINDOW

    rope_p = _rope_table(jnp.arange(seq, dtype=jnp.int32))
    rope_s = jnp.tile(_rope_table(PAST_LEN + jnp.arange(dec_seq, dtype=jnp.int32)), (1, SAMPLE_BATCH_TILE))

    wgu_b = w_gate_up.astype(BF16)
    gqk = jnp.concatenate([jnp.tile(q_norm_g, (1, A_HEADS)) * (A_HEAD_DIM ** -0.5),
                           jnp.tile(k_norm_g, (1, A_KV_HEADS))], axis=1)
    vec = lambda a: a[:, None, :]
    weights = (vec(norm_g), jnp.swapaxes(w_in, 1, 2).astype(BF16), vec(gqk), wgu_b, vec(b_gate), vec(gla_norm_g), w_out.astype(BF16),
               vec(pe_norm_g), w_pe.astype(BF16), w_pg.astype(BF16))
    weights_t = (jnp.swapaxes(w_gate_up, 1, 2).astype(BF16),
                 b_gate[:, :, None])

    hp = x_prompt.reshape(seq, D_MODEL)
    hs = x_sample.reshape(n_seq * dec_seq, D_MODEL)
    ck = jnp.transpose(cache_k, (0, 1, 3, 4, 2))
    cv = jnp.transpose(cache_v, (0, 1, 3, 4, 2))
    s0 = state_gla.reshape(DEPTH, n_seq, B_QK_WIDTH, B_VAL_DIM)
    pp = p_prompt.reshape(DEPTH, seq, D_PLE)
    ps = p_sample.reshape(DEPTH, n_seq * dec_seq, D_PLE)

    kp_l, vp_l, sp_l, ks_l, vs_l, ss_l = [], [], [], [], [], []
    for layer in range(DEPTH):
        hp, kn, vn, sn = _prompt_layer(layer, hp, pp, rope_p, sinks, weights, weights_t)
        kp_l.append(kn); vp_l.append(vn); sp_l.append(sn)
        hs, kn, vn, sn = _sample_layer(layer, hs, ps, rope_s, ck, cv, s0, sinks, weights, dec_seq)
        ks_l.append(kn); vs_l.append(vn); ss_l.append(sn)

    return (hp.reshape(batch, seq, D_MODEL),
            hs.reshape(n_seq, dec_seq, D_MODEL),
            jnp.stack(kp_l).reshape(DEPTH, batch, WINDOW, A_KV_HEADS, A_HEAD_DIM),
            jnp.stack(vp_l).reshape(DEPTH, batch, WINDOW, A_KV_HEADS, A_HEAD_DIM),
            jnp.stack(sp_l).reshape(DEPTH, batch, B_HEADS, B_KEY_DIM, B_VAL_DIM),
            jnp.stack(ks_l).reshape(DEPTH, n_seq, dec_seq, A_KV_HEADS, A_HEAD_DIM),
            jnp.stack(vs_l).reshape(DEPTH, n_seq, dec_seq, A_KV_HEADS, A_HEAD_DIM),
            jnp.stack(ss_l).reshape(DEPTH, n_seq, B_HEADS, B_KEY_DIM, B_VAL_DIM))
```
